```python
import math
import jax, jax.numpy as jnp
from jax import lax
import numpy as np

D_MODEL = 2048
BATCH = 4
SEQ = 2048
DEPTH = 1
DEC_BATCH = 128
DEC_SEQ = 8
PAST_LEN = 16384
PAGE_SIZE = 128

D_MIX = D_MODEL
D_CONV = D_MIX // 2
CONV_HEAD_DIM = 64
CONV_HEADS = D_CONV // CONV_HEAD_DIM
CONV_W = 3
D_SSM = D_MIX - D_CONV
SSM_GROUP = 16
N_SSM_GROUPS = D_SSM // SSM_GROUP
SSM_STATE = 64
N_EXPERTS = 32
TOP_K = 4
D_FF = D_MODEL
SWIGLU_LIMIT = 7.0
SWIGLU_ALPHA = 1.702
MOE_BLOCK = 128
EPS = 1e-6
N_MOD = 6

kernel_name = 'hybrid_conv_s5_moe_step'


def _rmsnorm(x, g):
    xf = x.astype(jnp.float32)
    y = xf * lax.rsqrt(jnp.mean(xf * xf, axis=-1, keepdims=True) + EPS)
    return (y * g.astype(jnp.float32)).astype(x.dtype)


def _modulation(c, w_mod, b_mod):
    m = jax.nn.silu(c) @ w_mod + b_mod
    return jnp.split(m[:, None, :], N_MOD, axis=-1)


def _short_conv(b_g, c_g, v, conv_buf, conv_w):
    z = c_g * v
    z_ext = jnp.concatenate([conv_buf.astype(z.dtype), z], axis=1)
    L = z.shape[1]
    y = conv_w[0] * z_ext[:, 0:L]
    for k in range(1, CONV_W):
        y = y + conv_w[k] * z_ext[:, k:k + L]
    return b_g * y, z_ext[:, L:]


def _s5(u, h0_re, h0_im, a_re, a_im, log_dt, b_re, b_im, c_re, c_im, d_skip, glu_w, glu_b):
    f32 = jnp.float32
    bsz, L, _ = u.shape
    ug = u.astype(f32).reshape(bsz, L, N_SSM_GROUPS, SSM_GROUP)
    lam = lax.complex(a_re.astype(f32), a_im.astype(f32))
    dt = jnp.exp(log_dt.astype(f32))[:, None]
    a_bar = jnp.exp(lam * dt)
    b_tilde = lax.complex(b_re.astype(f32), b_im.astype(f32))
    b_bar = ((a_bar - 1.0) / lam)[..., None] * b_tilde
    bu = lax.complex(jnp.einsum('blgi,gpi->blgp', ug, jnp.real(b_bar)),
                     jnp.einsum('blgi,gpi->blgp', ug, jnp.imag(b_bar)))
    h0 = lax.complex(h0_re.astype(f32), h0_im.astype(f32))
    bu = bu.at[:, 0].add(a_bar * h0)
    a_seq = jnp.broadcast_to(a_bar, bu.shape)

    def combine(left, right):
        a_l, b_l = left
        a_r, b_r = right
        return a_l * a_r, a_r * b_l + b_r

    _, h = lax.associative_scan(combine, (a_seq, bu), axis=1)
    y = (jnp.einsum('blgp,gip->blgi', jnp.real(h), c_re.astype(f32))
         - jnp.einsum('blgp,gip->blgi', jnp.imag(h), c_im.astype(f32))
         + d_skip.astype(f32) * ug)
    y = jax.nn.gelu(y)
    y = y * jax.nn.sigmoid(jnp.einsum('blgi,gij->blgj', y, glu_w.astype(f32)) + glu_b.astype(f32))
    h_last = h[:, -1]
    return (y.reshape(bsz, L, D_SSM).astype(u.dtype),
            jnp.real(h_last).astype(h0_re.dtype), jnp.imag(h_last).astype(h0_im.dtype))


def _moe(h, w_router, b_router, w1, b1, w2, b2):
    f32 = jnp.float32
    bsz, L, D = h.shape
    x = h.reshape(-1, D)
    n = x.shape[0]
    nk = n * TOP_K
    logits = x.astype(f32) @ w_router.astype(f32) + b_router.astype(f32)
    top_v, top_i = lax.top_k(logits, TOP_K)
    gates = jax.nn.softmax(top_v, axis=-1)
    flat_e = top_i.reshape(-1).astype(jnp.int32)
    flat_tok = jnp.arange(nk, dtype=jnp.int32) // TOP_K
    order = jnp.argsort(flat_e)
    sorted_e = flat_e[order]
    counts = jnp.bincount(flat_e, length=N_EXPERTS).astype(jnp.int32)
    padded = (counts + MOE_BLOCK - 1) // MOE_BLOCK * MOE_BLOCK
    ends = jnp.cumsum(padded)
    starts_p = ends - padded
    starts = jnp.cumsum(counts) - counts
    rank = jnp.arange(nk, dtype=jnp.int32) - starts[sorted_e]
    dest_sorted = (starts_p[sorted_e] + rank).astype(jnp.int32)
    n_blocks = -(-nk // MOE_BLOCK) + N_EXPERTS
    n_slots = n_blocks * MOE_BLOCK
    slot_tok = jnp.full((n_slots,), n, jnp.int32).at[dest_sorted].set(flat_tok[order])
    block_e = jnp.minimum(
        jnp.searchsorted(ends, jnp.arange(n_blocks, dtype=jnp.int32) * MOE_BLOCK, side='right'),
        N_EXPERTS - 1)
    x_pad = jnp.concatenate([x, jnp.zeros((1, D), x.dtype)], axis=0)
    xs = x_pad[slot_tok].reshape(n_blocks, MOE_BLOCK, D)

    def expert_block(args):
        xb, e = args
        hcat = xb @ w1[e] + b1[e]
        g, lin = jnp.split(hcat, 2, axis=-1)
        g = jnp.minimum(g, SWIGLU_LIMIT)
        lin = jnp.clip(lin, -SWIGLU_LIMIT, SWIGLU_LIMIT)
        act = g * jax.nn.sigmoid(SWIGLU_ALPHA * g) * (lin + 1.0)
        return act @ w2[e] + b2[e]

    ys = lax.map(expert_block, (xs, block_e)).reshape(n_slots, D)
    dest = jnp.zeros((nk,), jnp.int32).at[order].set(dest_sorted)
    y = jnp.einsum('nk,nkd->nd', gates.astype(ys.dtype), ys[dest].reshape(n, TOP_K, D))
    return y.reshape(bsz, L, D).astype(h.dtype)


def _layer(x, c, conv_buf, h0_re, h0_im, g_mix, g_ffn, w_mod, b_mod, w_in, conv_w,
           a_re, a_im, log_dt, b_re, b_im, c_re, c_im, d_skip, glu_w, glu_b, w_out,
           w_router, b_router, w1, b1, w2, b2):
    sh1, sc1, gt1, sh2, sc2, gt2 = _modulation(c, w_mod, b_mod)
    h = _rmsnorm(x, g_mix) * (1.0 + sc1) + sh1
    proj = h @ w_in
    b_g, c_g, v, u = jnp.split(proj, [D_CONV, 2 * D_CONV, 3 * D_CONV], axis=-1)
    y_conv, new_conv = _short_conv(b_g, c_g, v, conv_buf, conv_w)
    y_ssm, new_re, new_im = _s5(u, h0_re, h0_im, a_re, a_im, log_dt, b_re, b_im,
                                c_re, c_im, d_skip, glu_w, glu_b)
    x = x + gt1 * (jnp.concatenate([y_conv, y_ssm], axis=-1) @ w_out)
    h = _rmsnorm(x, g_ffn) * (1.0 + sc2) + sh2
    x = x + gt2 * _moe(h, w_router, b_router, w1, b1, w2, b2)
    return x, new_conv.astype(conv_buf.dtype), new_re, new_im


def setup_inputs(seed: int = 0) -> dict:
    key = jax.random.key(seed)
    ks = iter(jax.random.split(key, 40))
    nrm = lambda shape, s: jax.random.normal(next(ks), shape, jnp.float32) * s
    G, P, I = N_SSM_GROUPS, SSM_STATE, SSM_GROUP
    d_in = 3 * D_CONV + D_SSM
    inp = {}
    inp['x_prompt'] = nrm((BATCH, SEQ, D_MODEL), 1.0)
    inp['x_sample'] = nrm((DEC_BATCH, DEC_SEQ, D_MODEL), 1.0)
    inp['c_prompt'] = nrm((BATCH, D_MODEL), 1.0)
    inp['c_sample'] = nrm((DEC_BATCH, D_MODEL), 1.0)
    inp['state_conv'] = nrm((DEPTH, DEC_BATCH, CONV_W - 1, D_CONV), 1.0)
    inp['state_ssm_re'] = nrm((DEPTH, DEC_BATCH, G, P), 0.5)
    inp['state_ssm_im'] = nrm((DEPTH, DEC_BATCH, G, P), 0.5)
    inp['g_mix'] = 1.0 + nrm((DEPTH, D_MODEL), 0.02)
    inp['g_ffn'] = 1.0 + nrm((DEPTH, D_MODEL), 0.02)
    inp['w_mod'] = nrm((DEPTH, D_MODEL, N_MOD * D_MODEL), 0.5 * D_MODEL ** -0.5)
    inp['b_mod'] = nrm((DEPTH, N_MOD * D_MODEL), 0.02)
    inp['w_in'] = nrm((DEPTH, D_MODEL, d_in), D_MODEL ** -0.5)
    inp['conv_w'] = nrm((DEPTH, CONV_W, D_CONV), CONV_W ** -0.5)
    inp['ssm_a_re'] = -0.5 + nrm((DEPTH, G, P), 0.01)
    inp['ssm_a_im'] = jnp.pi * jnp.arange(P, dtype=jnp.float32) + nrm((DEPTH, G, P), 0.01)
    inp['ssm_log_dt'] = jax.random.uniform(next(ks), (DEPTH, G), jnp.float32,
                                           minval=math.log(1e-3), maxval=math.log(1e-1))
    inp['ssm_b_re'] = nrm((DEPTH, G, P, I), (2 * I) ** -0.5)
    inp['ssm_b_im'] = nrm((DEPTH, G, P, I), (2 * I) ** -0.5)
    inp['ssm_c_re'] = nrm((DEPTH, G, I, P), (2 * P) ** -0.5)
    inp['ssm_c_im'] = nrm((DEPTH, G, I, P), (2 * P) ** -0.5)
    inp['ssm_d'] = nrm((DEPTH, G, I), 0.5)
    inp['glu_w'] = nrm((DEPTH, G, I, I), I ** -0.5)
    inp['glu_b'] = nrm((DEPTH, G, I), 0.02)
    inp['w_out'] = nrm((DEPTH, D_MIX, D_MODEL), D_MIX ** -0.5)
    inp['w_router'] = nrm((DEPTH, D_MODEL, N_EXPERTS), D_MODEL ** -0.5)
    inp['b_router'] = nrm((DEPTH, N_EXPERTS), 0.01)
    inp['w1'] = nrm((DEPTH, N_EXPERTS, D_MODEL, 2 * D_FF), D_MODEL ** -0.5)
    inp['b1'] = nrm((DEPTH, N_EXPERTS, 2 * D_FF), 0.01)
    inp['w2'] = nrm((DEPTH, N_EXPERTS, D_FF, D_MODEL), D_FF ** -0.5)
    inp['b2'] = nrm((DEPTH, N_EXPERTS, D_MODEL), 0.01)
    inp['g_final'] = 1.0 + nrm((D_MODEL,), 0.02)
    return inp


def reference(x_prompt, x_sample, c_prompt, c_sample, state_conv, state_ssm_re, state_ssm_im,
              g_mix, g_ffn, w_mod, b_mod, w_in, conv_w, ssm_a_re, ssm_a_im, ssm_log_dt,
              ssm_b_re, ssm_b_im, ssm_c_re, ssm_c_im, ssm_d, glu_w, glu_b, w_out,
              w_router, b_router, w1, b1, w2, b2, g_final):
    bp = x_prompt.shape[0]
    xp, xs = x_prompt, x_sample
    conv_p, re_p, im_p, conv_s, re_s, im_s = [], [], [], [], [], []
    for l in range(DEPTH):
        lp = (g_mix[l], g_ffn[l], w_mod[l], b_mod[l], w_in[l], conv_w[l],
              ssm_a_re[l], ssm_a_im[l], ssm_log_dt[l], ssm_b_re[l], ssm_b_im[l],
              ssm_c_re[l], ssm_c_im[l], ssm_d[l], glu_w[l], glu_b[l], w_out[l],
              w_router[l], b_router[l], w1[l], b1[l], w2[l], b2[l])
        zero_conv = jnp.zeros((bp, CONV_W - 1, D_CONV), state_conv.dtype)
        zero_ssm = jnp.zeros((bp, N_SSM_GROUPS, SSM_STATE), state_ssm_re.dtype)
        xp, cp, rp, ip = _layer(xp, c_prompt, zero_conv, zero_ssm, zero_ssm, *lp)
        xs, cs, rs, is_ = _layer(xs, c_sample, state_conv[l], state_ssm_re[l], state_ssm_im[l], *lp)
        conv_p.append(cp); re_p.append(rp); im_p.append(ip)
        conv_s.append(cs); re_s.append(rs); im_s.append(is_)
    y_prompt = _rmsnorm(xp, g_final)
    y_sample = _rmsnorm(xs, g_final)
    return (y_prompt, y_sample, jnp.stack(conv_p), jnp.stack(re_p), jnp.stack(im_p),
            jnp.stack(conv_s), jnp.stack(re_s), jnp.stack(im_s))
```

```python
import functools
import math

import jax
import jax.numpy as jnp
from jax import lax
from jax.experimental import pallas as pl
from jax.experimental.pallas import tpu as pltpu

F32 = jnp.float32
BF16 = jnp.bfloat16
I32 = jnp.int32

EPS = 1e-6
N_MOD = 6
TOP_K = 4
SWIGLU_LIMIT = 7.0
SWIGLU_ALPHA = 1.702
GELU_C = math.sqrt(2.0 / math.pi)

S5_T = 16
MOE_TB = 1280
MOE_CH = 256
MOE_TF = 256
VMEM_LIMIT = 56 * 1024 * 1024
HI = lax.Precision.HIGHEST


def _cparams(n_axes):
    return pltpu.CompilerParams(dimension_semantics=("arbitrary",) * n_axes,
                                vmem_limit_bytes=VMEM_LIMIT)


def _sigmoid(x):
    return 1.0 / (1.0 + jnp.exp(-x))


def _resident(shape, index_map):
    return pl.BlockSpec(shape, index_map, pipeline_mode=pl.Buffered(1))


def _mod_body(c_ref, w_ref, b_ref, o_ref):
    c = c_ref[...]
    s = c * _sigmoid(c)
    o_ref[...] = jnp.dot(s.astype(BF16), w_ref[...].astype(BF16),
                         preferred_element_type=F32) + b_ref[...]


def _modulation(c_all, w_mod, b_mod):
    rows, d = c_all.shape
    n = w_mod.shape[1]
    tn = 1024
    return pl.pallas_call(
        _mod_body,
        grid=(n // tn,),
        in_specs=[pl.BlockSpec((rows, d), lambda j: (0, 0)),
                  pl.BlockSpec((d, tn), lambda j: (0, j)),
                  pl.BlockSpec((1, tn), lambda j: (0, j))],
        out_specs=pl.BlockSpec((rows, tn), lambda j: (0, j)),
        out_shape=jax.ShapeDtypeStruct((rows, n), F32),
        compiler_params=_cparams(1),
        name="modulation",
    )(c_all, w_mod, b_mod.reshape(1, n))


def _mod_spec(per_row, tm, rows_per_batch, d, col):
    if per_row:
        return pl.BlockSpec((tm, d), lambda i: (i, col))
    return pl.BlockSpec((None, 1, d), lambda i: ((i * tm) // rows_per_batch, 0, col))


def _inproj_body(x_ref, g_ref, sc_ref, sh_ref, w_ref, o_ref):
    x = x_ref[...]
    ms = jnp.mean(x * x, axis=-1, keepdims=True)
    h = x * lax.rsqrt(ms + EPS) * g_ref[...]
    h = h * (1.0 + sc_ref[...]) + sh_ref[...]
    o_ref[...] = jnp.dot(h.astype(BF16), w_ref[...], preferred_element_type=F32)


def _inproj(x, mod, per_row, rows_per_batch, g, w_bf, tm):
    n, d = x.shape
    dn = w_bf.shape[1]
    return pl.pallas_call(
        _inproj_body,
        grid=(n // tm,),
        in_specs=[pl.BlockSpec((tm, d), lambda i: (i, 0)),
                  pl.BlockSpec((1, d), lambda i: (0, 0)),
                  _mod_spec(per_row, tm, rows_per_batch, d, 1),
                  _mod_spec(per_row, tm, rows_per_batch, d, 0),
                  _resident((d, dn), lambda i: (0, 0))],
        out_specs=pl.BlockSpec((tm, dn), lambda i: (i, 0)),
        out_shape=jax.ShapeDtypeStruct((n, dn), F32),
        compiler_params=_cparams(1),
        name="inproj",
    )(x, g.reshape(1, d), mod, mod, w_bf)


def _conv_prompt_body(b_ref, c_ref, v_ref, w_ref, y_ref, tail_ref, carry_ref):
    @pl.when(pl.program_id(1) == 0)
    def _():
        carry_ref[...] = jnp.zeros_like(carry_ref)

    z = c_ref[...] * v_ref[...]
    tt = z.shape[0]
    zc = jnp.concatenate([carry_ref[...], z], axis=0)
    z1 = pltpu.roll(zc, 1, 0)[8:]
    z2 = pltpu.roll(zc, 2, 0)[8:]
    w = w_ref[...]
    y = w[0:1] * z2 + w[1:2] * z1 + w[2:3] * z
    y_ref[...] = (b_ref[...] * y).astype(y_ref.dtype)
    carry_ref[...] = z[tt - 8:]
    tail_ref[...] = z[tt - 8:]


def _conv_prompt(proj, conv_w, bsz, seq, dc, tt):
    nt = seq // tt
    spec = lambda col: pl.BlockSpec((tt, dc), lambda b, t: (b * nt + t, col))
    return pl.pallas_call(
        _conv_prompt_body,
        grid=(bsz, nt),
        in_specs=[spec(0), spec(1), spec(2),
                  pl.BlockSpec(conv_w.shape, lambda b, t: (0, 0))],
        out_specs=[pl.BlockSpec((tt, dc), lambda b, t: (b * nt + t, 0)),
                   pl.BlockSpec((None, 8, dc), lambda b, t: (b, 0, 0))],
        out_shape=[jax.ShapeDtypeStruct((bsz * seq, dc), BF16),
                   jax.ShapeDtypeStruct((bsz, 8, dc), F32)],
        scratch_shapes=[pltpu.VMEM((8, dc), F32)],
        compiler_params=_cparams(2),
        name="conv_prompt",
    )(proj, proj, proj, conv_w)


def _conv_sample_body(b_ref, c_ref, v_ref, e_ref, w_ref, y_ref, z_ref):
    z = c_ref[...] * v_ref[...]
    e = e_ref[...]
    rows = z.shape[0]
    tpos = lax.broadcasted_iota(I32, z.shape, 0) & 7
    z1 = jnp.where(tpos == 0, pltpu.roll(e, rows - 1, 0), pltpu.roll(z, 1, 0))
    z2 = jnp.where(tpos < 2, e, pltpu.roll(z, 2, 0))
    w = w_ref[...]
    y = w[0:1] * z2 + w[1:2] * z1 + w[2:3] * z
    y_ref[...] = (b_ref[...] * y).astype(y_ref.dtype)
    z_ref[...] = z


def _conv_sample(proj, e, conv_w, dc, tm):
    n = proj.shape[0]
    spec = lambda col: pl.BlockSpec((tm, dc), lambda i: (i, col))
    return pl.pallas_call(
        _conv_sample_body,
        grid=(n // tm,),
        in_specs=[spec(0), spec(1), spec(2), spec(0),
                  pl.BlockSpec(conv_w.shape, lambda i: (0, 0))],
        out_specs=[spec(0), spec(0)],
        out_shape=[jax.ShapeDtypeStruct((n, dc), BF16),
                   jax.ShapeDtypeStruct((n, dc), F32)],
        compiler_params=_cparams(1),
        name="conv_sample",
    )(proj, proj, proj, e, conv_w)


def _s5_operators(a_re, a_im, log_dt, b_re, b_im, c_re, c_im, d_skip, glu_w, glu_b):
    g, p, ch = b_re.shape
    t = S5_T
    dt = jnp.exp(log_dt)[:, None]
    mag = jnp.exp(a_re * dt)
    ar = mag * jnp.cos(a_im * dt)
    ai = mag * jnp.sin(a_im * dt)
    den = a_re * a_re + a_im * a_im
    qr = ((ar - 1.0) * a_re + ai * a_im) / den
    qi = (ai * a_re - (ar - 1.0) * a_im) / den
    bbr = qr[..., None] * b_re - qi[..., None] * b_im
    bbi = qr[..., None] * b_im + qi[..., None] * b_re
    pr, pi = [jnp.ones_like(ar)], [jnp.zeros_like(ar)]
    for _ in range(t):
        pr, pi = pr + [pr[-1] * ar - pi[-1] * ai], pi + [pr[-1] * ai + pi[-1] * ar]
    pw_r, pw_i = jnp.stack(pr), jnp.stack(pi)
    car = c_re[None] * pw_r[:, :, None, :] - c_im[None] * pw_i[:, :, None, :]
    cai = c_re[None] * pw_i[:, :, None, :] + c_im[None] * pw_r[:, :, None, :]
    kern = (jnp.einsum('jgop,gpi->jgoi', car[:t], bbr, precision=HI)
            - jnp.einsum('jgop,gpi->jgoi', cai[:t], bbi, precision=HI))
    lag = jnp.arange(t)[None, :] - jnp.arange(t)[:, None]
    toep = jnp.where((lag >= 0)[:, :, None, None, None], kern[jnp.clip(lag, 0, t - 1)], 0.0)
    m_op = toep.transpose(2, 0, 4, 1, 3).reshape(g, t * ch, t * ch)
    rev_r, rev_i = pw_r[t - 1::-1][:t], pw_i[t - 1::-1][:t]
    wsr = rev_r[..., None] * bbr[None] - rev_i[..., None] * bbi[None]
    wsi = rev_r[..., None] * bbi[None] + rev_i[..., None] * bbr[None]
    ws_op = jnp.concatenate([wsr.transpose(1, 0, 3, 2), wsi.transpose(1, 0, 3, 2)],
                            axis=-1).reshape(g, t * ch, 2 * p)
    wcr = car[1:].transpose(1, 3, 0, 2).reshape(g, p, t * ch)
    wci = cai[1:].transpose(1, 3, 0, 2).reshape(g, p, t * ch)
    wc_op = jnp.concatenate([wcr, -wci], axis=1)
    glu_op = jnp.einsum('st,gij->gsitj', jnp.eye(t, dtype=F32), glu_w).reshape(g, t * ch, t * ch)
    d_flat = jnp.tile(d_skip, (1, t))[:, None, :]
    gb_flat = jnp.tile(glu_b, (1, t))[:, None, :]

    def rot_tables(xr, xi):
        return jnp.concatenate([xr, xr], axis=-1), jnp.concatenate([-xi, xi], axis=-1)

    sr, si = pw_r[t], pw_i[t]
    p1s, p2s = [], []
    for _ in range(8):
        t1, t2 = rot_tables(sr, si)
        p1s.append(t1)
        p2s.append(t2)
        sr, si = sr * sr - si * si, 2.0 * sr * si
    h = t // 2
    p1_half, p2_half = rot_tables(pw_r[h], pw_i[h])
    return dict(
        m=m_op.astype(BF16), ws=ws_op.astype(BF16), wc=wc_op.astype(BF16), glu=glu_op.astype(BF16),
        d=d_flat, gb=gb_flat, p1=jnp.stack(p1s, axis=1), p2=jnp.stack(p2s, axis=1),
        p1_half=p1_half[:, None, :], p2_half=p2_half[:, None, :])


def _gelu_tanh(y):
    return 0.5 * y * (1.0 + jnp.tanh(GELU_C * (y + 0.044715 * (y * y * y))))


def _s5_tail(y, gl, gb):
    y = _gelu_tanh(y)
    gate = jnp.dot(y.astype(BF16), gl, preferred_element_type=F32) + gb
    return y * _sigmoid(gate)


def _s5_prompt_body(u_ref, m_ref, ws_ref, wc_ref, gl_ref, d_ref, gb_ref, p1_ref, p2_ref,
                    y_ref, hf_ref, *, groups, bsz, nc):
    half = ws_ref.shape[-1] // 2
    for g in range(groups):
        u = u_ref[g]
        ub = u.astype(BF16)
        yi = jnp.dot(ub, m_ref[g], preferred_element_type=F32)
        e = jnp.dot(ub, ws_ref[g], preferred_element_type=F32)
        cpos = lax.broadcasted_iota(I32, e.shape, 0) & (nc - 1)
        p1 = p1_ref[g]
        p2 = p2_ref[g]
        z = jnp.where(cpos >= 1, pltpu.roll(e, 1, 0), 0.0)
        d, k = 1, 0
        while d < nc:
            zs = jnp.where(cpos >= d, pltpu.roll(z, d, 0), 0.0)
            z = z + zs * p1[k:k + 1] + pltpu.roll(zs, half, 1) * p2[k:k + 1]
            d, k = d * 2, k + 1
        yc = jnp.dot(z.astype(BF16), wc_ref[g], preferred_element_type=F32)
        y = _s5_tail(yi + yc + d_ref[g] * u, gl_ref[g], gb_ref[g])
        y_ref[g] = y.astype(y_ref.dtype)
        hfin = e + z * p1[0:1] + pltpu.roll(z, half, 1) * p2[0:1]
        for b in range(bsz):
            r = b * nc + nc - 1
            hf_ref[g, b:b + 1, :] = hfin[r:r + 1, :]


def _s5_prompt(u_flat, ops, bsz, nc, groups):
    g, r, w = u_flat.shape
    p2x = ops['ws'].shape[-1]
    blk = lambda s1, s2: pl.BlockSpec((groups, s1, s2), lambda i: (i, 0, 0))
    return pl.pallas_call(
        functools.partial(_s5_prompt_body, groups=groups, bsz=bsz, nc=nc),
        grid=(g // groups,),
        in_specs=[blk(r, w), blk(w, w), blk(w, p2x), blk(p2x, w), blk(w, w),
                  blk(1, w), blk(1, w), blk(8, p2x), blk(8, p2x)],
        out_specs=[blk(r, w), blk(bsz, p2x)],
        out_shape=[jax.ShapeDtypeStruct((g, r, w), BF16),
                   jax.ShapeDtypeStruct((g, bsz, p2x), F32)],
        compiler_params=_cparams(1),
        name="s5_prompt",
    )(u_flat, ops['m'], ops['ws'], ops['wc'], ops['glu'], ops['d'], ops['gb'], ops['p1'], ops['p2'])


def _s5_sample_body(u_ref, h0_ref, m_ref, ws_ref, wc_ref, gl_ref, d_ref, gb_ref, p1_ref, p2_ref,
                    y_ref, hf_ref, *, groups):
    half = ws_ref.shape[-1] // 2
    for g in range(groups):
        u = u_ref[g]
        h0 = h0_ref[g]
        ub = u.astype(BF16)
        yi = jnp.dot(ub, m_ref[g], preferred_element_type=F32)
        e = jnp.dot(ub, ws_ref[g], preferred_element_type=F32)
        yc = jnp.dot(h0.astype(BF16), wc_ref[g], preferred_element_type=F32)
        y = _s5_tail(yi + yc + d_ref[g] * u, gl_ref[g], gb_ref[g])
        y_ref[g] = y.astype(y_ref.dtype)
        hf_ref[g] = e + h0 * p1_ref[g] + pltpu.roll(h0, half, 1) * p2_ref[g]


def _s5_sample(u_flat, h0, ops, groups):
    g, r, w = u_flat.shape
    p2x = h0.shape[-1]
    blk = lambda s1, s2: pl.BlockSpec((groups, s1, s2), lambda i: (i, 0, 0))
    return pl.pallas_call(
        functools.partial(_s5_sample_body, groups=groups),
        grid=(g // groups,),
        in_specs=[blk(r, w), blk(r, p2x), blk(w, w), blk(w, p2x), blk(p2x, w), blk(w, w),
                  blk(1, w), blk(1, w), blk(1, p2x), blk(1, p2x)],
        out_specs=[blk(r, w), blk(r, p2x)],
        out_shape=[jax.ShapeDtypeStruct((g, r, w), BF16),
                   jax.ShapeDtypeStruct((g, r, p2x), F32)],
        compiler_params=_cparams(1),
        name="s5_sample",
    )(u_flat, h0, ops['m'], ops['ws'], ops['wc'], ops['glu'], ops['d'], ops['gb'],
      ops['p1'], ops['p2'])


def _outproj_body(cnt0_ref, x_ref, yc_ref, ys_ref, wo_ref, gt_ref, g_ref, sc_ref, sh_ref,
                  wr_ref, br_ref, x1_ref, h2_ref, ti_ref, gate_ref, rank_ref, cnt_ref, run_ref):
    @pl.when(pl.program_id(0) == 0)
    def _():
        run_ref[...] = cnt0_ref[...]

    dc = yc_ref.shape[1]
    mix = (jnp.dot(yc_ref[...], wo_ref[0:dc, :], preferred_element_type=F32)
           + jnp.dot(ys_ref[...], wo_ref[dc:, :], preferred_element_type=F32))
    x1 = x_ref[...] + gt_ref[...] * mix
    x1_ref[...] = x1
    ms = jnp.mean(x1 * x1, axis=-1, keepdims=True)
    h = x1 * lax.rsqrt(ms + EPS) * g_ref[...]
    hb = (h * (1.0 + sc_ref[...]) + sh_ref[...]).astype(BF16)
    h2_ref[...] = hb
    logits = jnp.dot(hb, wr_ref[...], preferred_element_type=F32) + br_ref[...]
    tm, ne = logits.shape
    lane = lax.broadcasted_iota(I32, logits.shape, 1).astype(F32)
    work = logits
    vals, ids, sels = [], [], []
    for _ in range(TOP_K):
        m = jnp.max(work, axis=1, keepdims=True)
        idx = jnp.min(jnp.where(work == m, lane, float(ne)), axis=1, keepdims=True)
        sel = lane == idx
        vals.append(m)
        ids.append(idx)
        sels.append(sel)
        work = jnp.where(sel, -jnp.inf, work)
    exps = [jnp.exp(v - vals[0]) for v in vals]
    tot = exps[0]
    for ex in exps[1:]:
        tot = tot + ex
    gates = [ex / tot for ex in exps]
    onehot = sels[0]
    for s in sels[1:]:
        onehot = onehot | s
    onehot = onehot.astype(F32)
    row = lax.broadcasted_iota(I32, (tm, tm), 0)
    col = lax.broadcasted_iota(I32, (tm, tm), 1)
    below = (col < row).astype(BF16)
    before = jnp.dot(below, onehot.astype(BF16), preferred_element_type=F32) + run_ref[...]
    ranks = [jnp.sum(jnp.where(s, before, 0.0), axis=1, keepdims=True) for s in sels]
    run_ref[...] = run_ref[...] + jnp.sum(onehot, axis=0, keepdims=True)
    cnt_ref[...] = run_ref[...]

    wide = lax.broadcasted_iota(I32, ti_ref.shape, 1)

    def spread(cols):
        out = cols[TOP_K - 1]
        for k in range(TOP_K - 2, -1, -1):
            out = jnp.where(wide == k, cols[k], out)
        return out

    ti_ref[...] = spread(ids).astype(I32)
    gate_ref[...] = spread(gates)
    rank_ref[...] = spread(ranks).astype(I32)


def _outproj(cnt0, x, yc, ys, wo_bf, mod, per_row, rows_per_batch, g, wr_bf, br, tm):
    n, d = x.shape
    dc = yc.shape[1]
    ne = wr_bf.shape[1]
    row = lambda w: pl.BlockSpec((tm, w), lambda i: (i, 0))
    const = lambda s: pl.BlockSpec(s, lambda i: (0, 0))
    return pl.pallas_call(
        _outproj_body,
        grid=(n // tm,),
        in_specs=[const((1, ne)), row(d), row(dc), row(dc),
                  _resident((d, d), lambda i: (0, 0)),
                  _mod_spec(per_row, tm, rows_per_batch, d, 2), const((1, d)),
                  _mod_spec(per_row, tm, rows_per_batch, d, 4),
                  _mod_spec(per_row, tm, rows_per_batch, d, 3),
                  const((d, ne)), const((1, ne))],
        out_specs=[row(d), row(d), row(128), row(128), row(128), const((1, ne))],
        out_shape=[jax.ShapeDtypeStruct((n, d), F32), jax.ShapeDtypeStruct((n, d), BF16),
                   jax.ShapeDtypeStruct((n, 128), I32), jax.ShapeDtypeStruct((n, 128), F32),
                   jax.ShapeDtypeStruct((n, 128), I32), jax.ShapeDtypeStruct((1, ne), F32)],
        scratch_shapes=[pltpu.VMEM((1, ne), F32)],
        compiler_params=_cparams(1),
        name="outproj_router",
    )(cnt0, x, yc, ys, wo_bf, mod, g.reshape(1, d), mod, mod, wr_bf, br.reshape(1, ne))


def _moe_body(be_ref, rb_ref, nch_ref, x_ref, w1g_ref, w1l_ref, b1g_ref, b1l_ref, w2_ref, b2_ref,
              o_ref, wg_s, wl_s, w2_s):
    del be_ref, rb_ref
    b = pl.program_id(0)
    j = pl.program_id(1)
    n_chunks = nch_ref[b]

    @pl.when(n_chunks > 0)
    def _():
        wg_s[...] = w1g_ref[...].astype(BF16)
        wl_s[...] = w1l_ref[...].astype(BF16)
        w2_s[...] = w2_ref[...].astype(BF16)

        def chunk(i, carry):
            r0 = pl.multiple_of(i * MOE_CH, MOE_CH)
            xc = x_ref[pl.ds(r0, MOE_CH), :]
            gl = jnp.dot(xc, wg_s[...], preferred_element_type=F32) + b1g_ref[...]
            ln = jnp.dot(xc, wl_s[...], preferred_element_type=F32) + b1l_ref[...]
            gl = jnp.minimum(gl, SWIGLU_LIMIT)
            ln = jnp.clip(ln, -SWIGLU_LIMIT, SWIGLU_LIMIT)
            act = gl * _sigmoid(SWIGLU_ALPHA * gl) * (ln + 1.0)
            part = jnp.dot(act.astype(BF16), w2_s[...], preferred_element_type=F32)

            @pl.when(j == 0)
            def _():
                o_ref[pl.ds(r0, MOE_CH), :] = part + b2_ref[...]

            @pl.when(j > 0)
            def _():
                o_ref[pl.ds(r0, MOE_CH), :] += part

            return carry

        lax.fori_loop(0, n_chunks, chunk, 0)


def _moe_experts(xs, w1, b1, w2, b2, blk_expert, blk_rows, blk_chunks):
    n_slots, d = xs.shape
    ne, _, f2 = w1.shape
    f = f2 // 2
    nf = f // MOE_TF
    nb = n_slots // MOE_TB
    last = nf - 1

    def jj(b, j, nch):
        return jnp.where(nch[b] > 0, j, last)

    grid_spec = pltpu.PrefetchScalarGridSpec(
        num_scalar_prefetch=3,
        grid=(nb, nf),
        in_specs=[
            pl.BlockSpec((MOE_TB, d), lambda b, j, be, rb, nch: (rb[b], 0)),
            pl.BlockSpec((None, d, MOE_TF), lambda b, j, be, rb, nch: (be[b], 0, jj(b, j, nch))),
            pl.BlockSpec((None, d, MOE_TF), lambda b, j, be, rb, nch: (be[b], 0, nf + jj(b, j, nch))),
            pl.BlockSpec((None, 1, MOE_TF), lambda b, j, be, rb, nch: (be[b], 0, jj(b, j, nch))),
            pl.BlockSpec((None, 1, MOE_TF), lambda b, j, be, rb, nch: (be[b], 0, nf + jj(b, j, nch))),
            pl.BlockSpec((None, MOE_TF, d), lambda b, j, be, rb, nch: (be[b], jj(b, j, nch), 0)),
            pl.BlockSpec((None, 1, d), lambda b, j, be, rb, nch: (be[b], 0, 0)),
        ],
        out_specs=pl.BlockSpec((MOE_TB, d), lambda b, j, be, rb, nch: (rb[b], 0)),
        scratch_shapes=[pltpu.VMEM((d, MOE_TF), BF16), pltpu.VMEM((d, MOE_TF), BF16),
                        pltpu.VMEM((MOE_TF, d), BF16)],
    )
    return pl.pallas_call(
        _moe_body,
        grid_spec=grid_spec,
        out_shape=jax.ShapeDtypeStruct((n_slots, d), F32),
        compiler_params=_cparams(2),
        name="moe_experts",
    )(blk_expert, blk_rows, blk_chunks, xs, w1, w1, b1.reshape(ne, 1, f2), b1.reshape(ne, 1, f2),
      w2, b2.reshape(ne, 1, d))


def _moe_plan(counts, top_i, rank, n_tokens, n_blocks):
    ne = counts.shape[0]
    nblk = (counts + MOE_TB - 1) // MOE_TB
    blk_end = jnp.cumsum(nblk)
    blk_start = blk_end - nblk
    n_used = blk_end[-1]
    dest = blk_start[top_i] * MOE_TB + rank
    bidx = jnp.arange(n_blocks, dtype=I32)
    be = jnp.minimum(jnp.searchsorted(blk_end, bidx, side='right'), ne - 1).astype(I32)
    valid = jnp.clip(counts[be] - (bidx - blk_start[be]) * MOE_TB, 0, MOE_TB)
    active = bidx < n_used
    chunks = jnp.where(active, (valid + MOE_CH - 1) // MOE_CH, 0).astype(I32)
    last = jnp.maximum(n_used - 1, 0)
    be = jnp.where(active, be, be[last]).astype(I32)
    rows = jnp.where(active, bidx, last).astype(I32)
    tok = jnp.arange(n_tokens * TOP_K, dtype=I32) // TOP_K
    slot_tok = jnp.full((n_blocks * MOE_TB,), n_tokens, I32).at[dest.reshape(-1)].set(tok)
    return dest, slot_tok, be, rows, chunks


def _combine_body(x1_ref, yg_ref, gate_ref, gt_ref, gf_ref, o_ref):
    gates = gate_ref[...]
    y = gates[:, 0:1] * yg_ref[0]
    for k in range(1, TOP_K):
        y = y + gates[:, k:k + 1] * yg_ref[k]
    x2 = x1_ref[...] + gt_ref[...] * y
    ms = jnp.mean(x2 * x2, axis=-1, keepdims=True)
    o_ref[...] = x2 * lax.rsqrt(ms + EPS) * gf_ref[...]


def _combine(x1, yg, gates, mod, per_row, rows_per_batch, g_final, tm):
    n, d = x1.shape
    return pl.pallas_call(
        _combine_body,
        grid=(n // tm,),
        in_specs=[pl.BlockSpec((tm, d), lambda i: (i, 0)),
                  pl.BlockSpec((TOP_K, tm, d), lambda i: (0, i, 0)),
                  pl.BlockSpec((tm, 128), lambda i: (i, 0)),
                  _mod_spec(per_row, tm, rows_per_batch, d, 5),
                  pl.BlockSpec((1, d), lambda i: (0, 0))],
        out_specs=pl.BlockSpec((tm, d), lambda i: (i, 0)),
        out_shape=jax.ShapeDtypeStruct((n, d), F32),
        compiler_params=_cparams(1),
        name="combine_norm",
    )(x1, yg, gates, mod, g_final.reshape(1, d))


def kernel(x_prompt, x_sample, c_prompt, c_sample, state_conv, state_ssm_re, state_ssm_im, g_mix, g_ffn, w_mod, b_mod, w_in, conv_w, ssm_a_re, ssm_a_im, ssm_log_dt, ssm_b_re, ssm_b_im, ssm_c_re, ssm_c_im, ssm_d, glu_w, glu_b, w_out, w_router, b_router, w1, b1, w2, b2, g_final):
    bp, lp, d = x_prompt.shape
    bs, ls, _ = x_sample.shape
    depth = g_mix.shape[0]
    dc = conv_w.shape[-1]
    n_grp, p_st, ch = ssm_b_re.shape[1:]
    ne = w_router.shape[-1]
    np_, ns_ = bp * lp, bs * ls
    n_tok = np_ + ns_
    assert ls == S5_T // 2 and lp % S5_T == 0 and conv_w.shape[1] == 3
    nc = lp // S5_T
    n_blocks = -(-(n_tok * TOP_K) // MOE_TB) + ne

    xp = x_prompt.reshape(np_, d)
    xs = x_sample.reshape(ns_, d)
    c_all = jnp.concatenate([c_prompt, c_sample], axis=0)
    pad = (-c_all.shape[0]) % 8
    c_all = jnp.pad(c_all, ((0, pad), (0, 0)))

    assert depth == 1
    outs = [[] for _ in range(6)]
    for l in range(depth):
        m = _modulation(c_all, w_mod[l], b_mod[l])
        mod_p = m[:bp].reshape(bp, 1, N_MOD * d)
        mod_s = jnp.repeat(m[bp:bp + bs], ls, axis=0)

        w_in_bf = w_in[l].astype(BF16)
        proj_p = _inproj(xp, mod_p, False, lp, g_mix[l], w_in_bf, 512)
        proj_s = _inproj(xs, mod_s, True, ls, g_mix[l], w_in_bf, 256)

        yc_p, tail_p = _conv_prompt(proj_p, conv_w[l], bp, lp, dc, 512)
        e_s = jnp.pad(state_conv[l], ((0, 0), (0, ls - 2), (0, 0))).reshape(ns_, dc)
        yc_s, z_s = _conv_sample(proj_s, e_s, conv_w[l], dc, 256)
        new_conv_p = tail_p[:, 6:8, :]
        new_conv_s = z_s.reshape(bs, ls, dc)[:, ls - 2:, :]

        ops = _s5_operators(ssm_a_re[l], ssm_a_im[l], ssm_log_dt[l], ssm_b_re[l], ssm_b_im[l],
                            ssm_c_re[l], ssm_c_im[l], ssm_d[l], glu_w[l], glu_b[l])
        w16, w8 = S5_T * ch, ls * ch
        u_p = proj_p[:, 3 * dc:].reshape(bp * nc, S5_T, n_grp, ch).transpose(2, 0, 1, 3)
        u_p = u_p.reshape(n_grp, bp * nc, w16)
        ys_p, hf_p = _s5_prompt(u_p, ops, bp, nc, 4)
        ys_p = ys_p.reshape(n_grp, bp * nc, S5_T, ch).transpose(1, 2, 0, 3).reshape(np_, n_grp * ch)
        new_re_p = hf_p[:, :, :p_st].transpose(1, 0, 2)
        new_im_p = hf_p[:, :, p_st:].transpose(1, 0, 2)

        u_s = proj_s[:, 3 * dc:].reshape(bs, ls, n_grp, ch).transpose(2, 0, 1, 3).reshape(n_grp, bs, w8)
        h0 = jnp.concatenate([state_ssm_re[l], state_ssm_im[l]], axis=-1).transpose(1, 0, 2)
        ops_s = dict(m=ops['m'][:, :w8, :w8], ws=ops['ws'][:, w8:, :], wc=ops['wc'][:, :, :w8],
                     glu=ops['glu'][:, :w8, :w8], d=ops['d'][:, :, :w8], gb=ops['gb'][:, :, :w8],
                     p1=ops['p1_half'], p2=ops['p2_half'])
        ys_s, hf_s = _s5_sample(u_s, h0, ops_s, 8)
        ys_s = ys_s.reshape(n_grp, bs, ls, ch).transpose(1, 2, 0, 3).reshape(ns_, n_grp * ch)
        new_re_s = hf_s[:, :, :p_st].transpose(1, 0, 2)
        new_im_s = hf_s[:, :, p_st:].transpose(1, 0, 2)

        wo_bf = w_out[l].astype(BF16)
        wr_bf = w_router[l].astype(BF16)
        cnt0 = jnp.zeros((1, ne), F32)
        x1_p, h2_p, ti_p, gate_p, rank_p, cnt1 = _outproj(
            cnt0, xp, yc_p, ys_p, wo_bf, mod_p, False, lp, g_ffn[l], wr_bf, b_router[l], 256)
        x1_s, h2_s, ti_s, gate_s, rank_s, cnt2 = _outproj(
            cnt1, xs, yc_s, ys_s, wo_bf, mod_s, True, ls, g_ffn[l], wr_bf, b_router[l], 256)

        counts = cnt2[0].astype(I32)
        top_i = jnp.concatenate([ti_p[:, :TOP_K], ti_s[:, :TOP_K]], axis=0)
        rank = jnp.concatenate([rank_p[:, :TOP_K], rank_s[:, :TOP_K]], axis=0)
        dest, slot_tok, blk_e, blk_rows, blk_chunks = _moe_plan(counts, top_i, rank, n_tok, n_blocks)
        h2_all = jnp.concatenate([h2_p, h2_s, jnp.zeros((8, d), BF16)], axis=0)
        x_sorted = h2_all[slot_tok]
        y_sorted = _moe_experts(x_sorted, w1[l], b1[l], w2[l], b2[l], blk_e, blk_rows, blk_chunks)
        yg = y_sorted[dest.T]

        xp = _combine(x1_p, yg[:, :np_], gate_p, mod_p, False, lp, g_final, 256)
        xs = _combine(x1_s, yg[:, np_:], gate_s, mod_s, True, ls, g_final, 256)
        for lst, val in zip(outs, (new_conv_p, new_re_p, new_im_p, new_conv_s, new_re_s, new_im_s)):
            lst.append(val)

    y_prompt = xp.reshape(bp, lp, d)
    y_sample = xs.reshape(bs, ls, d)
    return (y_prompt, y_sample) + tuple(jnp.stack(o) for o in outs)
```

```python
import functools
import math

import jax
import jax.numpy as jnp
from jax import lax
from jax.experimental import pallas as pl
from jax.experimental.pallas import tpu as pltpu

F32 = jnp.float32
BF16 = jnp.bfloat16
I32 = jnp.int32

EPS = 1e-6
N_MOD = 6
TOP_K = 4
SWIGLU_LIMIT = 7.0
SWIGLU_ALPHA = 1.702
GELU_C = math.sqrt(2.0 / math.pi)

S5_T = 16
MOE_TB = 1536
MOE_CH = 256
MOE_TF = 256
MOE_TN = 512
VMEM_LIMIT = 56 * 1024 * 1024
HI = lax.Precision.HIGHEST


def _cparams(n_axes):
    return pltpu.CompilerParams(dimension_semantics=("arbitrary",) * n_axes,
                                vmem_limit_bytes=VMEM_LIMIT)


def _sigmoid(x):
    return 1.0 / (1.0 + jnp.exp(-x))


def _resident(shape, index_map):
    return pl.BlockSpec(shape, index_map, pipeline_mode=pl.Buffered(1))


def _mod_body(c_ref, w_ref, b_ref, o_ref):
    c = c_ref[...]
    s = c * _sigmoid(c)
    o_ref[...] = jnp.dot(s.astype(BF16), w_ref[...].astype(BF16),
                         preferred_element_type=F32) + b_ref[...]


def _modulation(c_all, w_mod, b_mod):
    rows, d = c_all.shape
    n = w_mod.shape[1]
    tn = 1024
    return pl.pallas_call(
        _mod_body,
        grid=(n // tn,),
        in_specs=[pl.BlockSpec((rows, d), lambda j: (0, 0)),
                  pl.BlockSpec((d, tn), lambda j: (0, j)),
                  pl.BlockSpec((1, tn), lambda j: (0, j))],
        out_specs=pl.BlockSpec((rows, tn), lambda j: (0, j)),
        out_shape=jax.ShapeDtypeStruct((rows, n), F32),
        compiler_params=_cparams(1),
        name="modulation",
    )(c_all, w_mod, b_mod.reshape(1, n))


def _mod_spec(per_row, tm, rows_per_batch, d, col):
    if per_row:
        return pl.BlockSpec((tm, d), lambda i: (i, col))
    return pl.BlockSpec((None, 1, d), lambda i: ((i * tm) // rows_per_batch, 0, col))


def _inproj_body(x_ref, g_ref, sc_ref, sh_ref, w_ref, o_ref):
    x = x_ref[...]
    ms = jnp.mean(x * x, axis=-1, keepdims=True)
    h = x * lax.rsqrt(ms + EPS) * g_ref[...]
    h = h * (1.0 + sc_ref[...]) + sh_ref[...]
    o_ref[...] = jnp.dot(h.astype(BF16), w_ref[...], preferred_element_type=F32)


def _inproj(x, mod, per_row, rows_per_batch, g, w_bf, tm):
    n, d = x.shape
    dn = w_bf.shape[1]
    return pl.pallas_call(
        _inproj_body,
        grid=(n // tm,),
        in_specs=[pl.BlockSpec((tm, d), lambda i: (i, 0)),
                  pl.BlockSpec((1, d), lambda i: (0, 0)),
                  _mod_spec(per_row, tm, rows_per_batch, d, 1),
                  _mod_spec(per_row, tm, rows_per_batch, d, 0),
                  _resident((d, dn), lambda i: (0, 0))],
        out_specs=pl.BlockSpec((tm, dn), lambda i: (i, 0)),
        out_shape=jax.ShapeDtypeStruct((n, dn), F32),
        compiler_params=_cparams(1),
        name="inproj",
    )(x, g.reshape(1, d), mod, mod, w_bf)


def _conv_prompt_body(b_ref, c_ref, v_ref, w_ref, y_ref, tail_ref, carry_ref):
    @pl.when(pl.program_id(1) == 0)
    def _():
        carry_ref[...] = jnp.zeros_like(carry_ref)

    z = c_ref[...] * v_ref[...]
    tt = z.shape[0]
    zc = jnp.concatenate([carry_ref[...], z], axis=0)
    z1 = pltpu.roll(zc, 1, 0)[8:]
    z2 = pltpu.roll(zc, 2, 0)[8:]
    w = w_ref[...]
    y = w[0:1] * z2 + w[1:2] * z1 + w[2:3] * z
    y_ref[...] = (b_ref[...] * y).astype(y_ref.dtype)
    carry_ref[...] = z[tt - 8:]
    tail_ref[...] = z[tt - 8:]


def _conv_prompt(proj, conv_w, bsz, seq, dc, tt):
    nt = seq // tt
    spec = lambda col: pl.BlockSpec((tt, dc), lambda b, t: (b * nt + t, col))
    return pl.pallas_call(
        _conv_prompt_body,
        grid=(bsz, nt),
        in_specs=[spec(0), spec(1), spec(2),
                  pl.BlockSpec(conv_w.shape, lambda b, t: (0, 0))],
        out_specs=[pl.BlockSpec((tt, dc), lambda b, t: (b * nt + t, 0)),
                   pl.BlockSpec((None, 8, dc), lambda b, t: (b, 0, 0))],
        out_shape=[jax.ShapeDtypeStruct((bsz * seq, dc), BF16),
                   jax.ShapeDtypeStruct((bsz, 8, dc), F32)],
        scratch_shapes=[pltpu.VMEM((8, dc), F32)],
        compiler_params=_cparams(2),
        name="conv_prompt",
    )(proj, proj, proj, conv_w)


def _conv_sample_body(b_ref, c_ref, v_ref, e_ref, w_ref, y_ref, z_ref):
    z = c_ref[...] * v_ref[...]
    e = e_ref[...]
    rows = z.shape[0]
    tpos = lax.broadcasted_iota(I32, z.shape, 0) & 7
    z1 = jnp.where(tpos == 0, pltpu.roll(e, rows - 1, 0), pltpu.roll(z, 1, 0))
    z2 = jnp.where(tpos < 2, e, pltpu.roll(z, 2, 0))
    w = w_ref[...]
    y = w[0:1] * z2 + w[1:2] * z1 + w[2:3] * z
    y_ref[...] = (b_ref[...] * y).astype(y_ref.dtype)
    z_ref[...] = z


def _conv_sample(proj, e, conv_w, dc, tm):
    n = proj.shape[0]
    spec = lambda col: pl.BlockSpec((tm, dc), lambda i: (i, col))
    return pl.pallas_call(
        _conv_sample_body,
        grid=(n // tm,),
        in_specs=[spec(0), spec(1), spec(2), spec(0),
                  pl.BlockSpec(conv_w.shape, lambda i: (0, 0))],
        out_specs=[spec(0), spec(0)],
        out_shape=[jax.ShapeDtypeStruct((n, dc), BF16),
                   jax.ShapeDtypeStruct((n, dc), F32)],
        compiler_params=_cparams(1),
        name="conv_sample",
    )(proj, proj, proj, e, conv_w)


def _s5_operators(a_re, a_im, log_dt, b_re, b_im, c_re, c_im, d_skip, glu_w, glu_b):
    g, p, ch = b_re.shape
    t = S5_T
    dt = jnp.exp(log_dt)[:, None]
    mag = jnp.exp(a_re * dt)
    ar = mag * jnp.cos(a_im * dt)
    ai = mag * jnp.sin(a_im * dt)
    den = a_re * a_re + a_im * a_im
    qr = ((ar - 1.0) * a_re + ai * a_im) / den
    qi = (ai * a_re - (ar - 1.0) * a_im) / den
    bbr = qr[..., None] * b_re - qi[..., None] * b_im
    bbi = qr[..., None] * b_im + qi[..., None] * b_re
    pr, pi = [jnp.ones_like(ar)], [jnp.zeros_like(ar)]
    for _ in range(t):
        pr, pi = pr + [pr[-1] * ar - pi[-1] * ai], pi + [pr[-1] * ai + pi[-1] * ar]
    pw_r, pw_i = jnp.stack(pr), jnp.stack(pi)
    car = c_re[None] * pw_r[:, :, None, :] - c_im[None] * pw_i[:, :, None, :]
    cai = c_re[None] * pw_i[:, :, None, :] + c_im[None] * pw_r[:, :, None, :]
    kern = (jnp.einsum('jgop,gpi->jgoi', car[:t], bbr, precision=HI)
            - jnp.einsum('jgop,gpi->jgoi', cai[:t], bbi, precision=HI))
    lag = jnp.arange(t)[None, :] - jnp.arange(t)[:, None]
    toep = jnp.where((lag >= 0)[:, :, None, None, None], kern[jnp.clip(lag, 0, t - 1)], 0.0)
    m_op = toep.transpose(2, 0, 4, 1, 3).reshape(g, t * ch, t * ch)
    rev_r, rev_i = pw_r[t - 1::-1][:t], pw_i[t - 1::-1][:t]
    wsr = rev_r[..., None] * bbr[None] - rev_i[..., None] * bbi[None]
    wsi = rev_r[..., None] * bbi[None] + rev_i[..., None] * bbr[None]
    ws_op = jnp.concatenate([wsr.transpose(1, 0, 3, 2), wsi.transpose(1, 0, 3, 2)],
                            axis=-1).reshape(g, t * ch, 2 * p)
    wcr = car[1:].transpose(1, 3, 0, 2).reshape(g, p, t * ch)
    wci = cai[1:].transpose(1, 3, 0, 2).reshape(g, p, t * ch)
    wc_op = jnp.concatenate([wcr, -wci], axis=1)
    glu_op = jnp.einsum('st,gij->gsitj', jnp.eye(t, dtype=F32), glu_w).reshape(g, t * ch, t * ch)
    d_flat = jnp.tile(d_skip, (1, t))[:, None, :]
    gb_flat = jnp.tile(glu_b, (1, t))[:, None, :]

    def rot_tables(xr, xi):
        return jnp.concatenate([xr, xr], axis=-1), jnp.concatenate([-xi, xi], axis=-1)

    sr, si = pw_r[t], pw_i[t]
    p1s, p2s = [], []
    for _ in range(8):
        t1, t2 = rot_tables(sr, si)
        p1s.append(t1)
        p2s.append(t2)
        sr, si = sr * sr - si * si, 2.0 * sr * si
    h = t // 2
    p1_half, p2_half = rot_tables(pw_r[h], pw_i[h])
    return dict(
        m=m_op.astype(BF16), ws=ws_op.astype(BF16), wc=wc_op.astype(BF16), glu=glu_op.astype(BF16),
        d=d_flat, gb=gb_flat, p1=jnp.stack(p1s, axis=1), p2=jnp.stack(p2s, axis=1),
        p1_half=p1_half[:, None, :], p2_half=p2_half[:, None, :])


def _gelu_tanh(y):
    return 0.5 * y * (1.0 + jnp.tanh(GELU_C * (y + 0.044715 * (y * y * y))))


def _s5_tail(y, gl, gb):
    y = _gelu_tanh(y)
    gate = jnp.dot(y.astype(BF16), gl, preferred_element_type=F32) + gb
    return y * _sigmoid(gate)


def _s5_prompt_body(u_ref, m_ref, ws_ref, wc_ref, gl_ref, d_ref, gb_ref, p1_ref, p2_ref,
                    y_ref, hf_ref, *, groups, bsz, nc):
    half = ws_ref.shape[-1] // 2
    for g in range(groups):
        u = u_ref[g]
        ub = u.astype(BF16)
        yi = jnp.dot(ub, m_ref[g], preferred_element_type=F32)
        e = jnp.dot(ub, ws_ref[g], preferred_element_type=F32)
        cpos = lax.broadcasted_iota(I32, e.shape, 0) & (nc - 1)
        p1 = p1_ref[g]
        p2 = p2_ref[g]
        z = jnp.where(cpos >= 1, pltpu.roll(e, 1, 0), 0.0)
        d, k = 1, 0
        while d < nc:
            zs = jnp.where(cpos >= d, pltpu.roll(z, d, 0), 0.0)
            z = z + zs * p1[k:k + 1] + pltpu.roll(zs, half, 1) * p2[k:k + 1]
            d, k = d * 2, k + 1
        yc = jnp.dot(z.astype(BF16), wc_ref[g], preferred_element_type=F32)
        y = _s5_tail(yi + yc + d_ref[g] * u, gl_ref[g], gb_ref[g])
        y_ref[g] = y.astype(y_ref.dtype)
        hfin = e + z * p1[0:1] + pltpu.roll(z, half, 1) * p2[0:1]
        for b in range(bsz):
            r = b * nc + nc - 1
            hf_ref[g, b:b + 1, :] = hfin[r:r + 1, :]


def _s5_prompt(u_flat, ops, bsz, nc, groups):
    g, r, w = u_flat.shape
    p2x = ops['ws'].shape[-1]
    blk = lambda s1, s2: pl.BlockSpec((groups, s1, s2), lambda i: (i, 0, 0))
    return pl.pallas_call(
        functools.partial(_s5_prompt_body, groups=groups, bsz=bsz, nc=nc),
        grid=(g // groups,),
        in_specs=[blk(r, w), blk(w, w), blk(w, p2x), blk(p2x, w), blk(w, w),
                  blk(1, w), blk(1, w), blk(8, p2x), blk(8, p2x)],
        out_specs=[blk(r, w), blk(bsz, p2x)],
        out_shape=[jax.ShapeDtypeStruct((g, r, w), BF16),
                   jax.ShapeDtypeStruct((g, bsz, p2x), F32)],
        compiler_params=_cparams(1),
        name="s5_prompt",
    )(u_flat, ops['m'], ops['ws'], ops['wc'], ops['glu'], ops['d'], ops['gb'], ops['p1'], ops['p2'])


def _s5_sample_body(u_ref, h0_ref, m_ref, ws_ref, wc_ref, gl_ref, d_ref, gb_ref, p1_ref, p2_ref,
                    y_ref, hf_ref, *, groups):
    half = ws_ref.shape[-1] // 2
    for g in range(groups):
        u = u_ref[g]
        h0 = h0_ref[g]
        ub = u.astype(BF16)
        yi = jnp.dot(ub, m_ref[g], preferred_element_type=F32)
        e = jnp.dot(ub, ws_ref[g], preferred_element_type=F32)
        yc = jnp.dot(h0.astype(BF16), wc_ref[g], preferred_element_type=F32)
        y = _s5_tail(yi + yc + d_ref[g] * u, gl_ref[g], gb_ref[g])
        y_ref[g] = y.astype(y_ref.dtype)
        hf_ref[g] = e + h0 * p1_ref[g] + pltpu.roll(h0, half, 1) * p2_ref[g]


def _s5_sample(u_flat, h0, ops, groups):
    g, r, w = u_flat.shape
    p2x = h0.shape[-1]
    blk = lambda s1, s2: pl.BlockSpec((groups, s1, s2), lambda i: (i, 0, 0))
    return pl.pallas_call(
        functools.partial(_s5_sample_body, groups=groups),
        grid=(g // groups,),
        in_specs=[blk(r, w), blk(r, p2x), blk(w, w), blk(w, p2x), blk(p2x, w), blk(w, w),
                  blk(1, w), blk(1, w), blk(1, p2x), blk(1, p2x)],
        out_specs=[blk(r, w), blk(r, p2x)],
        out_shape=[jax.ShapeDtypeStruct((g, r, w), BF16),
                   jax.ShapeDtypeStruct((g, r, p2x), F32)],
        compiler_params=_cparams(1),
        name="s5_sample",
    )(u_flat, h0, ops['m'], ops['ws'], ops['wc'], ops['glu'], ops['d'], ops['gb'],
      ops['p1'], ops['p2'])


def _outproj_body(cnt0_ref, x_ref, yc_ref, ys_ref, wo_ref, gt_ref, g_ref, sc_ref, sh_ref,
                  wr_ref, br_ref, *rest, has_prev):
    x1_ref, h2_ref, ti_ref, gate_ref, rank_ref, cnt_ref, run_ref = rest[1:] if has_prev else rest

    @pl.when(pl.program_id(0) == 0)
    def _():
        run_ref[...] = cnt0_ref[...]

    dc = yc_ref.shape[1]
    mix = (jnp.dot(yc_ref[...], wo_ref[0:dc, :], preferred_element_type=F32)
           + jnp.dot(ys_ref[...], wo_ref[dc:, :], preferred_element_type=F32))
    x1 = x_ref[...] + gt_ref[...] * mix
    x1_ref[...] = x1
    ms = jnp.mean(x1 * x1, axis=-1, keepdims=True)
    h = x1 * lax.rsqrt(ms + EPS) * g_ref[...]
    hb = (h * (1.0 + sc_ref[...]) + sh_ref[...]).astype(BF16)
    h2_ref[...] = hb.astype(F32)
    logits =jnp.dot(hb, wr_ref[...], preferred_element_type=F32) + br_ref[...]
    tm, ne = logits.shape
    lane = lax.broadcasted_iota(I32, logits.shape, 1).astype(F32)
    work = logits
    vals, ids, sels = [], [], []
    for _ in range(TOP_K):
        m = jnp.max(work, axis=1, keepdims=True)
        idx = jnp.min(jnp.where(work == m, lane, float(ne)), axis=1, keepdims=True)
        sel = lane == idx
        vals.append(m)
        ids.append(idx)
        sels.append(sel)
        work = jnp.where(sel, -jnp.inf, work)
    exps = [jnp.exp(v - vals[0]) for v in vals]
    tot = exps[0]
    for ex in exps[1:]:
        tot = tot + ex
    gates = [ex / tot for ex in exps]
    onehot = sels[0]
    for s in sels[1:]:
        onehot = onehot | s
    onehot = onehot.astype(F32)
    row = lax.broadcasted_iota(I32, (tm, tm), 0)
    col = lax.broadcasted_iota(I32, (tm, tm), 1)
    below = (col < row).astype(BF16)
    before = jnp.dot(below, onehot.astype(BF16), preferred_element_type=F32) + run_ref[...]
    ranks = [jnp.sum(jnp.where(s, before, 0.0), axis=1, keepdims=True) for s in sels]
    run_ref[...] = run_ref[...] + jnp.sum(onehot, axis=0, keepdims=True)
    cnt_ref[...] = run_ref[...]

    wide = lax.broadcasted_iota(I32, ti_ref.shape, 1)

    def spread(cols):
        out = cols[TOP_K - 1]
        for k in range(TOP_K - 2, -1, -1):
            out = jnp.where(wide == k, cols[k], out)
        return out

    ti_ref[...] = spread(ids).astype(I32)
    gate_ref[...] = spread(gates)
    rank_ref[...] = spread(ranks).astype(I32)


def _outproj(cnt0, x, yc, ys, wo_bf, mod, per_row, rows_per_batch, g, wr_bf, br, h2_prev, n_all,
             row0, tm):
    n, d = x.shape
    dc = yc.shape[1]
    ne = wr_bf.shape[1]
    blk0 = row0 // tm
    row = lambda w: pl.BlockSpec((tm, w), lambda i: (i, 0))
    const = lambda s: pl.BlockSpec(s, lambda i: (0, 0))
    has_prev = h2_prev is not None
    in_specs = [const((1, ne)), row(d), row(dc), row(dc),
                _resident((d, d), lambda i: (0, 0)),
                _mod_spec(per_row, tm, rows_per_batch, d, 2), const((1, d)),
                _mod_spec(per_row, tm, rows_per_batch, d, 4),
                _mod_spec(per_row, tm, rows_per_batch, d, 3),
                const((d, ne)), const((1, ne))]
    args = [cnt0, x, yc, ys, wo_bf, mod, g.reshape(1, d), mod, mod, wr_bf, br.reshape(1, ne)]
    if has_prev:
        in_specs.append(pl.BlockSpec(memory_space=pl.ANY))
        args.append(h2_prev)
    return pl.pallas_call(
        functools.partial(_outproj_body, has_prev=has_prev),
        grid=(n // tm,),
        in_specs=in_specs,
        out_specs=[row(d), pl.BlockSpec((tm, d), lambda i: (blk0 + i, 0)),
                   row(128), row(128), row(128), const((1, ne))],
        out_shape=[jax.ShapeDtypeStruct((n, d), F32), jax.ShapeDtypeStruct((n_all, d), F32),
                   jax.ShapeDtypeStruct((n, 128), I32), jax.ShapeDtypeStruct((n, 128), F32),
                   jax.ShapeDtypeStruct((n, 128), I32), jax.ShapeDtypeStruct((1, ne), F32)],
        scratch_shapes=[pltpu.VMEM((1, ne), F32)],
        input_output_aliases={len(args) - 1: 1} if has_prev else {},
        compiler_params=_cparams(1),
        name="outproj_router",
    )(*args)


def _moe_body(be_ref, rb_ref, nch_ref, nrow_ref, tok0_ref, tokn_ref, h2_hbm,
              w1g_ref, w1l_ref, b1g_ref, b1l_ref, w2_ref, b2_ref,
              o_ref, xf, xb, act, wbf, sem, *, nf1):
    del be_ref, rb_ref
    b = pl.program_id(0)
    s = pl.program_id(1)
    n_chunks = nch_ref[b]

    def issue_rows(tok_ref, n_rows):
        def body(i, carry):
            r0 = i * 8
            for u in range(8):
                pltpu.make_async_copy(h2_hbm.at[pl.ds(tok_ref[0, r0 + u], 1), :],
                                      xf.at[pl.ds(r0 + u, 1), :], sem.at[0]).start()
            return carry

        lax.fori_loop(0, n_rows // 8, body, 0)

    def wait_rows(n_rows):
        size = 8
        while size <= MOE_TB:
            @pl.when((n_rows & size) != 0)
            def _(size=size):
                pltpu.make_async_copy(h2_hbm.at[pl.ds(0, size), :], xf.at[pl.ds(0, size), :],
                                      sem.at[0]).wait()

            size *= 2

    @pl.when((b == 0) & (s == 0))
    def _():
        xf[...] = jnp.zeros_like(xf)
        issue_rows(tok0_ref, nrow_ref[0])

    @pl.when(s == 0)
    def _():
        wait_rows(nrow_ref[b])

        def cast(i, carry):
            r0 = pl.multiple_of(i * MOE_CH, MOE_CH)
            xb[pl.ds(r0, MOE_CH), :] = xf[pl.ds(r0, MOE_CH), :].astype(BF16)
            return carry

        lax.fori_loop(0, n_chunks, cast, 0)
        issue_rows(tokn_ref, nrow_ref[b + 1])

    @pl.when((s < nf1) & (n_chunks > 0))
    def _():
        wbf[:, :MOE_TF] = w1g_ref[...].astype(BF16)
        wbf[:, MOE_TF:] = w1l_ref[...].astype(BF16)

        def chunk(i, carry):
            r0 = pl.multiple_of(i * MOE_CH, MOE_CH)
            h = jnp.dot(xb[pl.ds(r0, MOE_CH), :], wbf[...], preferred_element_type=F32)
            gl = jnp.minimum(h[:, :MOE_TF] + b1g_ref[...], SWIGLU_LIMIT)
            ln = jnp.clip(h[:, MOE_TF:] + b1l_ref[...], -SWIGLU_LIMIT, SWIGLU_LIMIT)
            a = gl * _sigmoid(SWIGLU_ALPHA * gl) * (ln + 1.0)
            act[s, pl.ds(r0, MOE_CH), :] = a.astype(BF16)
            return carry

        lax.fori_loop(0, n_chunks, chunk, 0)

    @pl.when((s >= nf1) & (n_chunks > 0))
    def _():
        wbf[...] = w2_ref[...].astype(BF16)

        def chunk(i, carry):
            r0 = pl.multiple_of(i * MOE_CH, MOE_CH)
            a = jnp.concatenate([act[k, pl.ds(r0, MOE_CH), :] for k in range(nf1)], axis=1)
            o_ref[pl.ds(r0, MOE_CH), :] = (
                jnp.dot(a, wbf[...], preferred_element_type=F32) + b2_ref[...])
            return carry

        lax.fori_loop(0, n_chunks, chunk, 0)

        def zero(i, carry):
            r0 = pl.multiple_of(i * MOE_CH, MOE_CH)
            o_ref[pl.ds(r0, MOE_CH), :] = jnp.zeros((MOE_CH, o_ref.shape[1]), F32)
            return carry

        lax.fori_loop(n_chunks, MOE_TB // MOE_CH, zero, 0)


def _moe_experts(h2_all, slot_tok, w1, b1, w2, b2, blk_expert, blk_rows, blk_chunks, blk_nrow,
                 n_grid_blocks):
    _, d = h2_all.shape
    ne, _, f2 = w1.shape
    f = f2 // 2
    nf1 = f // MOE_TF
    nf2 = d // MOE_TN
    nb = slot_tok.shape[0]
    assert MOE_TN == 2 * MOE_TF and f == d

    def s1(b, s, nch):
        return jnp.where(nch[b] > 0, jnp.minimum(s, nf1 - 1), nf1 - 1)

    def s2(b, s, nch):
        return jnp.where(nch[b] > 0, jnp.maximum(s - nf1, 0), nf2 - 1)

    grid_spec = pltpu.PrefetchScalarGridSpec(
        num_scalar_prefetch=4,
        grid=(n_grid_blocks, nf1 + nf2),
        in_specs=[
            pl.BlockSpec((None, 1, MOE_TB), lambda b, s, be, rb, nch, nr: (0, 0, 0),
                         memory_space=pltpu.SMEM),
            pl.BlockSpec((None, 1, MOE_TB),
                         lambda b, s, be, rb, nch, nr: (jnp.minimum(b + 1, nb - 1), 0, 0),
                         memory_space=pltpu.SMEM),
            pl.BlockSpec(memory_space=pl.ANY),
            pl.BlockSpec((None, d, MOE_TF), lambda b, s, be, rb, nch, nr: (be[b], 0, s1(b, s, nch))),
            pl.BlockSpec((None, d, MOE_TF),
                         lambda b, s, be, rb, nch, nr: (be[b], 0, nf1 + s1(b, s, nch))),
            pl.BlockSpec((None, 1, MOE_TF), lambda b, s, be, rb, nch, nr: (be[b], 0, s1(b, s, nch))),
            pl.BlockSpec((None, 1, MOE_TF),
                         lambda b, s, be, rb, nch, nr: (be[b], 0, nf1 + s1(b, s, nch))),
            pl.BlockSpec((None, f, MOE_TN), lambda b, s, be, rb, nch, nr: (be[b], 0, s2(b, s, nch))),
            pl.BlockSpec((None, 1, MOE_TN), lambda b, s, be, rb, nch, nr: (be[b], 0, s2(b, s, nch))),
        ],
        out_specs=pl.BlockSpec((MOE_TB, MOE_TN),
                               lambda b, s, be, rb, nch, nr: (rb[b], s2(b, s, nch))),
        scratch_shapes=[pltpu.VMEM((MOE_TB, d), F32), pltpu.VMEM((MOE_TB, d), BF16),
                        pltpu.VMEM((nf1, MOE_TB, MOE_TF), BF16), pltpu.VMEM((f, MOE_TN), BF16),
                        pltpu.SemaphoreType.DMA((1,))],
    )
    return pl.pallas_call(
        functools.partial(_moe_body, nf1=nf1),
        grid_spec=grid_spec,
        out_shape=jax.ShapeDtypeStruct((nb * MOE_TB, d), F32),
        compiler_params=_cparams(2),
        name="moe_experts",
    )(blk_expert, blk_rows, blk_chunks, blk_nrow, slot_tok, slot_tok, h2_all,
      w1, w1, b1.reshape(ne, 1, f2), b1.reshape(ne, 1, f2), w2, b2.reshape(ne, 1, d))


def _moe_plan(counts, top_i, rank, n_tokens, n_blocks):
    ne = counts.shape[0]
    nblk = (counts + MOE_TB - 1) // MOE_TB
    blk_end = jnp.cumsum(nblk)
    blk_start = blk_end - nblk
    n_used = blk_end[-1]
    dest = blk_start[top_i] * MOE_TB + rank
    bidx = jnp.arange(n_blocks + 1, dtype=I32)
    be = jnp.minimum(jnp.searchsorted(blk_end, bidx, side='right'), ne - 1).astype(I32)
    active = bidx < n_used
    valid = jnp.where(active, jnp.clip(counts[be] - (bidx - blk_start[be]) * MOE_TB, 0, MOE_TB), 0)
    chunks = ((valid + MOE_CH - 1) // MOE_CH).astype(I32)
    nrow = ((valid + 7) // 8 * 8).astype(I32)
    last = jnp.maximum(n_used - 1, 0)
    be = jnp.where(active, be, be[last]).astype(I32)
    rows = jnp.where(active, bidx, last).astype(I32)
    tok = jnp.arange(n_tokens * TOP_K, dtype=I32) // TOP_K
    slot_tok = jnp.zeros((n_blocks * MOE_TB,), I32).at[dest.reshape(-1)].set(tok)
    return dest, slot_tok.reshape(n_blocks, 1, MOE_TB), be, rows, chunks, nrow, n_used


def _combine_body(d0_ref, dn_ref, x1_ref, gate_ref, gt_ref, gf_ref, ys_hbm, o_ref, ybuf, sem):
    i = pl.program_id(0)
    tm = x1_ref.shape[0]
    slot = lax.rem(i, 2)

    def issue_rows(dest_ref, sl):
        def body(r, carry):
            for k in range(TOP_K):
                pltpu.make_async_copy(ys_hbm.at[pl.ds(dest_ref[0, r * TOP_K + k], 1), :],
                                      ybuf.at[sl, k, pl.ds(r, 1), :], sem.at[sl]).start()
            return carry

        lax.fori_loop(0, tm, body, 0)

    @pl.when(i == 0)
    def _():
        issue_rows(d0_ref, 0)

    @pl.when(i + 1 < pl.num_programs(0))
    def _():
        issue_rows(dn_ref, 1 - slot)

    pltpu.make_async_copy(ybuf.at[slot], ybuf.at[slot], sem.at[slot]).wait()
    gates = gate_ref[...]
    y = gates[:, 0:1] * ybuf[slot, 0]
    for k in range(1, TOP_K):
        y = y + gates[:, k:k + 1] * ybuf[slot, k]
    x2 = x1_ref[...] + gt_ref[...] * y
    ms = jnp.mean(x2 * x2, axis=-1, keepdims=True)
    o_ref[...] = x2 * lax.rsqrt(ms + EPS) * gf_ref[...]


def _combine(x1, dest, ys, gates, mod, per_row, rows_per_batch, g_final, tm):
    n, d = x1.shape
    nt = n // tm
    dest3 = dest.reshape(nt, 1, tm * TOP_K)
    return pl.pallas_call(
        _combine_body,
        grid=(nt,),
        in_specs=[pl.BlockSpec((None, 1, tm * TOP_K), lambda i: (0, 0, 0), memory_space=pltpu.SMEM),
                  pl.BlockSpec((None, 1, tm * TOP_K), lambda i: (jnp.minimum(i + 1, nt - 1), 0, 0),
                               memory_space=pltpu.SMEM),
                  pl.BlockSpec((tm, d), lambda i: (i, 0)),
                  pl.BlockSpec((tm, 128), lambda i: (i, 0)),
                  _mod_spec(per_row, tm, rows_per_batch, d, 5),
                  pl.BlockSpec((1, d), lambda i: (0, 0)),
                  pl.BlockSpec(memory_space=pl.ANY)],
        out_specs=pl.BlockSpec((tm, d), lambda i: (i, 0)),
        out_shape=jax.ShapeDtypeStruct((n, d), F32),
        scratch_shapes=[pltpu.VMEM((2, TOP_K, tm, d), F32), pltpu.SemaphoreType.DMA((2,))],
        compiler_params=_cparams(1),
        name="combine_norm",
    )(dest3, dest3, x1, gates, mod, g_final.reshape(1, d), ys)


def kernel(x_prompt, x_sample, c_prompt, c_sample, state_conv, state_ssm_re, state_ssm_im, g_mix, g_ffn, w_mod, b_mod, w_in, conv_w, ssm_a_re, ssm_a_im, ssm_log_dt, ssm_b_re, ssm_b_im, ssm_c_re, ssm_c_im, ssm_d, glu_w, glu_b, w_out, w_router, b_router, w1, b1, w2, b2, g_final):
    bp, lp, d = x_prompt.shape
    bs, ls, _ = x_sample.shape
    depth = g_mix.shape[0]
    dc = conv_w.shape[-1]
    n_grp, p_st, ch = ssm_b_re.shape[1:]
    ne = w_router.shape[-1]
    np_, ns_ = bp * lp, bs * ls
    n_tok = np_ + ns_
    assert ls == S5_T // 2 and lp % S5_T == 0 and conv_w.shape[1] == 3
    nc = lp // S5_T
    n_blocks = -(-(n_tok * TOP_K) // MOE_TB) + ne

    xp = x_prompt.reshape(np_, d)
    xs = x_sample.reshape(ns_, d)
    c_all = jnp.concatenate([c_prompt, c_sample], axis=0)
    pad = (-c_all.shape[0]) % 8
    c_all = jnp.pad(c_all, ((0, pad), (0, 0)))

    assert depth == 1
    outs = [[] for _ in range(6)]
    for l in range(depth):
        m = _modulation(c_all, w_mod[l], b_mod[l])
        mod_p = m[:bp].reshape(bp, 1, N_MOD * d)
        mod_s = jnp.repeat(m[bp:bp + bs], ls, axis=0)

        w_in_bf = w_in[l].astype(BF16)
        proj_p = _inproj(xp, mod_p, False, lp, g_mix[l], w_in_bf, 512)
        proj_s = _inproj(xs, mod_s, True, ls, g_mix[l], w_in_bf, 256)

        yc_p, tail_p = _conv_prompt(proj_p, conv_w[l], bp, lp, dc, 512)
        e_s = jnp.pad(state_conv[l], ((0, 0), (0, ls - 2), (0, 0))).reshape(ns_, dc)
        yc_s, z_s = _conv_sample(proj_s, e_s, conv_w[l], dc, 256)
        new_conv_p = tail_p[:, 6:8, :]
        new_conv_s = z_s.reshape(bs, ls, dc)[:, ls - 2:, :]

        ops = _s5_operators(ssm_a_re[l], ssm_a_im[l], ssm_log_dt[l], ssm_b_re[l], ssm_b_im[l],
                            ssm_c_re[l], ssm_c_im[l], ssm_d[l], glu_w[l], glu_b[l])
        w16, w8 = S5_T * ch, ls * ch
        u_p = proj_p[:, 3 * dc:].reshape(bp * nc, S5_T, n_grp, ch).transpose(2, 0, 1, 3)
        u_p = u_p.reshape(n_grp, bp * nc, w16)
        ys_p, hf_p = _s5_prompt(u_p, ops, bp, nc, 4)
        ys_p = ys_p.reshape(n_grp, bp * nc, S5_T, ch).transpose(1, 2, 0, 3).reshape(np_, n_grp * ch)
        new_re_p = hf_p[:, :, :p_st].transpose(1, 0, 2)
        new_im_p = hf_p[:, :, p_st:].transpose(1, 0, 2)

        u_s = proj_s[:, 3 * dc:].reshape(bs, ls, n_grp, ch).transpose(2, 0, 1, 3).reshape(n_grp, bs, w8)
        h0 = jnp.concatenate([state_ssm_re[l], state_ssm_im[l]], axis=-1).transpose(1, 0, 2)
        ops_s = dict(m=ops['m'][:, :w8, :w8], ws=ops['ws'][:, w8:, :], wc=ops['wc'][:, :, :w8],
                     glu=ops['glu'][:, :w8, :w8], d=ops['d'][:, :, :w8], gb=ops['gb'][:, :, :w8],
                     p1=ops['p1_half'], p2=ops['p2_half'])
        ys_s, hf_s = _s5_sample(u_s, h0, ops_s, 8)
        ys_s = ys_s.reshape(n_grp, bs, ls, ch).transpose(1, 2, 0, 3).reshape(ns_, n_grp * ch)
        new_re_s = hf_s[:, :, :p_st].transpose(1, 0, 2)
        new_im_s = hf_s[:, :, p_st:].transpose(1, 0, 2)

        wo_bf = w_out[l].astype(BF16)
        wr_bf = w_router[l].astype(BF16)
        cnt0 = jnp.zeros((1, ne), F32)
        x1_p, h2_all, ti_p, gate_p, rank_p, cnt1 = _outproj(
            cnt0, xp, yc_p, ys_p, wo_bf, mod_p, False, lp, g_ffn[l], wr_bf, b_router[l],
            None, n_tok, 0, 256)
        x1_s, h2_all, ti_s, gate_s, rank_s, cnt2 = _outproj(
            cnt1, xs, yc_s, ys_s, wo_bf, mod_s, True, ls, g_ffn[l], wr_bf, b_router[l],
            h2_all, n_tok, np_, 256)

        counts = cnt2[0].astype(I32)
        top_i = jnp.concatenate([ti_p[:, :TOP_K], ti_s[:, :TOP_K]], axis=0)
        rank = jnp.concatenate([rank_p[:, :TOP_K], rank_s[:, :TOP_K]], axis=0)
        dest, slot_tok, blk_e, blk_rows, blk_chunks, blk_nrow, n_used = _moe_plan(
            counts, top_i, rank, n_tok, n_blocks)
        y_sorted = _moe_experts(h2_all, slot_tok, w1[l], b1[l], w2[l], b2[l],
                                blk_e, blk_rows, blk_chunks, blk_nrow, n_used)

        xp = _combine(x1_p, dest[:np_], y_sorted, gate_p, mod_p, False, lp, g_final, 128)
        xs = _combine(x1_s, dest[np_:], y_sorted, gate_s, mod_s, True, ls, g_final, 128)
        for lst, val in zip(outs, (new_conv_p, new_re_p, new_im_p, new_conv_s, new_re_s, new_im_s)):
            lst.append(val)

    y_prompt = xp.reshape(bp, lp, d)
    y_sample = xs.reshape(bs, ls, d)
    return (y_prompt, y_sample) + tuple(jnp.stack(o) for o in outs)
```

```python
import functools
import math

import jax
import jax.numpy as jnp
from jax import lax
from jax.experimental import pallas as pl
from jax.experimental.pallas import tpu as pltpu

F32 = jnp.float32
BF16 = jnp.bfloat16
I32 = jnp.int32

EPS = 1e-6
N_MOD = 6
TOP_K = 4
SWIGLU_LIMIT = 7.0
SWIGLU_ALPHA = 1.702
GELU_C = math.sqrt(2.0 / math.pi)

S5_T = 16
MOE_TB = 1536
MOE_CH = 256
MOE_TF = 256
MOE_TN = 512
VMEM_LIMIT = 56 * 1024 * 1024
HI = lax.Precision.HIGHEST


def _cparams(n_axes):
    return pltpu.CompilerParams(dimension_semantics=("arbitrary",) * n_axes,
                                vmem_limit_bytes=VMEM_LIMIT)


def _sigmoid(x):
    return 1.0 / (1.0 + jnp.exp(-x))


def _resident(shape, index_map):
    return pl.BlockSpec(shape, index_map, pipeline_mode=pl.Buffered(1))


def _mod_body(c_ref, w_ref, b_ref, o_ref):
    c = c_ref[...]
    s = c * _sigmoid(c)
    o_ref[...] = jnp.dot(s.astype(BF16), w_ref[...].astype(BF16),
                         preferred_element_type=F32) + b_ref[...]


def _modulation(c_all, w_mod, b_mod):
    rows, d = c_all.shape
    n = w_mod.shape[1]
    tn = 1024
    return pl.pallas_call(
        _mod_body,
        grid=(n // tn,),
        in_specs=[pl.BlockSpec((rows, d), lambda j: (0, 0)),
                  pl.BlockSpec((d, tn), lambda j: (0, j)),
                  pl.BlockSpec((1, tn), lambda j: (0, j))],
        out_specs=pl.BlockSpec((rows, tn), lambda j: (0, j)),
        out_shape=jax.ShapeDtypeStruct((rows, n), F32),
        compiler_params=_cparams(1),
        name="modulation",
    )(c_all, w_mod, b_mod.reshape(1, n))


def _mod_spec(per_row, tm, rows_per_batch, d, col):
    if per_row:
        return pl.BlockSpec((tm, d), lambda i: (i, col))
    return pl.BlockSpec((None, 1, d), lambda i: ((i * tm) // rows_per_batch, 0, col))


def _inproj_body(x_ref, g_ref, sc_ref, sh_ref, w_ref, o_ref):
    x = x_ref[...]
    ms = jnp.mean(x * x, axis=-1, keepdims=True)
    h = x * lax.rsqrt(ms + EPS) * g_ref[...]
    h = h * (1.0 + sc_ref[...]) + sh_ref[...]
    o_ref[...] = jnp.dot(h.astype(BF16), w_ref[...], preferred_element_type=F32)


def _inproj(x, mod, per_row, rows_per_batch, g, w_bf, tm):
    n, d = x.shape
    dn = w_bf.shape[1]
    return pl.pallas_call(
        _inproj_body,
        grid=(n // tm,),
        in_specs=[pl.BlockSpec((tm, d), lambda i: (i, 0)),
                  pl.BlockSpec((1, d), lambda i: (0, 0)),
                  _mod_spec(per_row, tm, rows_per_batch, d, 1),
                  _mod_spec(per_row, tm, rows_per_batch, d, 0),
                  _resident((d, dn), lambda i: (0, 0))],
        out_specs=pl.BlockSpec((tm, dn), lambda i: (i, 0)),
        out_shape=jax.ShapeDtypeStruct((n, dn), F32),
        compiler_params=_cparams(1),
        name="inproj",
    )(x, g.reshape(1, d), mod, mod, w_bf)


def _conv_prompt_body(b_ref, c_ref, v_ref, w_ref, y_ref, tail_ref, carry_ref):
    @pl.when(pl.program_id(1) == 0)
    def _():
        carry_ref[...] = jnp.zeros_like(carry_ref)

    z = c_ref[...] * v_ref[...]
    tt = z.shape[0]
    zc = jnp.concatenate([carry_ref[...], z], axis=0)
    z1 = pltpu.roll(zc, 1, 0)[8:]
    z2 = pltpu.roll(zc, 2, 0)[8:]
    w = w_ref[...]
    y = w[0:1] * z2 + w[1:2] * z1 + w[2:3] * z
    y_ref[...] = (b_ref[...] * y).astype(y_ref.dtype)
    carry_ref[...] = z[tt - 8:]
    tail_ref[...] = z[tt - 8:]


def _conv_prompt(proj, conv_w, bsz, seq, dc, tt):
    nt = seq // tt
    spec = lambda col: pl.BlockSpec((tt, dc), lambda b, t: (b * nt + t, col))
    return pl.pallas_call(
        _conv_prompt_body,
        grid=(bsz, nt),
        in_specs=[spec(0), spec(1), spec(2),
                  pl.BlockSpec(conv_w.shape, lambda b, t: (0, 0))],
        out_specs=[pl.BlockSpec((tt, dc), lambda b, t: (b * nt + t, 0)),
                   pl.BlockSpec((None, 8, dc), lambda b, t: (b, 0, 0))],
        out_shape=[jax.ShapeDtypeStruct((bsz * seq, dc), BF16),
                   jax.ShapeDtypeStruct((bsz, 8, dc), F32)],
        scratch_shapes=[pltpu.VMEM((8, dc), F32)],
        compiler_params=_cparams(2),
        name="conv_prompt",
    )(proj, proj, proj, conv_w)


def _conv_sample_body(b_ref, c_ref, v_ref, e_ref, w_ref, y_ref, z_ref):
    z = c_ref[...] * v_ref[...]
    e = e_ref[...]
    rows = z.shape[0]
    tpos = lax.broadcasted_iota(I32, z.shape, 0) & 7
    z1 = jnp.where(tpos == 0, pltpu.roll(e, rows - 1, 0), pltpu.roll(z, 1, 0))
    z2 = jnp.where(tpos < 2, e, pltpu.roll(z, 2, 0))
    w = w_ref[...]
    y = w[0:1] * z2 + w[1:2] * z1 + w[2:3] * z
    y_ref[...] = (b_ref[...] * y).astype(y_ref.dtype)
    z_ref[...] = z


def _conv_sample(proj, e, conv_w, dc, tm):
    n = proj.shape[0]
    spec = lambda col: pl.BlockSpec((tm, dc), lambda i: (i, col))
    return pl.pallas_call(
        _conv_sample_body,
        grid=(n // tm,),
        in_specs=[spec(0), spec(1), spec(2), spec(0),
                  pl.BlockSpec(conv_w.shape, lambda i: (0, 0))],
        out_specs=[spec(0), spec(0)],
        out_shape=[jax.ShapeDtypeStruct((n, dc), BF16),
                   jax.ShapeDtypeStruct((n, dc), F32)],
        compiler_params=_cparams(1),
        name="conv_sample",
    )(proj, proj, proj, e, conv_w)


def _s5_operators(a_re, a_im, log_dt, b_re, b_im, c_re, c_im, d_skip, glu_w, glu_b):
    g, p, ch = b_re.shape
    t = S5_T
    dt = jnp.exp(log_dt)[:, None]
    mag = jnp.exp(a_re * dt)
    ar = mag * jnp.cos(a_im * dt)
    ai = mag * jnp.sin(a_im * dt)
    den = a_re * a_re + a_im * a_im
    qr = ((ar - 1.0) * a_re + ai * a_im) / den
    qi = (ai * a_re - (ar - 1.0) * a_im) / den
    bbr = qr[..., None] * b_re - qi[..., None] * b_im
    bbi = qr[..., None] * b_im + qi[..., None] * b_re
    pr, pi = [jnp.ones_like(ar)], [jnp.zeros_like(ar)]
    for _ in range(t):
        pr, pi = pr + [pr[-1] * ar - pi[-1] * ai], pi + [pr[-1] * ai + pi[-1] * ar]
    pw_r, pw_i = jnp.stack(pr), jnp.stack(pi)
    car = c_re[None] * pw_r[:, :, None, :] - c_im[None] * pw_i[:, :, None, :]
    cai = c_re[None] * pw_i[:, :, None, :] + c_im[None] * pw_r[:, :, None, :]
    kern = (jnp.einsum('jgop,gpi->jgoi', car[:t], bbr, precision=HI)
            - jnp.einsum('jgop,gpi->jgoi', cai[:t], bbi, precision=HI))
    lag = jnp.arange(t)[None, :] - jnp.arange(t)[:, None]
    toep = jnp.where((lag >= 0)[:, :, None, None, None], kern[jnp.clip(lag, 0, t - 1)], 0.0)
    m_op = toep.transpose(2, 0, 4, 1, 3).reshape(g, t * ch, t * ch)
    rev_r, rev_i = pw_r[t - 1::-1][:t], pw_i[t - 1::-1][:t]
    wsr = rev_r[..., None] * bbr[None] - rev_i[..., None] * bbi[None]
    wsi = rev_r[..., None] * bbi[None] + rev_i[..., None] * bbr[None]
    ws_op = jnp.concatenate([wsr.transpose(1, 0, 3, 2), wsi.transpose(1, 0, 3, 2)],
                            axis=-1).reshape(g, t * ch, 2 * p)
    wcr = car[1:].transpose(1, 3, 0, 2).reshape(g, p, t * ch)
    wci = cai[1:].transpose(1, 3, 0, 2).reshape(g, p, t * ch)
    wc_op = jnp.concatenate([wcr, -wci], axis=1)
    glu_op = jnp.einsum('st,gij->gsitj', jnp.eye(t, dtype=F32), glu_w).reshape(g, t * ch, t * ch)
    d_flat = jnp.tile(d_skip, (1, t))[:, None, :]
    gb_flat = jnp.tile(glu_b, (1, t))[:, None, :]

    def rot_tables(xr, xi):
        return jnp.concatenate([xr, xr], axis=-1), jnp.concatenate([-xi, xi], axis=-1)

    sr, si = pw_r[t], pw_i[t]
    p1s, p2s = [], []
    for _ in range(8):
        t1, t2 = rot_tables(sr, si)
        p1s.append(t1)
        p2s.append(t2)
        sr, si = sr * sr - si * si, 2.0 * sr * si
    h = t // 2
    p1_half, p2_half = rot_tables(pw_r[h], pw_i[h])
    return dict(
        m=m_op.astype(BF16), ws=ws_op.astype(BF16), wc=wc_op.astype(BF16), glu=glu_op.astype(BF16),
        d=d_flat, gb=gb_flat, p1=jnp.stack(p1s, axis=1), p2=jnp.stack(p2s, axis=1),
        p1_half=p1_half[:, None, :], p2_half=p2_half[:, None, :])


def _gelu_tanh(y):
    return 0.5 * y * (1.0 + jnp.tanh(GELU_C * (y + 0.044715 * (y * y * y))))


def _s5_tail(y, gl, gb):
    y = _gelu_tanh(y)
    gate = jnp.dot(y.astype(BF16), gl, preferred_element_type=F32) + gb
    return y * _sigmoid(gate)


S5_OCT = 8


def _fold_time(rows, q, blk):
    parts = []
    for h in range(len(rows) // S5_OCT):
        acc = None
        for tt in range(S5_OCT):
            r = rows[S5_OCT * h + tt]
            shift = (16 * (tt - q)) % 128
            if shift:
                r = pltpu.roll(r, shift, 1)
            acc = r if acc is None else jnp.where(blk == tt, r, acc)
        parts.append(acc)
    return parts[0] if len(parts) == 1 else jnp.concatenate(parts, axis=1)


def _unfold_time(ys, t, blk):
    h, tt = divmod(t, S5_OCT)
    acc = None
    for q in range(S5_OCT):
        r = ys[q][:, 128 * h:128 * (h + 1)]
        shift = (16 * (q - tt)) % 128
        if shift:
            r = pltpu.roll(r, shift, 1)
        acc = r if acc is None else jnp.where(blk == q, r, acc)
    return acc


def _s5_prompt_body(u_ref, m_ref, ws_ref, wc_ref, gl_ref, d_ref, gb_ref, p1_ref, p2_ref,
                    y_ref, hf_ref, uf, yf, *, bsz, nc):
    half = ws_ref.shape[-1] // 2
    seq = nc * S5_T
    blk = lax.broadcasted_iota(I32, (nc, 128), 1) // 16

    def fold(b, carry):
        rows = [u_ref[pl.ds(pl.multiple_of(b * seq, seq) + t, nc, stride=S5_T), :]
                for t in range(S5_T)]
        for q in range(S5_OCT):
            uf[q, pl.ds(pl.multiple_of(b * nc, nc), nc), :] = _fold_time(rows, q, blk)
        return carry

    lax.fori_loop(0, bsz, fold, 0)

    for g in range(S5_OCT):
        u = uf[g]
        ub = u.astype(BF16)
        yi = jnp.dot(ub, m_ref[g], preferred_element_type=F32)
        e = jnp.dot(ub, ws_ref[g], preferred_element_type=F32)
        cpos = lax.broadcasted_iota(I32, e.shape, 0) & (nc - 1)
        p1 = p1_ref[g]
        p2 = p2_ref[g]
        z = jnp.where(cpos >= 1, pltpu.roll(e, 1, 0), 0.0)
        d, k = 1, 0
        while d < nc:
            zs = jnp.where(cpos >= d, pltpu.roll(z, d, 0), 0.0)
            z = z + zs * p1[k:k + 1] + pltpu.roll(zs, half, 1) * p2[k:k + 1]
            d, k = d * 2, k + 1
        yc = jnp.dot(z.astype(BF16), wc_ref[g], preferred_element_type=F32)
        yf[g] = _s5_tail(yi + yc + d_ref[g] * u, gl_ref[g], gb_ref[g])
        hfin = e + z * p1[0:1] + pltpu.roll(z, half, 1) * p2[0:1]
        for b in range(bsz):
            r = b * nc + nc - 1
            hf_ref[g, b:b + 1, :] = hfin[r:r + 1, :]

    def unfold(b, carry):
        ys = [yf[q, pl.ds(pl.multiple_of(b * nc, nc), nc), :] for q in range(S5_OCT)]
        for t in range(S5_T):
            y_ref[pl.ds(pl.multiple_of(b * seq, seq) + t, nc, stride=S5_T), :] = (
                _unfold_time(ys, t, blk))
        return carry

    lax.fori_loop(0, bsz, unfold, 0)


def _s5_prompt(proj, col0, ops, bsz, nc):
    n = proj.shape[0]
    g = ops['m'].shape[0]
    w = ops['m'].shape[-1]
    p2x = ops['ws'].shape[-1]
    blk = lambda s1, s2: pl.BlockSpec((S5_OCT, s1, s2), lambda i: (i, 0, 0))
    return pl.pallas_call(
        functools.partial(_s5_prompt_body, bsz=bsz, nc=nc),
        grid=(g // S5_OCT,),
        in_specs=[pl.BlockSpec((n, 128), lambda i: (0, col0 + i)),
                  blk(w, w), blk(w, p2x), blk(p2x, w), blk(w, w),
                  blk(1, w), blk(1, w), blk(8, p2x), blk(8, p2x)],
        out_specs=[pl.BlockSpec((n, 128), lambda i: (0, i)), blk(bsz, p2x)],
        out_shape=[jax.ShapeDtypeStruct((n, g * 16), F32),
                   jax.ShapeDtypeStruct((g, bsz, p2x), F32)],
        scratch_shapes=[pltpu.VMEM((S5_OCT, bsz * nc, w), F32),
                        pltpu.VMEM((S5_OCT, bsz * nc, w), F32)],
        compiler_params=_cparams(1),
        name="s5_prompt",
    )(proj, ops['m'], ops['ws'], ops['wc'], ops['glu'], ops['d'], ops['gb'], ops['p1'], ops['p2'])


def _s5_sample_body(u_ref, h0_ref, m_ref, ws_ref, wc_ref, gl_ref, d_ref, gb_ref, p1_ref, p2_ref,
                    y_ref, hf_ref, *, steps):
    half = ws_ref.shape[-1] // 2
    nseq = u_ref.shape[0] // steps
    blk = lax.broadcasted_iota(I32, (nseq, 128), 1) // 16
    rows = [u_ref[pl.ds(t, nseq, stride=steps), :] for t in range(steps)]
    ys = []
    for g in range(S5_OCT):
        u = _fold_time(rows, g, blk)
        h0 = h0_ref[g]
        ub = u.astype(BF16)
        yi = jnp.dot(ub, m_ref[g], preferred_element_type=F32)
        e = jnp.dot(ub, ws_ref[g], preferred_element_type=F32)
        yc = jnp.dot(h0.astype(BF16), wc_ref[g], preferred_element_type=F32)
        ys.append(_s5_tail(yi + yc + d_ref[g] * u, gl_ref[g], gb_ref[g]))
        hf_ref[g] = e + h0 * p1_ref[g] + pltpu.roll(h0, half, 1) * p2_ref[g]
    for t in range(steps):
        y_ref[pl.ds(t, nseq, stride=steps), :] = _unfold_time(ys, t, blk)


def _s5_sample(proj, col0, h0, ops, steps):
    n = proj.shape[0]
    g, r, p2x = h0.shape
    w = ops['m'].shape[-1]
    blk = lambda s1, s2: pl.BlockSpec((S5_OCT, s1, s2), lambda i: (i, 0, 0))
    return pl.pallas_call(
        functools.partial(_s5_sample_body, steps=steps),
        grid=(g // S5_OCT,),
        in_specs=[pl.BlockSpec((n, 128), lambda i: (0, col0 + i)),
                  blk(r, p2x), blk(w, w), blk(w, p2x), blk(p2x, w), blk(w, w),
                  blk(1, w), blk(1, w), blk(1, p2x), blk(1, p2x)],
        out_specs=[pl.BlockSpec((n, 128), lambda i: (0, i)), blk(r, p2x)],
        out_shape=[jax.ShapeDtypeStruct((n, g * 16), F32),
                   jax.ShapeDtypeStruct((g, r, p2x), F32)],
        compiler_params=_cparams(1),
        name="s5_sample",
    )(proj, h0, ops['m'], ops['ws'], ops['wc'], ops['glu'], ops['d'], ops['gb'],
      ops['p1'], ops['p2'])


def _outproj_body(cnt0_ref, x_ref, yc_ref, ys_ref, wo_ref, gt_ref, g_ref, sc_ref, sh_ref,
                  wr_ref, br_ref, *rest, has_prev):
    x1_ref, h2_ref, ti_ref, gate_ref, rank_ref, cnt_ref, run_ref = rest[1:] if has_prev else rest

    @pl.when(pl.program_id(0) == 0)
    def _():
        run_ref[...] = cnt0_ref[...]

    dc = yc_ref.shape[1]
    mix = (jnp.dot(yc_ref[...], wo_ref[0:dc, :], preferred_element_type=F32)
           + jnp.dot(ys_ref[...].astype(BF16), wo_ref[dc:, :], preferred_element_type=F32))
    x1 = x_ref[...] + gt_ref[...] * mix
    x1_ref[...] = x1
    ms = jnp.mean(x1 * x1, axis=-1, keepdims=True)
    h = x1 * lax.rsqrt(ms + EPS) * g_ref[...]
    hb = (h * (1.0 + sc_ref[...]) + sh_ref[...]).astype(BF16)
    h2_ref[...] = hb.astype(F32)
    logits =jnp.dot(hb, wr_ref[...], preferred_element_type=F32) + br_ref[...]
    tm, ne = logits.shape
    lane = lax.broadcasted_iota(I32, logits.shape, 1).astype(F32)
    work = logits
    vals, ids, sels = [], [], []
    for _ in range(TOP_K):
        m = jnp.max(work, axis=1, keepdims=True)
        idx = jnp.min(jnp.where(work == m, lane, float(ne)), axis=1, keepdims=True)
        sel = lane == idx
        vals.append(m)
        ids.append(idx)
        sels.append(sel)
        work = jnp.where(sel, -jnp.inf, work)
    exps = [jnp.exp(v - vals[0]) for v in vals]
    tot = exps[0]
    for ex in exps[1:]:
        tot = tot + ex
    gates = [ex / tot for ex in exps]
    onehot = sels[0]
    for s in sels[1:]:
        onehot = onehot | s
    onehot = onehot.astype(F32)
    row = lax.broadcasted_iota(I32, (tm, tm), 0)
    col = lax.broadcasted_iota(I32, (tm, tm), 1)
    below = (col < row).astype(BF16)
    before = jnp.dot(below, onehot.astype(BF16), preferred_element_type=F32) + run_ref[...]
    ranks = [jnp.sum(jnp.where(s, before, 0.0), axis=1, keepdims=True) for s in sels]
    run_ref[...] = run_ref[...] + jnp.sum(onehot, axis=0, keepdims=True)
    cnt_ref[...] = run_ref[...]

    wide = lax.broadcasted_iota(I32, ti_ref.shape, 1)

    def spread(cols):
        out = cols[TOP_K - 1]
        for k in range(TOP_K - 2, -1, -1):
            out = jnp.where(wide == k, cols[k], out)
        return out

    ti_ref[...] = spread(ids).astype(I32)
    gate_ref[...] = spread(gates)
    rank_ref[...] = spread(ranks).astype(I32)


def _outproj(cnt0, x, yc, ys, wo_bf, mod, per_row, rows_per_batch, g, wr_bf, br, h2_prev, n_all,
             row0, tm):
    n, d = x.shape
    dc = yc.shape[1]
    ne = wr_bf.shape[1]
    blk0 = row0 // tm
    row = lambda w: pl.BlockSpec((tm, w), lambda i: (i, 0))
    const = lambda s: pl.BlockSpec(s, lambda i: (0, 0))
    has_prev = h2_prev is not None
    in_specs = [const((1, ne)), row(d), row(dc), row(dc),
                _resident((d, d), lambda i: (0, 0)),
                _mod_spec(per_row, tm, rows_per_batch, d, 2), const((1, d)),
                _mod_spec(per_row, tm, rows_per_batch, d, 4),
                _mod_spec(per_row, tm, rows_per_batch, d, 3),
                const((d, ne)), const((1, ne))]
    args = [cnt0, x, yc, ys, wo_bf, mod, g.reshape(1, d), mod, mod, wr_bf, br.reshape(1, ne)]
    if has_prev:
        in_specs.append(pl.BlockSpec(memory_space=pl.ANY))
        args.append(h2_prev)
    return pl.pallas_call(
        functools.partial(_outproj_body, has_prev=has_prev),
        grid=(n // tm,),
        in_specs=in_specs,
        out_specs=[row(d), pl.BlockSpec((tm, d), lambda i: (blk0 + i, 0)),
                   row(128), row(128), row(128), const((1, ne))],
        out_shape=[jax.ShapeDtypeStruct((n, d), F32), jax.ShapeDtypeStruct((n_all, d), F32),
                   jax.ShapeDtypeStruct((n, 128), I32), jax.ShapeDtypeStruct((n, 128), F32),
                   jax.ShapeDtypeStruct((n, 128), I32), jax.ShapeDtypeStruct((1, ne), F32)],
        scratch_shapes=[pltpu.VMEM((1, ne), F32)],
        input_output_aliases={len(args) - 1: 1} if has_prev else {},
        compiler_params=_cparams(1),
        name="outproj_router",
    )(*args)


def _moe_body(be_ref, rb_ref, nch_ref, nrow_ref, tok0_ref, tokn_ref, h2_hbm,
              w1g_ref, w1l_ref, b1g_ref, b1l_ref, w2_ref, b2_ref,
              o_ref, xf, xb, act, sem, *, nf1):
    del be_ref, rb_ref
    b = pl.program_id(0)
    s = pl.program_id(1)
    n_chunks = nch_ref[b]

    def issue_rows(tok_ref, n_rows):
        def body(i, carry):
            r0 = i * 8
            for u in range(8):
                pltpu.make_async_copy(h2_hbm.at[pl.ds(tok_ref[0, r0 + u], 1), :],
                                      xf.at[pl.ds(r0 + u, 1), :], sem.at[0]).start()
            return carry

        lax.fori_loop(0, n_rows // 8, body, 0)

    def wait_rows(n_rows):
        size = 8
        while size <= MOE_TB:
            @pl.when((n_rows & size) != 0)
            def _(size=size):
                pltpu.make_async_copy(h2_hbm.at[pl.ds(0, size), :], xf.at[pl.ds(0, size), :],
                                      sem.at[0]).wait()

            size *= 2

    @pl.when((b == 0) & (s == 0))
    def _():
        xf[...] = jnp.zeros_like(xf)
        issue_rows(tok0_ref, nrow_ref[0])

    @pl.when(s == 0)
    def _():
        wait_rows(nrow_ref[b])

        def cast(i, carry):
            r0 = pl.multiple_of(i * MOE_CH, MOE_CH)
            xb[pl.ds(r0, MOE_CH), :] = xf[pl.ds(r0, MOE_CH), :].astype(BF16)
            return carry

        lax.fori_loop(0, n_chunks, cast, 0)
        issue_rows(tokn_ref, nrow_ref[b + 1])

    def chunk_pairs(one):
        def pair(i, carry):
            one(pl.multiple_of(i * (2 * MOE_CH), 2 * MOE_CH), 2 * MOE_CH)
            return carry

        lax.fori_loop(0, n_chunks // 2, pair, 0)

        @pl.when(n_chunks % 2 == 1)
        def _():
            one(pl.multiple_of((n_chunks - 1) * MOE_CH, MOE_CH), MOE_CH)

    @pl.when((s < nf1) & (n_chunks > 0))
    def _():
        def one(r0, rows):
            x = xb[pl.ds(r0, rows), :]
            hg = jnp.dot(x, w1g_ref[...].astype(BF16), preferred_element_type=F32)
            hl = jnp.dot(x, w1l_ref[...].astype(BF16), preferred_element_type=F32)
            gl = jnp.minimum(hg + b1g_ref[...], SWIGLU_LIMIT)
            ln = jnp.clip(hl + b1l_ref[...], -SWIGLU_LIMIT, SWIGLU_LIMIT)
            a = gl * _sigmoid(SWIGLU_ALPHA * gl) * (ln + 1.0)
            act[s, pl.ds(r0, rows), :] = a.astype(BF16)

        chunk_pairs(one)

    @pl.when((s >= nf1) & (n_chunks > 0))
    def _():
        def one(r0, rows):
            a = jnp.concatenate([act[k, pl.ds(r0, rows), :] for k in range(nf1)], axis=1)
            o_ref[pl.ds(r0, rows), :] = (
                jnp.dot(a, w2_ref[...].astype(BF16), preferred_element_type=F32) + b2_ref[...])

        chunk_pairs(one)

        def zero(i, carry):
            r0 = pl.multiple_of(i * MOE_CH, MOE_CH)
            o_ref[pl.ds(r0, MOE_CH), :] = jnp.zeros((MOE_CH, o_ref.shape[1]), F32)
            return carry

        lax.fori_loop(n_chunks, MOE_TB // MOE_CH, zero, 0)


def _moe_experts(h2_all, slot_tok, w1, b1, w2, b2, blk_expert, blk_rows, blk_chunks, blk_nrow,
                 n_grid_blocks):
    _, d = h2_all.shape
    ne, _, f2 = w1.shape
    f = f2 // 2
    nf1 = f // MOE_TF
    nf2 = d // MOE_TN
    nb = slot_tok.shape[0]

    def s1(b, s, nch):
        return jnp.where(nch[b] > 0, jnp.minimum(s, nf1 - 1), nf1 - 1)

    def s2(b, s, nch):
        return jnp.where(nch[b] > 0, jnp.maximum(s - nf1, 0), nf2 - 1)

    grid_spec = pltpu.PrefetchScalarGridSpec(
        num_scalar_prefetch=4,
        grid=(n_grid_blocks, nf1 + nf2),
        in_specs=[
            pl.BlockSpec((None, 1, MOE_TB), lambda b, s, be, rb, nch, nr: (0, 0, 0),
                         memory_space=pltpu.SMEM),
            pl.BlockSpec((None, 1, MOE_TB),
                         lambda b, s, be, rb, nch, nr: (jnp.minimum(b + 1, nb - 1), 0, 0),
                         memory_space=pltpu.SMEM),
            pl.BlockSpec(memory_space=pl.ANY),
            pl.BlockSpec((None, d, MOE_TF), lambda b, s, be, rb, nch, nr: (be[b], 0, s1(b, s, nch))),
            pl.BlockSpec((None, d, MOE_TF),
                         lambda b, s, be, rb, nch, nr: (be[b], 0, nf1 + s1(b, s, nch))),
            pl.BlockSpec((None, 1, MOE_TF), lambda b, s, be, rb, nch, nr: (be[b], 0, s1(b, s, nch))),
            pl.BlockSpec((None, 1, MOE_TF),
                         lambda b, s, be, rb, nch, nr: (be[b], 0, nf1 + s1(b, s, nch))),
            pl.BlockSpec((None, f, MOE_TN), lambda b, s, be, rb, nch, nr: (be[b], 0, s2(b, s, nch))),
            pl.BlockSpec((None, 1, MOE_TN), lambda b, s, be, rb, nch, nr: (be[b], 0, s2(b, s, nch))),
        ],
        out_specs=pl.BlockSpec((MOE_TB, MOE_TN),
                               lambda b, s, be, rb, nch, nr: (rb[b], s2(b, s, nch))),
        scratch_shapes=[pltpu.VMEM((MOE_TB, d), F32), pltpu.VMEM((MOE_TB, d), BF16),
                        pltpu.VMEM((nf1, MOE_TB, MOE_TF), BF16),
                        pltpu.SemaphoreType.DMA((1,))],
    )
    return pl.pallas_call(
        functools.partial(_moe_body, nf1=nf1),
        grid_spec=grid_spec,
        out_shape=jax.ShapeDtypeStruct((nb * MOE_TB, d), F32),
        compiler_params=_cparams(2),
        name="moe_experts",
    )(blk_expert, blk_rows, blk_chunks, blk_nrow, slot_tok, slot_tok, h2_all,
      w1, w1, b1.reshape(ne, 1, f2), b1.reshape(ne, 1, f2), w2, b2.reshape(ne, 1, d))


def _moe_plan(counts, top_i, rank, n_tokens, n_blocks):
    ne = counts.shape[0]
    nblk = (counts + MOE_TB - 1) // MOE_TB
    blk_end = jnp.cumsum(nblk)
    blk_start = blk_end - nblk
    n_used = blk_end[-1]
    dest = blk_start[top_i] * MOE_TB + rank
    bidx = jnp.arange(n_blocks + 1, dtype=I32)
    be = jnp.minimum(jnp.searchsorted(blk_end, bidx, side='right'), ne - 1).astype(I32)
    active = bidx < n_used
    valid = jnp.where(active, jnp.clip(counts[be] - (bidx - blk_start[be]) * MOE_TB, 0, MOE_TB), 0)
    chunks = ((valid + MOE_CH - 1) // MOE_CH).astype(I32)
    nrow = ((valid + 7) // 8 * 8).astype(I32)
    last = jnp.maximum(n_used - 1, 0)
    be = jnp.where(active, be, be[last]).astype(I32)
    rows = jnp.where(active, bidx, last).astype(I32)
    tok = jnp.arange(n_tokens * TOP_K, dtype=I32) // TOP_K
    slot_tok = jnp.zeros((n_blocks * MOE_TB,), I32).at[dest.reshape(-1)].set(tok)
    return dest, slot_tok.reshape(n_blocks, 1, MOE_TB), be, rows, chunks, nrow, n_used


def _combine_body(d0_ref, dn_ref, x1_ref, gate_ref, gt_ref, gf_ref, ys_hbm, o_ref, ybuf, sem):
    i = pl.program_id(0)
    tm = x1_ref.shape[0]
    slot = lax.rem(i, 2)

    def issue_rows(dest_ref, sl):
        def body(r, carry):
            for k in range(TOP_K):
                pltpu.make_async_copy(ys_hbm.at[pl.ds(dest_ref[0, r * TOP_K + k], 1), :],
                                      ybuf.at[sl, k, pl.ds(r, 1), :], sem.at[sl]).start()
            return carry

        lax.fori_loop(0, tm, body, 0)

    @pl.when(i == 0)
    def _():
        issue_rows(d0_ref, 0)

    @pl.when(i + 1 < pl.num_programs(0))
    def _():
        issue_rows(dn_ref, 1 - slot)

    pltpu.make_async_copy(ybuf.at[slot], ybuf.at[slot], sem.at[slot]).wait()
    gates = gate_ref[...]
    y = gates[:, 0:1] * ybuf[slot, 0]
    for k in range(1, TOP_K):
        y = y + gates[:, k:k + 1] * ybuf[slot, k]
    x2 = x1_ref[...] + gt_ref[...] * y
    ms = jnp.mean(x2 * x2, axis=-1, keepdims=True)
    o_ref[...] = x2 * lax.rsqrt(ms + EPS) * gf_ref[...]


def _combine(x1, dest, ys, gates, mod, per_row, rows_per_batch, g_final, tm):
    n, d = x1.shape
    nt = n // tm
    dest3 = dest.reshape(nt, 1, tm * TOP_K)
    return pl.pallas_call(
        _combine_body,
        grid=(nt,),
        in_specs=[pl.BlockSpec((None, 1, tm * TOP_K), lambda i: (0, 0, 0), memory_space=pltpu.SMEM),
                  pl.BlockSpec((None, 1, tm * TOP_K), lambda i: (jnp.minimum(i + 1, nt - 1), 0, 0),
                               memory_space=pltpu.SMEM),
                  pl.BlockSpec((tm, d), lambda i: (i, 0)),
                  pl.BlockSpec((tm, 128), lambda i: (i, 0)),
                  _mod_spec(per_row, tm, rows_per_batch, d, 5),
                  pl.BlockSpec((1, d), lambda i: (0, 0)),
                  pl.BlockSpec(memory_space=pl.ANY)],
        out_specs=pl.BlockSpec((tm, d), lambda i: (i, 0)),
        out_shape=jax.ShapeDtypeStruct((n, d), F32),
        scratch_shapes=[pltpu.VMEM((2, TOP_K, tm, d), F32), pltpu.SemaphoreType.DMA((2,))],
        compiler_params=_cparams(1),
        name="combine_norm",
    )(dest3, dest3, x1, gates, mod, g_final.reshape(1, d), ys)


def kernel(x_prompt, x_sample, c_prompt, c_sample, state_conv, state_ssm_re, state_ssm_im, g_mix, g_ffn, w_mod, b_mod, w_in, conv_w, ssm_a_re, ssm_a_im, ssm_log_dt, ssm_b_re, ssm_b_im, ssm_c_re, ssm_c_im, ssm_d, glu_w, glu_b, w_out, w_router, b_router, w1, b1, w2, b2, g_final):
    bp, lp, d = x_prompt.shape
    bs, ls, _ = x_sample.shape
    depth = g_mix.shape[0]
    dc = conv_w.shape[-1]
    n_grp, p_st, ch = ssm_b_re.shape[1:]
    ne = w_router.shape[-1]
    np_, ns_ = bp * lp, bs * ls
    n_tok = np_ + ns_
    assert ls == S5_T // 2 and lp % S5_T == 0 and conv_w.shape[1] == 3
    nc = lp // S5_T
    n_blocks = -(-(n_tok * TOP_K) // MOE_TB) + ne

    xp = x_prompt.reshape(np_, d)
    xs = x_sample.reshape(ns_, d)
    c_all = jnp.concatenate([c_prompt, c_sample], axis=0)
    pad = (-c_all.shape[0]) % 8
    c_all = jnp.pad(c_all, ((0, pad), (0, 0)))

    assert depth == 1
    outs = [[] for _ in range(6)]
    for l in range(depth):
        m = _modulation(c_all, w_mod[l], b_mod[l])
        mod_p = m[:bp].reshape(bp, 1, N_MOD * d)
        mod_s = jnp.repeat(m[bp:bp + bs], ls, axis=0)

        w_in_bf = w_in[l].astype(BF16)
        proj_p = _inproj(xp, mod_p, False, lp, g_mix[l], w_in_bf, 512)
        proj_s = _inproj(xs, mod_s, True, ls, g_mix[l], w_in_bf, 256)

        yc_p, tail_p = _conv_prompt(proj_p, conv_w[l], bp, lp, dc, 512)
        e_s = jnp.pad(state_conv[l], ((0, 0), (0, ls - 2), (0, 0))).reshape(ns_, dc)
        yc_s, z_s = _conv_sample(proj_s, e_s, conv_w[l], dc, 256)
        new_conv_p = tail_p[:, 6:8, :]
        new_conv_s = z_s.reshape(bs, ls, dc)[:, ls - 2:, :]

        ops = _s5_operators(ssm_a_re[l], ssm_a_im[l], ssm_log_dt[l], ssm_b_re[l], ssm_b_im[l],
                            ssm_c_re[l], ssm_c_im[l], ssm_d[l], glu_w[l], glu_b[l])
        w8 = ls * ch
        ucol0 = (3 * dc) // 128
        ys_p, hf_p = _s5_prompt(proj_p, ucol0, ops, bp, nc)
        new_re_p = hf_p[:, :, :p_st].transpose(1, 0, 2)
        new_im_p = hf_p[:, :, p_st:].transpose(1, 0, 2)

        h0 = jnp.concatenate([state_ssm_re[l], state_ssm_im[l]], axis=-1).transpose(1, 0, 2)
        ops_s = dict(m=ops['m'][:, :w8, :w8], ws=ops['ws'][:, w8:, :], wc=ops['wc'][:, :, :w8],
                     glu=ops['glu'][:, :w8, :w8], d=ops['d'][:, :, :w8], gb=ops['gb'][:, :, :w8],
                     p1=ops['p1_half'], p2=ops['p2_half'])
        ys_s, hf_s = _s5_sample(proj_s, ucol0, h0, ops_s, ls)
        new_re_s = hf_s[:, :, :p_st].transpose(1, 0, 2)
        new_im_s = hf_s[:, :, p_st:].transpose(1, 0, 2)

        wo_bf = w_out[l].astype(BF16)
        wr_bf = w_router[l].astype(BF16)
        cnt0 = jnp.zeros((1, ne), F32)
        x1_p, h2_all, ti_p, gate_p, rank_p, cnt1 = _outproj(
            cnt0, xp, yc_p, ys_p, wo_bf, mod_p, False, lp, g_ffn[l], wr_bf, b_router[l],
            None, n_tok, 0, 256)
        x1_s, h2_all, ti_s, gate_s, rank_s, cnt2 = _outproj(
            cnt1, xs, yc_s, ys_s, wo_bf, mod_s, True, ls, g_ffn[l], wr_bf, b_router[l],
            h2_all, n_tok, np_, 256)

        counts = cnt2[0].astype(I32)
        top_i = jnp.concatenate([ti_p[:, :TOP_K], ti_s[:, :TOP_K]], axis=0)
        rank = jnp.concatenate([rank_p[:, :TOP_K], rank_s[:, :TOP_K]], axis=0)
        dest, slot_tok, blk_e, blk_rows, blk_chunks, blk_nrow, n_used = _moe_plan(
            counts, top_i, rank, n_tok, n_blocks)
        y_sorted = _moe_experts(h2_all, slot_tok, w1[l], b1[l], w2[l], b2[l],
                                blk_e, blk_rows, blk_chunks, blk_nrow, n_used)

        xp = _combine(x1_p, dest[:np_], y_sorted, gate_p, mod_p, False, lp, g_final, 128)
        xs = _combine(x1_s, dest[np_:], y_sorted, gate_s, mod_s, True, ls, g_final, 128)
        for lst, val in zip(outs, (new_conv_p, new_re_p, new_im_p, new_conv_s, new_re_s, new_im_s)):
            lst.append(val)

    y_prompt = xp.reshape(bp, lp, d)
    y_sample = xs.reshape(bs, ls, d)
    return (y_prompt, y_sample) + tuple(jnp.stack(o) for o in outs)
```

```python
import functools
import math

import jax
import jax.numpy as jnp
from jax import lax
from jax.experimental import pallas as pl
from jax.experimental.pallas import tpu as pltpu

F32 = jnp.float32
BF16 = jnp.bfloat16
I32 = jnp.int32

EPS = 1e-6
N_MOD = 6
TOP_K = 4
SWIGLU_LIMIT = 7.0
SWIGLU_ALPHA = 1.702
GELU_C = math.sqrt(2.0 / math.pi)

S5_T = 16
MOE_TB = 1536
MOE_CH = 256
MOE_TF = 512
MOE_TN = 512
VMEM_LIMIT = 56 * 1024 * 1024
HI = lax.Precision.HIGHEST


def _cparams(n_axes):
    return pltpu.CompilerParams(dimension_semantics=("arbitrary",) * n_axes,
                                vmem_limit_bytes=VMEM_LIMIT)


def _sigmoid(x):
    return 1.0 / (1.0 + jnp.exp(-x))


def _pack_bf16_pair(lo, hi):
    lo_bits = lax.bitcast_convert_type(lo, jnp.uint32) >> 16
    hi_bits = lax.bitcast_convert_type(hi, jnp.uint32) & jnp.uint32(0xFFFF0000)
    return lo_bits | hi_bits


def _unpack_bf16_pair(packed):
    lo = lax.bitcast_convert_type(packed << 16, F32)
    hi = lax.bitcast_convert_type(packed & jnp.uint32(0xFFFF0000), F32)
    return lo.astype(BF16), hi.astype(BF16)


def _resident(shape, index_map):
    return pl.BlockSpec(shape, index_map, pipeline_mode=pl.Buffered(1))


def _mod_body(c_ref, w_ref, b_ref, o_ref):
    c = c_ref[...]
    s = c * _sigmoid(c)
    o_ref[...] = jnp.dot(s.astype(BF16), w_ref[...].astype(BF16),
                         preferred_element_type=F32) + b_ref[...]


def _modulation(c_all, w_mod, b_mod):
    rows, d = c_all.shape
    n = w_mod.shape[1]
    tn = 1024
    return pl.pallas_call(
        _mod_body,
        grid=(n // tn,),
        in_specs=[pl.BlockSpec((rows, d), lambda j: (0, 0)),
                  pl.BlockSpec((d, tn), lambda j: (0, j)),
                  pl.BlockSpec((1, tn), lambda j: (0, j))],
        out_specs=pl.BlockSpec((rows, tn), lambda j: (0, j)),
        out_shape=jax.ShapeDtypeStruct((rows, n), F32),
        compiler_params=_cparams(1),
        name="modulation",
    )(c_all, w_mod, b_mod.reshape(1, n))


def _mod_spec(per_row, tm, rows_per_batch, d, col):
    if per_row:
        return pl.BlockSpec((tm, d), lambda i: (i, col))
    return pl.BlockSpec((None, 1, d), lambda i: ((i * tm) // rows_per_batch, 0, col))


def _inproj_body(x_ref, g_ref, sc_ref, sh_ref, w_ref, o_ref):
    x = x_ref[...]
    ms = jnp.mean(x * x, axis=-1, keepdims=True)
    h = x * lax.rsqrt(ms + EPS) * g_ref[...]
    h = h * (1.0 + sc_ref[...]) + sh_ref[...]
    o_ref[...] = jnp.dot(h.astype(BF16), w_ref[...], preferred_element_type=F32)


def _inproj(x, mod, per_row, rows_per_batch, g, w_bf, tm):
    n, d = x.shape
    dn = w_bf.shape[1]
    return pl.pallas_call(
        _inproj_body,
        grid=(n // tm,),
        in_specs=[pl.BlockSpec((tm, d), lambda i: (i, 0)),
                  pl.BlockSpec((1, d), lambda i: (0, 0)),
                  _mod_spec(per_row, tm, rows_per_batch, d, 1),
                  _mod_spec(per_row, tm, rows_per_batch, d, 0),
                  _resident((d, dn), lambda i: (0, 0))],
        out_specs=pl.BlockSpec((tm, dn), lambda i: (i, 0)),
        out_shape=jax.ShapeDtypeStruct((n, dn), F32),
        compiler_params=_cparams(1),
        name="inproj",
    )(x, g.reshape(1, d), mod, mod, w_bf)


def _conv_prompt_body(b_ref, c_ref, v_ref, w_ref, y_ref, tail_ref, carry_ref):
    @pl.when(pl.program_id(1) == 0)
    def _():
        carry_ref[...] = jnp.zeros_like(carry_ref)

    z = c_ref[...] * v_ref[...]
    tt = z.shape[0]
    zc = jnp.concatenate([carry_ref[...], z], axis=0)
    z1 = pltpu.roll(zc, 1, 0)[8:]
    z2 = pltpu.roll(zc, 2, 0)[8:]
    w = w_ref[...]
    y = w[0:1] * z2 + w[1:2] * z1 + w[2:3] * z
    y_ref[...] = (b_ref[...] * y).astype(y_ref.dtype)
    carry_ref[...] = z[tt - 8:]
    tail_ref[...] = z[tt - 8:]


def _conv_prompt(proj, conv_w, bsz, seq, dc, tt):
    nt = seq // tt
    spec = lambda col: pl.BlockSpec((tt, dc), lambda b, t: (b * nt + t, col))
    return pl.pallas_call(
        _conv_prompt_body,
        grid=(bsz, nt),
        in_specs=[spec(0), spec(1), spec(2),
                  pl.BlockSpec(conv_w.shape, lambda b, t: (0, 0))],
        out_specs=[pl.BlockSpec((tt, dc), lambda b, t: (b * nt + t, 0)),
                   pl.BlockSpec((None, 8, dc), lambda b, t: (b, 0, 0))],
        out_shape=[jax.ShapeDtypeStruct((bsz * seq, dc), BF16),
                   jax.ShapeDtypeStruct((bsz, 8, dc), F32)],
        scratch_shapes=[pltpu.VMEM((8, dc), F32)],
        compiler_params=_cparams(2),
        name="conv_prompt",
    )(proj, proj, proj, conv_w)


def _conv_sample_body(b_ref, c_ref, v_ref, e_ref, w_ref, y_ref, z_ref):
    z = c_ref[...] * v_ref[...]
    e = e_ref[...]
    rows = z.shape[0]
    tpos = lax.broadcasted_iota(I32, z.shape, 0) & 7
    z1 = jnp.where(tpos == 0, pltpu.roll(e, rows - 1, 0), pltpu.roll(z, 1, 0))
    z2 = jnp.where(tpos < 2, e, pltpu.roll(z, 2, 0))
    w = w_ref[...]
    y = w[0:1] * z2 + w[1:2] * z1 + w[2:3] * z
    y_ref[...] = (b_ref[...] * y).astype(y_ref.dtype)
    z_ref[...] = z


def _conv_sample(proj, e, conv_w, dc, tm):
    n = proj.shape[0]
    spec = lambda col: pl.BlockSpec((tm, dc), lambda i: (i, col))
    return pl.pallas_call(
        _conv_sample_body,
        grid=(n // tm,),
        in_specs=[spec(0), spec(1), spec(2), spec(0),
                  pl.BlockSpec(conv_w.shape, lambda i: (0, 0))],
        out_specs=[spec(0), spec(0)],
        out_shape=[jax.ShapeDtypeStruct((n, dc), BF16),
                   jax.ShapeDtypeStruct((n, dc), F32)],
        compiler_params=_cparams(1),
        name="conv_sample",
    )(proj, proj, proj, e, conv_w)


def _s5_operators(a_re, a_im, log_dt, b_re, b_im, c_re, c_im, d_skip, glu_w, glu_b):
    g, p, ch = b_re.shape
    t = S5_T
    dt = jnp.exp(log_dt)[:, None]
    mag = jnp.exp(a_re * dt)
    ar = mag * jnp.cos(a_im * dt)
    ai = mag * jnp.sin(a_im * dt)
    den = a_re * a_re + a_im * a_im
    qr = ((ar - 1.0) * a_re + ai * a_im) / den
    qi = (ai * a_re - (ar - 1.0) * a_im) / den
    bbr = qr[..., None] * b_re - qi[..., None] * b_im
    bbi = qr[..., None] * b_im + qi[..., None] * b_re
    pr, pi = [jnp.ones_like(ar)], [jnp.zeros_like(ar)]
    for _ in range(t):
        pr, pi = pr + [pr[-1] * ar - pi[-1] * ai], pi + [pr[-1] * ai + pi[-1] * ar]
    pw_r, pw_i = jnp.stack(pr), jnp.stack(pi)
    car = c_re[None] * pw_r[:, :, None, :] - c_im[None] * pw_i[:, :, None, :]
    cai = c_re[None] * pw_i[:, :, None, :] + c_im[None] * pw_r[:, :, None, :]
    kern = (jnp.einsum('jgop,gpi->jgoi', car[:t], bbr, precision=HI)
            - jnp.einsum('jgop,gpi->jgoi', cai[:t], bbi, precision=HI))
    lag = jnp.arange(t)[None, :] - jnp.arange(t)[:, None]
    toep = jnp.where((lag >= 0)[:, :, None, None, None], kern[jnp.clip(lag, 0, t - 1)], 0.0)
    m_op = toep.transpose(2, 0, 4, 1, 3).reshape(g, t * ch, t * ch)
    rev_r, rev_i = pw_r[t - 1::-1][:t], pw_i[t - 1::-1][:t]
    wsr = rev_r[..., None] * bbr[None] - rev_i[..., None] * bbi[None]
    wsi = rev_r[..., None] * bbi[None] + rev_i[..., None] * bbr[None]
    ws_op = jnp.concatenate([wsr.transpose(1, 0, 3, 2), wsi.transpose(1, 0, 3, 2)],
                            axis=-1).reshape(g, t * ch, 2 * p)
    wcr = car[1:].transpose(1, 3, 0, 2).reshape(g, p, t * ch)
    wci = cai[1:].transpose(1, 3, 0, 2).reshape(g, p, t * ch)
    wc_op = jnp.concatenate([wcr, -wci], axis=1)
    glu_op = jnp.einsum('st,gij->gsitj', jnp.eye(t, dtype=F32), glu_w).reshape(g, t * ch, t * ch)
    d_flat = jnp.tile(d_skip, (1, t))[:, None, :]
    gb_flat = jnp.tile(glu_b, (1, t))[:, None, :]

    def rot_tables(xr, xi):
        return jnp.concatenate([xr, xr], axis=-1), jnp.concatenate([-xi, xi], axis=-1)

    sr, si = pw_r[t], pw_i[t]
    p1s, p2s = [], []
    for _ in range(8):
        t1, t2 = rot_tables(sr, si)
        p1s.append(t1)
        p2s.append(t2)
        sr, si = sr * sr - si * si, 2.0 * sr * si
    h = t // 2
    p1_half, p2_half = rot_tables(pw_r[h], pw_i[h])
    return dict(
        m=m_op.astype(BF16), ws=ws_op.astype(BF16), wc=wc_op.astype(BF16), glu=glu_op.astype(BF16),
        d=d_flat, gb=gb_flat, p1=jnp.stack(p1s, axis=1), p2=jnp.stack(p2s, axis=1),
        p1_half=p1_half[:, None, :], p2_half=p2_half[:, None, :])


def _gelu_tanh(y):
    return 0.5 * y * (1.0 + jnp.tanh(GELU_C * (y + 0.044715 * (y * y * y))))


def _s5_tail(y, gl, gb):
    y = _gelu_tanh(y)
    gate = jnp.dot(y.astype(BF16), gl, preferred_element_type=F32) + gb
    return y * _sigmoid(gate)


S5_OCT = 8


def _fold_time(rows, q, blk):
    parts = []
    for h in range(len(rows) // S5_OCT):
        acc = None
        for tt in range(S5_OCT):
            r = rows[S5_OCT * h + tt]
            shift = (16 * (tt - q)) % 128
            if shift:
                r = pltpu.roll(r, shift, 1)
            acc = r if acc is None else jnp.where(blk == tt, r, acc)
        parts.append(acc)
    return parts[0] if len(parts) == 1 else jnp.concatenate(parts, axis=1)


def _unfold_time(ys, t, blk):
    h, tt = divmod(t, S5_OCT)
    acc = None
    for q in range(S5_OCT):
        r = ys[q][:, 128 * h:128 * (h + 1)]
        shift = (16 * (q - tt)) % 128
        if shift:
            r = pltpu.roll(r, shift, 1)
        acc = r if acc is None else jnp.where(blk == q, r, acc)
    return acc


def _s5_prompt_body(u_ref, m_ref, ws_ref, wc_ref, gl_ref, d_ref, gb_ref, p1_ref, p2_ref,
                    y_ref, hf_ref, uf, yf, *, bsz, nc):
    half = ws_ref.shape[-1] // 2
    seq = nc * S5_T
    blk = lax.broadcasted_iota(I32, (nc, 128), 1) // 16

    def fold(b, carry):
        rows = [u_ref[pl.ds(pl.multiple_of(b * seq, seq) + t, nc, stride=S5_T), :]
                for t in range(S5_T)]
        for q in range(S5_OCT):
            uf[q, pl.ds(pl.multiple_of(b * nc, nc), nc), :] = _fold_time(rows, q, blk)
        return carry

    lax.fori_loop(0, bsz, fold, 0)

    for g in range(S5_OCT):
        u = uf[g]
        ub = u.astype(BF16)
        yi = jnp.dot(ub, m_ref[g], preferred_element_type=F32)
        e = jnp.dot(ub, ws_ref[g], preferred_element_type=F32)
        cpos = lax.broadcasted_iota(I32, e.shape, 0) & (nc - 1)
        p1 = p1_ref[g]
        p2 = p2_ref[g]
        z = jnp.where(cpos >= 1, pltpu.roll(e, 1, 0), 0.0)
        d, k = 1, 0
        while d < nc:
            zs = jnp.where(cpos >= d, pltpu.roll(z, d, 0), 0.0)
            z = z + zs * p1[k:k + 1] + pltpu.roll(zs, half, 1) * p2[k:k + 1]
            d, k = d * 2, k + 1
        yc = jnp.dot(z.astype(BF16), wc_ref[g], preferred_element_type=F32)
        yf[g] = _s5_tail(yi + yc + d_ref[g] * u, gl_ref[g], gb_ref[g])
        hfin = e + z * p1[0:1] + pltpu.roll(z, half, 1) * p2[0:1]
        for b in range(bsz):
            r = b * nc + nc - 1
            hf_ref[g, b:b + 1, :] = hfin[r:r + 1, :]

    def unfold(b, carry):
        ys = [yf[q, pl.ds(pl.multiple_of(b * nc, nc), nc), :] for q in range(S5_OCT)]
        for t in range(S5_T):
            y_ref[pl.ds(pl.multiple_of(b * seq, seq) + t, nc, stride=S5_T), :] = (
                _unfold_time(ys, t, blk))
        return carry

    lax.fori_loop(0, bsz, unfold, 0)


def _s5_prompt(proj, col0, ops, bsz, nc):
    n = proj.shape[0]
    g = ops['m'].shape[0]
    w = ops['m'].shape[-1]
    p2x = ops['ws'].shape[-1]
    blk = lambda s1, s2: pl.BlockSpec((S5_OCT, s1, s2), lambda i: (i, 0, 0))
    return pl.pallas_call(
        functools.partial(_s5_prompt_body, bsz=bsz, nc=nc),
        grid=(g // S5_OCT,),
        in_specs=[pl.BlockSpec((n, 128), lambda i: (0, col0 + i)),
                  blk(w, w), blk(w, p2x), blk(p2x, w), blk(w, w),
                  blk(1, w), blk(1, w), blk(8, p2x), blk(8, p2x)],
        out_specs=[pl.BlockSpec((n, 128), lambda i: (0, i)), blk(bsz, p2x)],
        out_shape=[jax.ShapeDtypeStruct((n, g * 16), F32),
                   jax.ShapeDtypeStruct((g, bsz, p2x), F32)],
        scratch_shapes=[pltpu.VMEM((S5_OCT, bsz * nc, w), F32),
                        pltpu.VMEM((S5_OCT, bsz * nc, w), F32)],
        compiler_params=_cparams(1),
        name="s5_prompt",
    )(proj, ops['m'], ops['ws'], ops['wc'], ops['glu'], ops['d'], ops['gb'], ops['p1'], ops['p2'])


def _s5_sample_body(u_ref, h0_ref, m_ref, ws_ref, wc_ref, gl_ref, d_ref, gb_ref, p1_ref, p2_ref,
                    y_ref, hf_ref, *, steps):
    half = ws_ref.shape[-1] // 2
    nseq = u_ref.shape[0] // steps
    blk = lax.broadcasted_iota(I32, (nseq, 128), 1) // 16
    rows = [u_ref[pl.ds(t, nseq, stride=steps), :] for t in range(steps)]
    ys = []
    for g in range(S5_OCT):
        u = _fold_time(rows, g, blk)
        h0 = h0_ref[g]
        ub = u.astype(BF16)
        yi = jnp.dot(ub, m_ref[g], preferred_element_type=F32)
        e = jnp.dot(ub, ws_ref[g], preferred_element_type=F32)
        yc = jnp.dot(h0.astype(BF16), wc_ref[g], preferred_element_type=F32)
        ys.append(_s5_tail(yi + yc + d_ref[g] * u, gl_ref[g], gb_ref[g]))
        hf_ref[g] = e + h0 * p1_ref[g] + pltpu.roll(h0, half, 1) * p2_ref[g]
    for t in range(steps):
        y_ref[pl.ds(t, nseq, stride=steps), :] = _unfold_time(ys, t, blk)


def _s5_sample(proj, col0, h0, ops, steps):
    n = proj.shape[0]
    g, r, p2x = h0.shape
    w = ops['m'].shape[-1]
    blk = lambda s1, s2: pl.BlockSpec((S5_OCT, s1, s2), lambda i: (i, 0, 0))
    return pl.pallas_call(
        functools.partial(_s5_sample_body, steps=steps),
        grid=(g // S5_OCT,),
        in_specs=[pl.BlockSpec((n, 128), lambda i: (0, col0 + i)),
                  blk(r, p2x), blk(w, w), blk(w, p2x), blk(p2x, w), blk(w, w),
                  blk(1, w), blk(1, w), blk(1, p2x), blk(1, p2x)],
        out_specs=[pl.BlockSpec((n, 128), lambda i: (0, i)), blk(r, p2x)],
        out_shape=[jax.ShapeDtypeStruct((n, g * 16), F32),
                   jax.ShapeDtypeStruct((g, r, p2x), F32)],
        compiler_params=_cparams(1),
        name="s5_sample",
    )(proj, h0, ops['m'], ops['ws'], ops['wc'], ops['glu'], ops['d'], ops['gb'],
      ops['p1'], ops['p2'])


def _outproj_body(cnt0_ref, x_ref, yc_ref, ys_ref, wo_ref, gt_ref, g_ref, sc_ref, sh_ref,
                  wr_ref, br_ref, *rest, has_prev):
    x1_ref, h2_ref, ti_ref, gate_ref, rank_ref, cnt_ref, run_ref = rest[1:] if has_prev else rest

    @pl.when(pl.program_id(0) == 0)
    def _():
        run_ref[...] = cnt0_ref[...]

    dc = yc_ref.shape[1]
    mix = (jnp.dot(yc_ref[...], wo_ref[0:dc, :], preferred_element_type=F32)
           + jnp.dot(ys_ref[...].astype(BF16), wo_ref[dc:, :], preferred_element_type=F32))
    x1 = x_ref[...] + gt_ref[...] * mix
    x1_ref[...] = x1
    ms = jnp.mean(x1 * x1, axis=-1, keepdims=True)
    h = x1 * lax.rsqrt(ms + EPS) * g_ref[...]
    hb = (h * (1.0 + sc_ref[...]) + sh_ref[...]).astype(BF16)
    hb32 = hb.astype(F32)
    half = h2_ref.shape[1]
    h2_ref[...] = _pack_bf16_pair(hb32[:, :half], hb32[:, half:])
    logits =jnp.dot(hb, wr_ref[...], preferred_element_type=F32) + br_ref[...]
    tm, ne = logits.shape
    lane = lax.broadcasted_iota(I32, logits.shape, 1).astype(F32)
    work = logits
    vals, ids, sels = [], [], []
    for _ in range(TOP_K):
        m = jnp.max(work, axis=1, keepdims=True)
        idx = jnp.min(jnp.where(work == m, lane, float(ne)), axis=1, keepdims=True)
        sel = lane == idx
        vals.append(m)
        ids.append(idx)
        sels.append(sel)
        work = jnp.where(sel, -jnp.inf, work)
    exps = [jnp.exp(v - vals[0]) for v in vals]
    tot = exps[0]
    for ex in exps[1:]:
        tot = tot + ex
    gates = [ex / tot for ex in exps]
    onehot = sels[0]
    for s in sels[1:]:
        onehot = onehot | s
    onehot = onehot.astype(F32)
    row = lax.broadcasted_iota(I32, (tm, tm), 0)
    col = lax.broadcasted_iota(I32, (tm, tm), 1)
    below = (col < row).astype(BF16)
    before = jnp.dot(below, onehot.astype(BF16), preferred_element_type=F32) + run_ref[...]
    ranks = [jnp.sum(jnp.where(s, before, 0.0), axis=1, keepdims=True) for s in sels]
    run_ref[...] = run_ref[...] + jnp.sum(onehot, axis=0, keepdims=True)
    cnt_ref[...] = run_ref[...]

    wide = lax.broadcasted_iota(I32, ti_ref.shape, 1)

    def spread(cols):
        out = cols[TOP_K - 1]
        for k in range(TOP_K - 2, -1, -1):
            out = jnp.where(wide == k, cols[k], out)
        return out

    ti_ref[...] = spread(ids).astype(I32)
    gate_ref[...] = spread(gates)
    rank_ref[...] = spread(ranks).astype(I32)


def _outproj(cnt0, x, yc, ys, wo_bf, mod, per_row, rows_per_batch, g, wr_bf, br, h2_prev, n_all,
             row0, tm):
    n, d = x.shape
    dc = yc.shape[1]
    ne = wr_bf.shape[1]
    blk0 = row0 // tm
    row = lambda w: pl.BlockSpec((tm, w), lambda i: (i, 0))
    const = lambda s: pl.BlockSpec(s, lambda i: (0, 0))
    has_prev = h2_prev is not None
    in_specs = [const((1, ne)), row(d), row(dc), row(dc),
                _resident((d, d), lambda i: (0, 0)),
                _mod_spec(per_row, tm, rows_per_batch, d, 2), const((1, d)),
                _mod_spec(per_row, tm, rows_per_batch, d, 4),
                _mod_spec(per_row, tm, rows_per_batch, d, 3),
                const((d, ne)), const((1, ne))]
    args = [cnt0, x, yc, ys, wo_bf, mod, g.reshape(1, d), mod, mod, wr_bf, br.reshape(1, ne)]
    if has_prev:
        in_specs.append(pl.BlockSpec(memory_space=pl.ANY))
        args.append(h2_prev)
    return pl.pallas_call(
        functools.partial(_outproj_body, has_prev=has_prev),
        grid=(n // tm,),
        in_specs=in_specs,
        out_specs=[row(d), pl.BlockSpec((tm, d // 2), lambda i: (blk0 + i, 0)),
                   row(128), row(128), row(128), const((1, ne))],
        out_shape=[jax.ShapeDtypeStruct((n, d), F32),
                   jax.ShapeDtypeStruct((n_all, d // 2), jnp.uint32),
                   jax.ShapeDtypeStruct((n, 128), I32), jax.ShapeDtypeStruct((n, 128), F32),
                   jax.ShapeDtypeStruct((n, 128), I32), jax.ShapeDtypeStruct((1, ne), F32)],
        scratch_shapes=[pltpu.VMEM((1, ne), F32)],
        input_output_aliases={len(args) - 1: 1} if has_prev else {},
        compiler_params=_cparams(1),
        name="outproj_router",
    )(*args)


def _moe_body(be_ref, rb_ref, nch_ref, nrow_ref, tok0_ref, tokn_ref, h2_hbm,
              w1g_ref, w1l_ref, b1g_ref, b1l_ref, w2_ref, b2_ref,
              o_ref, xp, xb, act, sem, *, nf1):
    del be_ref, rb_ref
    b = pl.program_id(0)
    s = pl.program_id(1)
    n_chunks = nch_ref[b]

    def issue_rows(tok_ref, n_rows):
        def body(i, carry):
            r0 = pl.multiple_of(i * 8, 8)
            for u in range(8):
                pltpu.make_async_copy(h2_hbm.at[pl.ds(tok_ref[0, r0 + u], 1), :],
                                      xp.at[pl.ds(r0 + u, 1), :], sem.at[0]).start()
            return carry

        lax.fori_loop(0, n_rows // 8, body, 0)

    def wait_rows(n_rows):
        size = 8
        while size <= MOE_TB:
            @pl.when((n_rows & size) != 0)
            def _(size=size):
                pltpu.make_async_copy(h2_hbm.at[pl.ds(0, size), :], xp.at[pl.ds(0, size), :],
                                      sem.at[0]).wait()

            size *= 2

    @pl.when((b == 0) & (s == 0))
    def _():
        xp[...] = jnp.zeros_like(xp)
        issue_rows(tok0_ref, nrow_ref[0])

    @pl.when(s == 0)
    def _():
        wait_rows(nrow_ref[b])
        half = xp.shape[1]

        def unpack(i, carry):
            r0 = pl.multiple_of(i * MOE_CH, MOE_CH)
            xb[pl.ds(r0, MOE_CH), :half], xb[pl.ds(r0, MOE_CH), half:] = _unpack_bf16_pair(
                xp[pl.ds(r0, MOE_CH), :])
            return carry

        lax.fori_loop(0, n_chunks, unpack, 0)
        issue_rows(tokn_ref, nrow_ref[b + 1])

    def chunk_groups(one):
        assert MOE_TB // MOE_CH < 8
        for size in (4, 2, 1):
            @pl.when((n_chunks & size) != 0)
            def _(size=size):
                r0 = (n_chunks & ~(2 * size - 1)) * MOE_CH
                one(pl.multiple_of(r0, size * MOE_CH), size * MOE_CH)

    @pl.when((s < nf1) & (n_chunks > 0))
    def _():
        def one(r0, rows):
            x = xb[pl.ds(r0, rows), :]
            hg = jnp.dot(x, w1g_ref[...].astype(BF16), preferred_element_type=F32)
            hl = jnp.dot(x, w1l_ref[...].astype(BF16), preferred_element_type=F32)
            gl = jnp.minimum(hg + b1g_ref[...], SWIGLU_LIMIT)
            ln = jnp.clip(hl + b1l_ref[...], -SWIGLU_LIMIT, SWIGLU_LIMIT)
            a = gl * _sigmoid(SWIGLU_ALPHA * gl) * (ln + 1.0)
            act[s, pl.ds(r0, rows), :] = a.astype(BF16)

        chunk_groups(one)

    @pl.when((s >= nf1) & (n_chunks > 0))
    def _():
        def one(r0, rows):
            a = jnp.concatenate([act[k, pl.ds(r0, rows), :] for k in range(nf1)], axis=1)
            o_ref[pl.ds(r0, rows), :] = (
                jnp.dot(a, w2_ref[...].astype(BF16), preferred_element_type=F32) + b2_ref[...])

        chunk_groups(one)

        def zero(i, carry):
            r0 = pl.multiple_of(i * MOE_CH, MOE_CH)
            o_ref[pl.ds(r0, MOE_CH), :] = jnp.zeros((MOE_CH, o_ref.shape[1]), F32)
            return carry

        lax.fori_loop(n_chunks, MOE_TB // MOE_CH, zero, 0)


def _moe_experts(h2_all, slot_tok, w1, b1, w2, b2, blk_expert, blk_rows, blk_chunks, blk_nrow,
                 n_grid_blocks):
    ne, d, f2 = w1.shape
    f = f2 // 2
    nf1 = f // MOE_TF
    nf2 = d // MOE_TN
    nb = slot_tok.shape[0]

    def s1(b, s, nch):
        return jnp.where(nch[b] > 0, jnp.minimum(s, nf1 - 1), nf1 - 1)

    def s2(b, s, nch):
        return jnp.where(nch[b] > 0, jnp.maximum(s - nf1, 0), nf2 - 1)

    grid_spec = pltpu.PrefetchScalarGridSpec(
        num_scalar_prefetch=4,
        grid=(n_grid_blocks, nf1 + nf2),
        in_specs=[
            pl.BlockSpec((None, 1, MOE_TB), lambda b, s, be, rb, nch, nr: (0, 0, 0),
                         memory_space=pltpu.SMEM),
            pl.BlockSpec((None, 1, MOE_TB),
                         lambda b, s, be, rb, nch, nr: (jnp.minimum(b + 1, nb - 1), 0, 0),
                         memory_space=pltpu.SMEM),
            pl.BlockSpec(memory_space=pl.ANY),
            pl.BlockSpec((None, d, MOE_TF), lambda b, s, be, rb, nch, nr: (be[b], 0, s1(b, s, nch))),
            pl.BlockSpec((None, d, MOE_TF),
                         lambda b, s, be, rb, nch, nr: (be[b], 0, nf1 + s1(b, s, nch))),
            pl.BlockSpec((None, 1, MOE_TF), lambda b, s, be, rb, nch, nr: (be[b], 0, s1(b, s, nch))),
            pl.BlockSpec((None, 1, MOE_TF),
                         lambda b, s, be, rb, nch, nr: (be[b], 0, nf1 + s1(b, s, nch))),
            pl.BlockSpec((None, f, MOE_TN), lambda b, s, be, rb, nch, nr: (be[b], 0, s2(b, s, nch))),
            pl.BlockSpec((None, 1, MOE_TN), lambda b, s, be, rb, nch, nr: (be[b], 0, s2(b, s, nch))),
        ],
        out_specs=pl.BlockSpec((MOE_TB, MOE_TN),
                               lambda b, s, be, rb, nch, nr: (rb[b], s2(b, s, nch))),
        scratch_shapes=[pltpu.VMEM((MOE_TB, d // 2), jnp.uint32), pltpu.VMEM((MOE_TB, d), BF16),
                        pltpu.VMEM((nf1, MOE_TB, MOE_TF), BF16),
                        pltpu.SemaphoreType.DMA((1,))],
    )
    return pl.pallas_call(
        functools.partial(_moe_body, nf1=nf1),
        grid_spec=grid_spec,
        out_shape=jax.ShapeDtypeStruct((nb * MOE_TB, d), F32),
        compiler_params=_cparams(2),
        name="moe_experts",
    )(blk_expert, blk_rows, blk_chunks, blk_nrow, slot_tok, slot_tok, h2_all,
      w1, w1, b1.reshape(ne, 1, f2), b1.reshape(ne, 1, f2), w2, b2.reshape(ne, 1, d))


def _moe_plan(counts, top_i, rank, n_tokens, n_blocks):
    ne = counts.shape[0]
    nblk = (counts + MOE_TB - 1) // MOE_TB
    blk_end = jnp.cumsum(nblk)
    blk_start = blk_end - nblk
    n_used = blk_end[-1]
    dest = blk_start[top_i] * MOE_TB + rank
    bidx = jnp.arange(n_blocks + 1, dtype=I32)
    be = jnp.minimum(jnp.searchsorted(blk_end, bidx, side='right'), ne - 1).astype(I32)
    active = bidx < n_used
    valid = jnp.where(active, jnp.clip(counts[be] - (bidx - blk_start[be]) * MOE_TB, 0, MOE_TB), 0)
    chunks = ((valid + MOE_CH - 1) // MOE_CH).astype(I32)
    nrow = ((valid + 7) // 8 * 8).astype(I32)
    last = jnp.maximum(n_used - 1, 0)
    be = jnp.where(active, be, be[last]).astype(I32)
    rows = jnp.where(active, bidx, last).astype(I32)
    tok = jnp.arange(n_tokens * TOP_K, dtype=I32) // TOP_K
    slot_tok = jnp.zeros((n_blocks * MOE_TB,), I32).at[dest.reshape(-1)].set(tok)
    return dest, slot_tok.reshape(n_blocks, 1, MOE_TB), be, rows, chunks, nrow, n_used


def _combine_body(d0_ref, dn_ref, x1_ref, gate_ref, gt_ref, gf_ref, ys_hbm, o_ref, ybuf, sem):
    i = pl.program_id(0)
    tm = x1_ref.shape[0]
    slot = lax.rem(i, 2)

    def issue_rows(dest_ref, sl):
        def body(i, carry):
            r0 = pl.multiple_of(i * 8, 8)
            for u in range(8):
                for k in range(TOP_K):
                    pltpu.make_async_copy(
                        ys_hbm.at[pl.ds(dest_ref[0, (r0 + u) * TOP_K + k], 1), :],
                        ybuf.at[sl, k, pl.ds(r0 + u, 1), :], sem.at[sl]).start()
            return carry

        lax.fori_loop(0, tm // 8, body, 0)

    @pl.when(i == 0)
    def _():
        issue_rows(d0_ref, 0)

    @pl.when(i + 1 < pl.num_programs(0))
    def _():
        issue_rows(dn_ref, 1 - slot)

    pltpu.make_async_copy(ybuf.at[slot], ybuf.at[slot], sem.at[slot]).wait()
    gates = gate_ref[...]
    y = gates[:, 0:1] * ybuf[slot, 0]
    for k in range(1, TOP_K):
        y = y + gates[:, k:k + 1] * ybuf[slot, k]
    x2 = x1_ref[...] + gt_ref[...] * y
    ms = jnp.mean(x2 * x2, axis=-1, keepdims=True)
    o_ref[...] = x2 * lax.rsqrt(ms + EPS) * gf_ref[...]


def _combine(x1, dest, ys, gates, mod, per_row, rows_per_batch, g_final, tm):
    n, d = x1.shape
    nt = n // tm
    dest3 = dest.reshape(nt, 1, tm * TOP_K)
    return pl.pallas_call(
        _combine_body,
        grid=(nt,),
        in_specs=[pl.BlockSpec((None, 1, tm * TOP_K), lambda i: (0, 0, 0), memory_space=pltpu.SMEM),
                  pl.BlockSpec((None, 1, tm * TOP_K), lambda i: (jnp.minimum(i + 1, nt - 1), 0, 0),
                               memory_space=pltpu.SMEM),
                  pl.BlockSpec((tm, d), lambda i: (i, 0)),
                  pl.BlockSpec((tm, 128), lambda i: (i, 0)),
                  _mod_spec(per_row, tm, rows_per_batch, d, 5),
                  pl.BlockSpec((1, d), lambda i: (0, 0)),
                  pl.BlockSpec(memory_space=pl.ANY)],
        out_specs=pl.BlockSpec((tm, d), lambda i: (i, 0)),
        out_shape=jax.ShapeDtypeStruct((n, d), F32),
        scratch_shapes=[pltpu.VMEM((2, TOP_K, tm, d), F32), pltpu.SemaphoreType.DMA((2,))],
        compiler_params=_cparams(1),
        name="combine_norm",
    )(dest3, dest3, x1, gates, mod, g_final.reshape(1, d), ys)


def kernel(x_prompt, x_sample, c_prompt, c_sample, state_conv, state_ssm_re, state_ssm_im, g_mix, g_ffn, w_mod, b_mod, w_in, conv_w, ssm_a_re, ssm_a_im, ssm_log_dt, ssm_b_re, ssm_b_im, ssm_c_re, ssm_c_im, ssm_d, glu_w, glu_b, w_out, w_router, b_router, w1, b1, w2, b2, g_final):
    bp, lp, d = x_prompt.shape
    bs, ls, _ = x_sample.shape
    depth = g_mix.shape[0]
    dc = conv_w.shape[-1]
    n_grp, p_st, ch = ssm_b_re.shape[1:]
    ne = w_router.shape[-1]
    np_, ns_ = bp * lp, bs * ls
    n_tok = np_ + ns_
    assert ls == S5_T // 2 and lp % S5_T == 0 and conv_w.shape[1] == 3
    nc = lp // S5_T
    n_blocks = -(-(n_tok * TOP_K) // MOE_TB) + ne

    xp = x_prompt.reshape(np_, d)
    xs = x_sample.reshape(ns_, d)
    c_all = jnp.concatenate([c_prompt, c_sample], axis=0)
    pad = (-c_all.shape[0]) % 8
    c_all = jnp.pad(c_all, ((0, pad), (0, 0)))

    assert depth == 1
    outs = [[] for _ in range(6)]
    for l in range(depth):
        m = _modulation(c_all, w_mod[l], b_mod[l])
        mod_p = m[:bp].reshape(bp, 1, N_MOD * d)
        mod_s = jnp.repeat(m[bp:bp + bs], ls, axis=0)

        w_in_bf = w_in[l].astype(BF16)
        proj_p = _inproj(xp, mod_p, False, lp, g_mix[l], w_in_bf, 512)
        proj_s = _inproj(xs, mod_s, True, ls, g_mix[l], w_in_bf, 256)

        yc_p, tail_p = _conv_prompt(proj_p, conv_w[l], bp, lp, dc, 512)
        e_s = jnp.pad(state_conv[l], ((0, 0), (0, ls - 2), (0, 0))).reshape(ns_, dc)
        yc_s, z_s = _conv_sample(proj_s, e_s, conv_w[l], dc, 256)
        new_conv_p = tail_p[:, 6:8, :]
        new_conv_s = z_s.reshape(bs, ls, dc)[:, ls - 2:, :]

        ops = _s5_operators(ssm_a_re[l], ssm_a_im[l], ssm_log_dt[l], ssm_b_re[l], ssm_b_im[l],
                            ssm_c_re[l], ssm_c_im[l], ssm_d[l], glu_w[l], glu_b[l])
        w8 = ls * ch
        ucol0 = (3 * dc) // 128
        ys_p, hf_p = _s5_prompt(proj_p, ucol0, ops, bp, nc)
        new_re_p = hf_p[:, :, :p_st].transpose(1, 0, 2)
        new_im_p = hf_p[:, :, p_st:].transpose(1, 0, 2)

        h0 = jnp.concatenate([state_ssm_re[l], state_ssm_im[l]], axis=-1).transpose(1, 0, 2)
        ops_s = dict(m=ops['m'][:, :w8, :w8], ws=ops['ws'][:, w8:, :], wc=ops['wc'][:, :, :w8],
                     glu=ops['glu'][:, :w8, :w8], d=ops['d'][:, :, :w8], gb=ops['gb'][:, :, :w8],
                     p1=ops['p1_half'], p2=ops['p2_half'])
        ys_s, hf_s = _s5_sample(proj_s, ucol0, h0, ops_s, ls)
        new_re_s = hf_s[:, :, :p_st].transpose(1, 0, 2)
        new_im_s = hf_s[:, :, p_st:].transpose(1, 0, 2)

        wo_bf = w_out[l].astype(BF16)
        wr_bf = w_router[l].astype(BF16)
        cnt0 = jnp.zeros((1, ne), F32)
        x1_p, h2_all, ti_p, gate_p, rank_p, cnt1 = _outproj(
            cnt0, xp, yc_p, ys_p, wo_bf, mod_p, False, lp, g_ffn[l], wr_bf, b_router[l],
            None, n_tok, 0, 256)
        x1_s, h2_all, ti_s, gate_s, rank_s, cnt2 = _outproj(
            cnt1, xs, yc_s, ys_s, wo_bf, mod_s, True, ls, g_ffn[l], wr_bf, b_router[l],
            h2_all, n_tok, np_, 256)

        counts = cnt2[0].astype(I32)
        top_i = jnp.concatenate([ti_p[:, :TOP_K], ti_s[:, :TOP_K]], axis=0)
        rank = jnp.concatenate([rank_p[:, :TOP_K], rank_s[:, :TOP_K]], axis=0)
        dest, slot_tok, blk_e, blk_rows, blk_chunks, blk_nrow, n_used = _moe_plan(
            counts, top_i, rank, n_tok, n_blocks)
        y_sorted = _moe_experts(h2_all, slot_tok, w1[l], b1[l], w2[l], b2[l],
                                blk_e, blk_rows, blk_chunks, blk_nrow, n_used)

        xp = _combine(x1_p, dest[:np_], y_sorted, gate_p, mod_p, False, lp, g_final, 128)
        xs = _combine(x1_s, dest[np_:], y_sorted, gate_s, mod_s, True, ls, g_final, 128)
        for lst, val in zip(outs, (new_conv_p, new_re_p, new_im_p, new_conv_s, new_re_s, new_im_s)):
            lst.append(val)

    y_prompt = xp.reshape(bp, lp, d)
    y_sample = xs.reshape(bs, ls, d)
    return (y_prompt, y_sample) + tuple(jnp.stack(o) for o in outs)
```

```python
import functools
import math

import jax
import jax.numpy as jnp
from jax import lax
from jax.experimental import pallas as pl
from jax.experimental.pallas import tpu as pltpu

F32 = jnp.float32
BF16 = jnp.bfloat16
I32 = jnp.int32

LANES = 128
EPS = 1e-6
N_MOD = 6
TOP_K = 4
SWIGLU_LIMIT = 7.0
SWIGLU_ALPHA = 1.702
GELU_C = math.sqrt(2.0 / math.pi)

S5_T = 16
MOE_TB = 1536
MOE_CH = 256
MOE_TF = 512
MOE_TN = 512
VMEM_LIMIT = 56 * 1024 * 1024
HI = lax.Precision.HIGHEST


def _cparams(n_axes):
    return pltpu.CompilerParams(dimension_semantics=("arbitrary",) * n_axes,
                                vmem_limit_bytes=VMEM_LIMIT)


def _sigmoid(x):
    return 1.0 / (1.0 + jnp.exp(-x))


def _pack_bf16_pair(lo, hi):
    lo_bits = lax.bitcast_convert_type(lo, jnp.uint32) >> 16
    hi_bits = lax.bitcast_convert_type(hi, jnp.uint32) & jnp.uint32(0xFFFF0000)
    return lo_bits | hi_bits


def _unpack_bf16_pair(packed):
    lo = lax.bitcast_convert_type(packed << 16, F32)
    hi = lax.bitcast_convert_type(packed & jnp.uint32(0xFFFF0000), F32)
    return lo.astype(BF16), hi.astype(BF16)


def _resident(shape, index_map):
    return pl.BlockSpec(shape, index_map, pipeline_mode=pl.Buffered(1))


def _mod_body(c_ref, w_ref, b_ref, o_ref):
    c = c_ref[...]
    s = c * _sigmoid(c)
    o_ref[...] = jnp.dot(s.astype(BF16), w_ref[...].astype(BF16),
                         preferred_element_type=F32) + b_ref[...]


def _modulation(c_all, w_mod, b_mod):
    rows, d = c_all.shape
    n = w_mod.shape[1]
    tn = 1024
    return pl.pallas_call(
        _mod_body,
        grid=(n // tn,),
        in_specs=[pl.BlockSpec((rows, d), lambda j: (0, 0)),
                  pl.BlockSpec((d, tn), lambda j: (0, j)),
                  pl.BlockSpec((1, tn), lambda j: (0, j))],
        out_specs=pl.BlockSpec((rows, tn), lambda j: (0, j)),
        out_shape=jax.ShapeDtypeStruct((rows, n), F32),
        compiler_params=_cparams(1),
        name="modulation",
    )(c_all, w_mod, b_mod.reshape(1, n))


def _mod_spec(per_row, tm, rows_per_batch, d, col):
    if per_row:
        return pl.BlockSpec((tm, d), lambda i: (i, col))
    return pl.BlockSpec((None, 1, d), lambda i: ((i * tm) // rows_per_batch, 0, col))


def _inproj_body(x_ref, g_ref, sc_ref, sh_ref, w_ref, o_ref):
    x = x_ref[...]
    ms = jnp.mean(x * x, axis=-1, keepdims=True)
    h = x * lax.rsqrt(ms + EPS) * g_ref[...]
    h = h * (1.0 + sc_ref[...]) + sh_ref[...]
    o_ref[...] = jnp.dot(h.astype(BF16), w_ref[...], preferred_element_type=F32)


def _inproj(x, mod, per_row, rows_per_batch, g, w_bf, tm):
    n, d = x.shape
    dn = w_bf.shape[1]
    return pl.pallas_call(
        _inproj_body,
        grid=(n // tm,),
        in_specs=[pl.BlockSpec((tm, d), lambda i: (i, 0)),
                  pl.BlockSpec((1, d), lambda i: (0, 0)),
                  _mod_spec(per_row, tm, rows_per_batch, d, 1),
                  _mod_spec(per_row, tm, rows_per_batch, d, 0),
                  _resident((d, dn), lambda i: (0, 0))],
        out_specs=pl.BlockSpec((tm, dn), lambda i: (i, 0)),
        out_shape=jax.ShapeDtypeStruct((n, dn), F32),
        compiler_params=_cparams(1),
        name="inproj",
    )(x, g.reshape(1, d), mod, mod, w_bf)


def _conv_prompt_body(b_ref, c_ref, v_ref, w_ref, y_ref, tail_ref, carry_ref):
    @pl.when(pl.program_id(1) == 0)
    def _():
        carry_ref[...] = jnp.zeros_like(carry_ref)

    z = c_ref[...] * v_ref[...]
    tt = z.shape[0]
    zc = jnp.concatenate([carry_ref[...], z], axis=0)
    z1 = pltpu.roll(zc, 1, 0)[8:]
    z2 = pltpu.roll(zc, 2, 0)[8:]
    w = w_ref[...]
    y = w[0:1] * z2 + w[1:2] * z1 + w[2:3] * z
    y_ref[...] = (b_ref[...] * y).astype(y_ref.dtype)
    carry_ref[...] = z[tt - 8:]
    tail_ref[...] = z[tt - 8:]


def _conv_prompt(proj, conv_w, bsz, seq, dc, tt):
    nt = seq // tt
    spec = lambda col: pl.BlockSpec((tt, dc), lambda b, t: (b * nt + t, col))
    return pl.pallas_call(
        _conv_prompt_body,
        grid=(bsz, nt),
        in_specs=[spec(0), spec(1), spec(2),
                  pl.BlockSpec(conv_w.shape, lambda b, t: (0, 0))],
        out_specs=[pl.BlockSpec((tt, dc), lambda b, t: (b * nt + t, 0)),
                   pl.BlockSpec((None, 8, dc), lambda b, t: (b, 0, 0))],
        out_shape=[jax.ShapeDtypeStruct((bsz * seq, dc), BF16),
                   jax.ShapeDtypeStruct((bsz, 8, dc), F32)],
        scratch_shapes=[pltpu.VMEM((8, dc), F32)],
        compiler_params=_cparams(2),
        name="conv_prompt",
    )(proj, proj, proj, conv_w)


def _conv_sample_body(b_ref, c_ref, v_ref, e_ref, w_ref, y_ref, z_ref):
    z = c_ref[...] * v_ref[...]
    e = e_ref[...]
    rows = z.shape[0]
    tpos = lax.broadcasted_iota(I32, z.shape, 0) & 7
    z1 = jnp.where(tpos == 0, pltpu.roll(e, rows - 1, 0), pltpu.roll(z, 1, 0))
    z2 = jnp.where(tpos < 2, e, pltpu.roll(z, 2, 0))
    w = w_ref[...]
    y = w[0:1] * z2 + w[1:2] * z1 + w[2:3] * z
    y_ref[...] = (b_ref[...] * y).astype(y_ref.dtype)
    z_ref[...] = z


def _conv_sample(proj, e, conv_w, dc, tm):
    n = proj.shape[0]
    spec = lambda col: pl.BlockSpec((tm, dc), lambda i: (i, col))
    return pl.pallas_call(
        _conv_sample_body,
        grid=(n // tm,),
        in_specs=[spec(0), spec(1), spec(2), spec(0),
                  pl.BlockSpec(conv_w.shape, lambda i: (0, 0))],
        out_specs=[spec(0), spec(0)],
        out_shape=[jax.ShapeDtypeStruct((n, dc), BF16),
                   jax.ShapeDtypeStruct((n, dc), F32)],
        compiler_params=_cparams(1),
        name="conv_sample",
    )(proj, proj, proj, e, conv_w)


def _s5_operators(a_re, a_im, log_dt, b_re, b_im, c_re, c_im, d_skip, glu_w, glu_b):
    g, p, ch = b_re.shape
    t = S5_T
    dt = jnp.exp(log_dt)[:, None]
    mag = jnp.exp(a_re * dt)
    ar = mag * jnp.cos(a_im * dt)
    ai = mag * jnp.sin(a_im * dt)
    den = a_re * a_re + a_im * a_im
    qr = ((ar - 1.0) * a_re + ai * a_im) / den
    qi = (ai * a_re - (ar - 1.0) * a_im) / den
    bbr = qr[..., None] * b_re - qi[..., None] * b_im
    bbi = qr[..., None] * b_im + qi[..., None] * b_re
    pr, pi = [jnp.ones_like(ar)], [jnp.zeros_like(ar)]
    for _ in range(t):
        pr, pi = pr + [pr[-1] * ar - pi[-1] * ai], pi + [pr[-1] * ai + pi[-1] * ar]
    pw_r, pw_i = jnp.stack(pr), jnp.stack(pi)
    car = c_re[None] * pw_r[:, :, None, :] - c_im[None] * pw_i[:, :, None, :]
    cai = c_re[None] * pw_i[:, :, None, :] + c_im[None] * pw_r[:, :, None, :]
    rev_r, rev_i = pw_r[t - 1::-1][:t], pw_i[t - 1::-1][:t]
    wsr = rev_r[..., None] * bbr[None] - rev_i[..., None] * bbi[None]
    wsi = rev_r[..., None] * bbi[None] + rev_i[..., None] * bbr[None]
    ws_op = jnp.concatenate([wsr.transpose(1, 0, 3, 2), wsi.transpose(1, 0, 3, 2)],
                            axis=-1).reshape(g, t * ch, 2 * p)
    wcr = car[1:].transpose(1, 3, 0, 2).reshape(g, p, t * ch)
    wci = cai[1:].transpose(1, 3, 0, 2).reshape(g, p, t * ch)
    wc_op = jnp.concatenate([wcr, -wci], axis=1)
    cmat_t = jnp.concatenate([c_re, -c_im], axis=-1)
    glu_t = jnp.swapaxes(glu_w, 1, 2)
    d_flat = jnp.tile(d_skip, (1, t))[:, None, :]
    gb_flat = jnp.tile(glu_b, (1, t))[:, None, :]

    def rot_tables(xr, xi):
        return jnp.concatenate([xr, xr], axis=-1), jnp.concatenate([-xi, xi], axis=-1)

    sr, si = pw_r[t], pw_i[t]
    p1s, p2s = [], []
    for _ in range(8):
        t1, t2 = rot_tables(sr, si)
        p1s.append(t1)
        p2s.append(t2)
        sr, si = sr * sr - si * si, 2.0 * sr * si
    h = t // 2
    p1_half, p2_half = rot_tables(pw_r[h], pw_i[h])
    return dict(
        ws=ws_op, wc=wc_op.astype(BF16), cmat=cmat_t, glu=glu_t,
        d=d_flat, gb=gb_flat, p1=jnp.stack(p1s, axis=1), p2=jnp.stack(p2s, axis=1),
        p1_half=p1_half[:, None, :], p2_half=p2_half[:, None, :])


def _gelu_tanh(y):
    return 0.5 * y * (1.0 + jnp.tanh(GELU_C * (y + 0.044715 * (y * y * y))))


def _dot_t(x, w_t):
    return lax.dot_general(x, w_t, (((1,), (1,)), ((), ())), preferred_element_type=F32)


def _block_toeplitz_t(seq):
    ch, n = seq.shape
    lane = lax.broadcasted_iota(I32, seq.shape, 1)
    rows = []
    for t in range(n // ch):
        shift = (ch * (t + 1)) % n
        r = pltpu.roll(seq, shift, 1) if shift else seq
        rows.append(jnp.where(lane < ch * (t + 1), r, 0.0))
    return jnp.concatenate(rows, axis=0).astype(BF16)


def _s5_group_operators(ws, cmat, glu_t):
    n = ws.shape[0]
    seq = lax.dot_general(cmat, ws, (((1,), (1,)), ((), ())), precision=HI,
                          preferred_element_type=F32)
    ch = cmat.shape[0]
    glu_seq = jnp.concatenate([jnp.zeros((ch, n - ch), F32), glu_t], axis=1)
    return _block_toeplitz_t(seq), _block_toeplitz_t(glu_seq)


def _s5_tail(y, gl_t, gb):
    y = _gelu_tanh(y)
    gate = _dot_t(y.astype(BF16), gl_t) + gb
    return y * _sigmoid(gate)


S5_OCT = 8


def _fold_time(rows, q, blk):
    parts = []
    for h in range(len(rows) // S5_OCT):
        acc = None
        for tt in range(S5_OCT):
            r = rows[S5_OCT * h + tt]
            shift = (16 * (tt - q)) % 128
            if shift:
                r = pltpu.roll(r, shift, 1)
            acc = r if acc is None else jnp.where(blk == tt, r, acc)
        parts.append(acc)
    return parts[0] if len(parts) == 1 else jnp.concatenate(parts, axis=1)


def _unfold_time(ys, t, blk):
    h, tt = divmod(t, S5_OCT)
    acc = None
    for q in range(S5_OCT):
        r = ys[q][:, 128 * h:128 * (h + 1)]
        shift = (16 * (q - tt)) % 128
        if shift:
            r = pltpu.roll(r, shift, 1)
        acc = r if acc is None else jnp.where(blk == q, r, acc)
    return acc


def _s5_prompt_body(u_ref, ws_ref, wc_ref, cm_ref, gl_ref, d_ref, gb_ref, p1_ref, p2_ref,
                    y_ref, hf_ref, uf, yf, *, bsz, nc):
    half = ws_ref.shape[-1] // 2
    seq = nc * S5_T
    blk = lax.broadcasted_iota(I32, (nc, 128), 1) // 16

    def fold(b, carry):
        rows = [u_ref[pl.ds(pl.multiple_of(b * seq, seq) + t, nc, stride=S5_T), :]
                for t in range(S5_T)]
        for q in range(S5_OCT):
            uf[q, pl.ds(pl.multiple_of(b * nc, nc), nc), :] = _fold_time(rows, q, blk)
        return carry

    lax.fori_loop(0, bsz, fold, 0)

    for g in range(S5_OCT):
        u = uf[g]
        ub = u.astype(BF16)
        m_t, glu_t = _s5_group_operators(ws_ref[g], cm_ref[g], gl_ref[g])
        yi = _dot_t(ub, m_t)
        e = jnp.dot(ub, ws_ref[g].astype(BF16), preferred_element_type=F32)
        cpos = lax.broadcasted_iota(I32, e.shape, 0) & (nc - 1)
        p1 = p1_ref[g]
        p2 = p2_ref[g]
        z = jnp.where(cpos >= 1, pltpu.roll(e, 1, 0), 0.0)
        d, k = 1, 0
        while d < nc:
            zs = jnp.where(cpos >= d, pltpu.roll(z, d, 0), 0.0)
            z = z + zs * p1[k:k + 1] + pltpu.roll(zs, half, 1) * p2[k:k + 1]
            d, k = d * 2, k + 1
        yc = jnp.dot(z.astype(BF16), wc_ref[g], preferred_element_type=F32)
        yf[g] = _s5_tail(yi + yc + d_ref[g] * u, glu_t, gb_ref[g])
        hfin = e + z * p1[0:1] + pltpu.roll(z, half, 1) * p2[0:1]
        for b in range(bsz):
            r = b * nc + nc - 1
            hf_ref[g, b:b + 1, :] = hfin[r:r + 1, :]

    def unfold(b, carry):
        ys = [yf[q, pl.ds(pl.multiple_of(b * nc, nc), nc), :] for q in range(S5_OCT)]
        for t in range(S5_T):
            y_ref[pl.ds(pl.multiple_of(b * seq, seq) + t, nc, stride=S5_T), :] = (
                _unfold_time(ys, t, blk))
        return carry

    lax.fori_loop(0, bsz, unfold, 0)


def _s5_prompt(proj, col0, ops, bsz, nc):
    n = proj.shape[0]
    g, w, p2x = ops['ws'].shape
    ch = ops['cmat'].shape[1]
    blk = lambda s1, s2: pl.BlockSpec((S5_OCT, s1, s2), lambda i: (i, 0, 0))
    return pl.pallas_call(
        functools.partial(_s5_prompt_body, bsz=bsz, nc=nc),
        grid=(g // S5_OCT,),
        in_specs=[pl.BlockSpec((n, 128), lambda i: (0, col0 + i)),
                  blk(w, p2x), blk(p2x, w), blk(ch, p2x), blk(ch, ch),
                  blk(1, w), blk(1, w), blk(8, p2x), blk(8, p2x)],
        out_specs=[pl.BlockSpec((n, 128), lambda i: (0, i)), blk(bsz, p2x)],
        out_shape=[jax.ShapeDtypeStruct((n, g * 16), F32),
                   jax.ShapeDtypeStruct((g, bsz, p2x), F32)],
        scratch_shapes=[pltpu.VMEM((S5_OCT, bsz * nc, w), F32),
                        pltpu.VMEM((S5_OCT, bsz * nc, w), F32)],
        compiler_params=_cparams(1),
        name="s5_prompt",
    )(proj, ops['ws'], ops['wc'], ops['cmat'], ops['glu'], ops['d'], ops['gb'], ops['p1'], ops['p2'])


def _s5_sample_body(u_ref, h0_ref, ws_ref, wc_ref, cm_ref, gl_ref, d_ref, gb_ref, p1_ref, p2_ref,
                    y_ref, hf_ref, *, steps):
    half = ws_ref.shape[-1] // 2
    nseq = u_ref.shape[0] // steps
    w = steps * cm_ref.shape[1]
    blk = lax.broadcasted_iota(I32, (nseq, 128), 1) // 16
    rows = [u_ref[pl.ds(t, nseq, stride=steps), :] for t in range(steps)]
    ys = []
    for g in range(S5_OCT):
        u = _fold_time(rows, g, blk)
        h0 = h0_ref[g]
        ub = u.astype(BF16)
        m_t, glu_t = _s5_group_operators(ws_ref[g], cm_ref[g], gl_ref[g])
        ws = ws_ref[g]
        yi = _dot_t(ub, m_t[:w, :w])
        e = jnp.dot(ub, ws[ws.shape[0] - w:, :].astype(BF16), preferred_element_type=F32)
        yc = jnp.dot(h0.astype(BF16), wc_ref[g], preferred_element_type=F32)
        ys.append(_s5_tail(yi + yc + d_ref[g] * u, glu_t[:w, :w], gb_ref[g]))
        hf_ref[g] = e + h0 * p1_ref[g] + pltpu.roll(h0, half, 1) * p2_ref[g]
    for t in range(steps):
        y_ref[pl.ds(t, nseq, stride=steps), :] = _unfold_time(ys, t, blk)


def _s5_sample(proj, col0, h0, ops, steps):
    n = proj.shape[0]
    g, r, p2x = h0.shape
    wfull = ops['ws'].shape[1]
    ch = ops['cmat'].shape[1]
    w = steps * ch
    blk = lambda s1, s2: pl.BlockSpec((S5_OCT, s1, s2), lambda i: (i, 0, 0))
    return pl.pallas_call(
        functools.partial(_s5_sample_body, steps=steps),
        grid=(g // S5_OCT,),
        in_specs=[pl.BlockSpec((n, 128), lambda i: (0, col0 + i)),
                  blk(r, p2x), blk(wfull, p2x), blk(p2x, w), blk(ch, p2x), blk(ch, ch),
                  blk(1, w), blk(1, w), blk(1, p2x), blk(1, p2x)],
        out_specs=[pl.BlockSpec((n, 128), lambda i: (0, i)), blk(r, p2x)],
        out_shape=[jax.ShapeDtypeStruct((n, g * 16), F32),
                   jax.ShapeDtypeStruct((g, r, p2x), F32)],
        compiler_params=_cparams(1),
        name="s5_sample",
    )(proj, h0, ops['ws'], ops['wc'], ops['cmat'], ops['glu'], ops['d'], ops['gb'],
      ops['p1'], ops['p2'])


def _outproj_body(cnt0_ref, x_ref, yc_ref, ys_ref, wo_ref, gt_ref, g_ref, sc_ref, sh_ref,
                  wr_ref, br_ref, *rest, has_prev):
    x1_ref, h2_ref, ti_ref, gate_ref, rank_ref, cnt_ref, run_ref = rest[1:] if has_prev else rest

    @pl.when(pl.program_id(0) == 0)
    def _():
        run_ref[...] = cnt0_ref[...]

    dc = yc_ref.shape[1]
    mix = (jnp.dot(yc_ref[...], wo_ref[0:dc, :], preferred_element_type=F32)
           + jnp.dot(ys_ref[...].astype(BF16), wo_ref[dc:, :], preferred_element_type=F32))
    x1 = x_ref[...] + gt_ref[...] * mix
    x1_ref[...] = x1
    ms = jnp.mean(x1 * x1, axis=-1, keepdims=True)
    h = x1 * lax.rsqrt(ms + EPS) * g_ref[...]
    hb = (h * (1.0 + sc_ref[...]) + sh_ref[...]).astype(BF16)
    hb32 = hb.astype(F32)
    half = h2_ref.shape[1]
    h2_ref[...] = _pack_bf16_pair(hb32[:, :half], hb32[:, half:])
    logits =jnp.dot(hb, wr_ref[...], preferred_element_type=F32) + br_ref[...]
    tm, ne = logits.shape
    lane = lax.broadcasted_iota(I32, logits.shape, 1).astype(F32)
    work = logits
    vals, ids, sels = [], [], []
    for _ in range(TOP_K):
        m = jnp.max(work, axis=1, keepdims=True)
        idx = jnp.min(jnp.where(work == m, lane, float(ne)), axis=1, keepdims=True)
        sel = lane == idx
        vals.append(m)
        ids.append(idx)
        sels.append(sel)
        work = jnp.where(sel, -jnp.inf, work)
    exps = [jnp.exp(v - vals[0]) for v in vals]
    tot = exps[0]
    for ex in exps[1:]:
        tot = tot + ex
    gates = [ex / tot for ex in exps]
    onehot = sels[0]
    for s in sels[1:]:
        onehot = onehot | s
    onehot = onehot.astype(F32)
    row = lax.broadcasted_iota(I32, (tm, tm), 0)
    col = lax.broadcasted_iota(I32, (tm, tm), 1)
    below = (col < row).astype(BF16)
    before = jnp.dot(below, onehot.astype(BF16), preferred_element_type=F32) + run_ref[...]
    ranks = [jnp.sum(jnp.where(s, before, 0.0), axis=1, keepdims=True) for s in sels]
    run_ref[...] = run_ref[...] + jnp.sum(onehot, axis=0, keepdims=True)
    cnt_ref[...] = run_ref[...]

    wide = lax.broadcasted_iota(I32, ti_ref.shape, 1)

    def spread(cols):
        out = cols[TOP_K - 1]
        for k in range(TOP_K - 2, -1, -1):
            out = jnp.where(wide == k, cols[k], out)
        return out

    ti_ref[...] = spread(ids).astype(I32)
    gate_ref[...] = spread(gates)
    rank_ref[...] = spread(ranks).astype(I32)


def _outproj(cnt0, x, yc, ys, wo_bf, mod, per_row, rows_per_batch, g, wr_bf, br, h2_prev, n_all,
             row0, tm):
    n, d = x.shape
    dc = yc.shape[1]
    ne = wr_bf.shape[1]
    blk0 = row0 // tm
    row = lambda w: pl.BlockSpec((tm, w), lambda i: (i, 0))
    const = lambda s: pl.BlockSpec(s, lambda i: (0, 0))
    has_prev = h2_prev is not None
    in_specs = [const((1, ne)), row(d), row(dc), row(dc),
                _resident((d, d), lambda i: (0, 0)),
                _mod_spec(per_row, tm, rows_per_batch, d, 2), const((1, d)),
                _mod_spec(per_row, tm, rows_per_batch, d, 4),
                _mod_spec(per_row, tm, rows_per_batch, d, 3),
                const((d, ne)), const((1, ne))]
    args = [cnt0, x, yc, ys, wo_bf, mod, g.reshape(1, d), mod, mod, wr_bf, br.reshape(1, ne)]
    if has_prev:
        in_specs.append(pl.BlockSpec(memory_space=pl.ANY))
        args.append(h2_prev)
    return pl.pallas_call(
        functools.partial(_outproj_body, has_prev=has_prev),
        grid=(n // tm,),
        in_specs=in_specs,
        out_specs=[row(d), pl.BlockSpec((tm, d // 2), lambda i: (blk0 + i, 0)),
                   row(128), row(128), row(128), const((1, ne))],
        out_shape=[jax.ShapeDtypeStruct((n, d), F32),
                   jax.ShapeDtypeStruct((n_all, d // 2), jnp.uint32),
                   jax.ShapeDtypeStruct((n, 128), I32), jax.ShapeDtypeStruct((n, 128), F32),
                   jax.ShapeDtypeStruct((n, 128), I32), jax.ShapeDtypeStruct((1, ne), F32)],
        scratch_shapes=[pltpu.VMEM((1, ne), F32)],
        input_output_aliases={len(args) - 1: 1} if has_prev else {},
        compiler_params=_cparams(1),
        name="outproj_router",
    )(*args)


def _moe_body(be_ref, rb_ref, nch_ref, nrow_ref, tok0_ref, tokn_ref, h2_hbm,
              w1g_ref, w1l_ref, b1g_ref, b1l_ref, w2_ref, b2_ref,
              o_ref, xp, xb, act, sem, *, nf1):
    del be_ref, rb_ref
    b = pl.program_id(0)
    s = pl.program_id(1)
    n_chunks = nch_ref[b]

    def issue_rows(tok_ref, n_rows):
        def body(i, carry):
            r0 = pl.multiple_of(i * 8, 8)
            for u in range(8):
                pltpu.make_async_copy(h2_hbm.at[pl.ds(tok_ref[0, r0 + u], 1), :],
                                      xp.at[pl.ds(r0 + u, 1), :], sem.at[0]).start()
            return carry

        lax.fori_loop(0, n_rows // 8, body, 0)

    def wait_rows(n_rows):
        size = 8
        while size <= MOE_TB:
            @pl.when((n_rows & size) != 0)
            def _(size=size):
                pltpu.make_async_copy(h2_hbm.at[pl.ds(0, size), :], xp.at[pl.ds(0, size), :],
                                      sem.at[0]).wait()

            size *= 2

    @pl.when((b == 0) & (s == 0))
    def _():
        xp[...] = jnp.zeros_like(xp)
        issue_rows(tok0_ref, nrow_ref[0])

    @pl.when(s == 0)
    def _():
        wait_rows(nrow_ref[b])
        half = xp.shape[1]

        def unpack(i, carry):
            r0 = pl.multiple_of(i * MOE_CH, MOE_CH)
            xb[pl.ds(r0, MOE_CH), :half], xb[pl.ds(r0, MOE_CH), half:] = _unpack_bf16_pair(
                xp[pl.ds(r0, MOE_CH), :])
            return carry

        lax.fori_loop(0, n_chunks, unpack, 0)
        issue_rows(tokn_ref, nrow_ref[b + 1])

    def chunk_groups(one):
        assert MOE_TB // MOE_CH < 8
        for size in (4, 2, 1):
            @pl.when((n_chunks & size) != 0)
            def _(size=size):
                r0 = (n_chunks & ~(2 * size - 1)) * MOE_CH
                one(pl.multiple_of(r0, size * MOE_CH), size * MOE_CH)

    @pl.when((s < nf1) & (n_chunks > 0))
    def _():
        def one(r0, rows):
            x = xb[pl.ds(r0, rows), :]
            hg = jnp.dot(x, w1g_ref[...].astype(BF16), preferred_element_type=F32)
            hl = jnp.dot(x, w1l_ref[...].astype(BF16), preferred_element_type=F32)
            gl = jnp.minimum(hg + b1g_ref[...], SWIGLU_LIMIT)
            ln = jnp.clip(hl + b1l_ref[...], -SWIGLU_LIMIT, SWIGLU_LIMIT)
            a = gl * _sigmoid(SWIGLU_ALPHA * gl) * (ln + 1.0)
            act[s, pl.ds(r0, rows), :] = a.astype(BF16)

        chunk_groups(one)

    @pl.when((s >= nf1) & (n_chunks > 0))
    def _():
        def one(r0, rows):
            a = jnp.concatenate([act[k, pl.ds(r0, rows), :] for k in range(nf1)], axis=1)
            o_ref[pl.ds(r0, rows), :] = (
                jnp.dot(a, w2_ref[...].astype(BF16), preferred_element_type=F32) + b2_ref[...])

        chunk_groups(one)

        def zero(i, carry):
            r0 = pl.multiple_of(i * MOE_CH, MOE_CH)
            o_ref[pl.ds(r0, MOE_CH), :] = jnp.zeros((MOE_CH, o_ref.shape[1]), F32)
            return carry

        lax.fori_loop(n_chunks, MOE_TB // MOE_CH, zero, 0)


def _moe_experts(h2_all, slot_tok, w1, b1, w2, b2, blk_expert, blk_rows, blk_chunks, blk_nrow,
                 n_grid_blocks):
    ne, d, f2 = w1.shape
    f = f2 // 2
    nf1 = f // MOE_TF
    nf2 = d // MOE_TN
    nb = slot_tok.shape[0]

    def s1(b, s, nch):
        return jnp.where(nch[b] > 0, jnp.minimum(s, nf1 - 1), nf1 - 1)

    def s2(b, s, nch):
        return jnp.where(nch[b] > 0, jnp.maximum(s - nf1, 0), nf2 - 1)

    grid_spec = pltpu.PrefetchScalarGridSpec(
        num_scalar_prefetch=4,
        grid=(n_grid_blocks, nf1 + nf2),
        in_specs=[
            pl.BlockSpec((None, 1, MOE_TB), lambda b, s, be, rb, nch, nr: (0, 0, 0),
                         memory_space=pltpu.SMEM),
            pl.BlockSpec((None, 1, MOE_TB),
                         lambda b, s, be, rb, nch, nr: (jnp.minimum(b + 1, nb - 1), 0, 0),
                         memory_space=pltpu.SMEM),
            pl.BlockSpec(memory_space=pl.ANY),
            pl.BlockSpec((None, d, MOE_TF), lambda b, s, be, rb, nch, nr: (be[b], 0, s1(b, s, nch))),
            pl.BlockSpec((None, d, MOE_TF),
                         lambda b, s, be, rb, nch, nr: (be[b], 0, nf1 + s1(b, s, nch))),
            pl.BlockSpec((None, 1, MOE_TF), lambda b, s, be, rb, nch, nr: (be[b], 0, s1(b, s, nch))),
            pl.BlockSpec((None, 1, MOE_TF),
                         lambda b, s, be, rb, nch, nr: (be[b], 0, nf1 + s1(b, s, nch))),
            pl.BlockSpec((None, f, MOE_TN), lambda b, s, be, rb, nch, nr: (be[b], 0, s2(b, s, nch))),
            pl.BlockSpec((None, 1, MOE_TN), lambda b, s, be, rb, nch, nr: (be[b], 0, s2(b, s, nch))),
        ],
        out_specs=pl.BlockSpec((MOE_TB, MOE_TN),
                               lambda b, s, be, rb, nch, nr: (rb[b], s2(b, s, nch))),
        scratch_shapes=[pltpu.VMEM((MOE_TB, d // 2), jnp.uint32), pltpu.VMEM((MOE_TB, d), BF16),
                        pltpu.VMEM((nf1, MOE_TB, MOE_TF), BF16),
                        pltpu.SemaphoreType.DMA((1,))],
    )
    return pl.pallas_call(
        functools.partial(_moe_body, nf1=nf1),
        grid_spec=grid_spec,
        out_shape=jax.ShapeDtypeStruct((nb * MOE_TB, d), F32),
        compiler_params=_cparams(2),
        name="moe_experts",
    )(blk_expert, blk_rows, blk_chunks, blk_nrow, slot_tok, slot_tok, h2_all,
      w1, w1, b1.reshape(ne, 1, f2), b1.reshape(ne, 1, f2), w2, b2.reshape(ne, 1, d))


def _moe_plan(counts, top_i, rank, n_tokens, n_blocks):
    ne = counts.shape[0]
    nblk = (counts + MOE_TB - 1) // MOE_TB
    blk_end = jnp.cumsum(nblk)
    blk_start = blk_end - nblk
    n_used = blk_end[-1]
    dest = blk_start[top_i] * MOE_TB + rank
    bidx = jnp.arange(n_blocks + 1, dtype=I32)
    be = jnp.minimum(jnp.searchsorted(blk_end, bidx, side='right'), ne - 1).astype(I32)
    active = bidx < n_used
    valid = jnp.where(active, jnp.clip(counts[be] - (bidx - blk_start[be]) * MOE_TB, 0, MOE_TB), 0)
    chunks = ((valid + MOE_CH - 1) // MOE_CH).astype(I32)
    nrow = ((valid + 7) // 8 * 8).astype(I32)
    last = jnp.maximum(n_used - 1, 0)
    be = jnp.where(active, be, be[last]).astype(I32)
    rows = jnp.where(active, bidx, last).astype(I32)
    tok = jnp.arange(n_tokens * TOP_K, dtype=I32) // TOP_K
    slot_tok = jnp.zeros((n_blocks * MOE_TB,), I32).at[dest.reshape(-1)].set(tok)
    return dest, slot_tok.reshape(n_blocks, 1, MOE_TB), be, rows, chunks, nrow, n_used


def _combine_body(d0_ref, dn_ref, x1_ref, gate_ref, gt_ref, gf_ref, ys_hbm, o_ref, ybuf, sem):
    i = pl.program_id(0)
    tm = x1_ref.shape[0]
    slot = lax.rem(i, 2)

    def issue_rows(dest_ref, sl):
        def body(i, carry):
            r0 = pl.multiple_of(i * 8, 8)
            for u in range(8):
                for k in range(TOP_K):
                    pltpu.make_async_copy(
                        ys_hbm.at[pl.ds(dest_ref[0, (r0 + u) * TOP_K + k], 1), :],
                        ybuf.at[sl, k, pl.ds(r0 + u, 1), :], sem.at[sl]).start()
            return carry

        lax.fori_loop(0, tm // 8, body, 0)

    @pl.when(i == 0)
    def _():
        issue_rows(d0_ref, 0)

    @pl.when(i + 1 < pl.num_programs(0))
    def _():
        issue_rows(dn_ref, 1 - slot)

    pltpu.make_async_copy(ybuf.at[slot], ybuf.at[slot], sem.at[slot]).wait()
    gates = gate_ref[...]
    y = gates[:, 0:1] * ybuf[slot, 0]
    for k in range(1, TOP_K):
        y = y + gates[:, k:k + 1] * ybuf[slot, k]
    x2 = x1_ref[...] + gt_ref[...] * y
    ms = jnp.mean(x2 * x2, axis=-1, keepdims=True)
    o_ref[...] = x2 * lax.rsqrt(ms + EPS) * gf_ref[...]


def _combine(x1, dest, ys, gates, mod, per_row, rows_per_batch, g_final, tm):
    n, d = x1.shape
    nt = n // tm
    dest3 = dest.reshape(nt, 1, tm * TOP_K)
    return pl.pallas_call(
        _combine_body,
        grid=(nt,),
        in_specs=[pl.BlockSpec((None, 1, tm * TOP_K), lambda i: (0, 0, 0), memory_space=pltpu.SMEM),
                  pl.BlockSpec((None, 1, tm * TOP_K), lambda i: (jnp.minimum(i + 1, nt - 1), 0, 0),
                               memory_space=pltpu.SMEM),
                  pl.BlockSpec((tm, d), lambda i: (i, 0)),
                  pl.BlockSpec((tm, 128), lambda i: (i, 0)),
                  _mod_spec(per_row, tm, rows_per_batch, d, 5),
                  pl.BlockSpec((1, d), lambda i: (0, 0)),
                  pl.BlockSpec(memory_space=pl.ANY)],
        out_specs=pl.BlockSpec((tm, d), lambda i: (i, 0)),
        out_shape=jax.ShapeDtypeStruct((n, d), F32),
        scratch_shapes=[pltpu.VMEM((2, TOP_K, tm, d), F32), pltpu.SemaphoreType.DMA((2,))],
        compiler_params=_cparams(1),
        name="combine_norm",
    )(dest3, dest3, x1, gates, mod, g_final.reshape(1, d), ys)


def kernel(x_prompt, x_sample, c_prompt, c_sample, state_conv, state_ssm_re, state_ssm_im, g_mix, g_ffn, w_mod, b_mod, w_in, conv_w, ssm_a_re, ssm_a_im, ssm_log_dt, ssm_b_re, ssm_b_im, ssm_c_re, ssm_c_im, ssm_d, glu_w, glu_b, w_out, w_router, b_router, w1, b1, w2, b2, g_final):
    bp, lp, d = x_prompt.shape
    bs, ls, _ = x_sample.shape
    depth = g_mix.shape[0]
    dc = conv_w.shape[-1]
    n_grp, p_st, ch = ssm_b_re.shape[1:]
    ne = w_router.shape[-1]
    np_, ns_ = bp * lp, bs * ls
    n_tok = np_ + ns_
    assert ls == S5_T // 2 and lp % S5_T == 0 and conv_w.shape[1] == 3
    nc = lp // S5_T
    n_blocks = -(-(n_tok * TOP_K) // MOE_TB) + ne

    xp = x_prompt.reshape(np_, d)
    xs = x_sample.reshape(ns_, d)
    c_all = jnp.concatenate([c_prompt, c_sample], axis=0)
    pad = (-c_all.shape[0]) % 8
    c_all = jnp.pad(c_all, ((0, pad), (0, 0)))

    assert depth == 1
    outs = [[] for _ in range(6)]
    for l in range(depth):
        m = _modulation(c_all, w_mod[l], b_mod[l])
        mod_p = m[:bp].reshape(bp, 1, N_MOD * d)
        mod_s = jnp.repeat(m[bp:bp + bs], ls, axis=0)

        w_in_bf = w_in[l].astype(BF16)
        proj_p = _inproj(xp, mod_p, False, lp, g_mix[l], w_in_bf, 512)
        proj_s = _inproj(xs, mod_s, True, ls, g_mix[l], w_in_bf, 256)

        yc_p, tail_p = _conv_prompt(proj_p, conv_w[l], bp, lp, dc, 512)
        e_s = jnp.pad(state_conv[l], ((0, 0), (0, ls - 2), (0, 0))).reshape(ns_, dc)
        yc_s, z_s = _conv_sample(proj_s, e_s, conv_w[l], dc, 256)
        new_conv_p = tail_p[:, 6:8, :]
        new_conv_s = z_s.reshape(bs, ls, dc)[:, ls - 2:, :]

        ops = _s5_operators(ssm_a_re[l], ssm_a_im[l], ssm_log_dt[l], ssm_b_re[l], ssm_b_im[l],
                            ssm_c_re[l], ssm_c_im[l], ssm_d[l], glu_w[l], glu_b[l])
        w8 = ls * ch
        ucol0 = (3 * dc) // 128
        ys_p, hf_p = _s5_prompt(proj_p, ucol0, ops, bp, nc)
        new_re_p = hf_p[:, :, :p_st].transpose(1, 0, 2)
        new_im_p = hf_p[:, :, p_st:].transpose(1, 0, 2)

        h0 = jnp.concatenate([state_ssm_re[l], state_ssm_im[l]], axis=-1).transpose(1, 0, 2)
        ops_s = dict(ops, wc=ops['wc'][:, :, :w8], d=ops['d'][:, :, :w8], gb=ops['gb'][:, :, :w8],
                     p1=ops['p1_half'], p2=ops['p2_half'])
        ys_s, hf_s = _s5_sample(proj_s, ucol0, h0, ops_s, ls)
        new_re_s = hf_s[:, :, :p_st].transpose(1, 0, 2)
        new_im_s = hf_s[:, :, p_st:].transpose(1, 0, 2)

        wo_bf = w_out[l].astype(BF16)
        wr_bf = w_router[l].astype(BF16)
        cnt0 = jnp.zeros((1, ne), F32)
        x1_p, h2_all, ti_p, gate_p, rank_p, cnt1 = _outproj(
            cnt0, xp, yc_p, ys_p, wo_bf, mod_p, False, lp, g_ffn[l], wr_bf, b_router[l],
            None, n_tok, 0, 256)
        x1_s, h2_all, ti_s, gate_s, rank_s, cnt2 = _outproj(
            cnt1, xs, yc_s, ys_s, wo_bf, mod_s, True, ls, g_ffn[l], wr_bf, b_router[l],
            h2_all, n_tok, np_, 256)

        counts = cnt2[0].astype(I32)
        top_i = jnp.concatenate([ti_p[:, :TOP_K], ti_s[:, :TOP_K]], axis=0)
        rank = jnp.concatenate([rank_p[:, :TOP_K], rank_s[:, :TOP_K]], axis=0)
        dest, slot_tok, blk_e, blk_rows, blk_chunks, blk_nrow, n_used = _moe_plan(
            counts, top_i, rank, n_tok, n_blocks)
        y_sorted = _moe_experts(h2_all, slot_tok, w1[l], b1[l], w2[l], b2[l],
                                blk_e, blk_rows, blk_chunks, blk_nrow, n_used)

        xp = _combine(x1_p, dest[:np_], y_sorted, gate_p, mod_p, False, lp, g_final, 128)
        xs = _combine(x1_s, dest[np_:], y_sorted, gate_s, mod_s, True, ls, g_final, 128)
        for lst, val in zip(outs, (new_conv_p, new_re_p, new_im_p, new_conv_s, new_re_s, new_im_s)):
            lst.append(val)

    y_prompt = xp.reshape(bp, lp, d)
    y_sample = xs.reshape(bs, ls, d)
    return (y_prompt, y_sample) + tuple(jnp.stack(o) for o in outs)
```

```python
import functools
import math

import jax
import jax.numpy as jnp
from jax import lax
from jax.experimental import pallas as pl
from jax.experimental.pallas import tpu as pltpu

F32 = jnp.float32
BF16 = jnp.bfloat16
I32 = jnp.int32

LANES = 128
EPS = 1e-6
N_MOD = 6
TOP_K = 4
SWIGLU_LIMIT = 7.0
SWIGLU_ALPHA = 1.702
GELU_C = math.sqrt(2.0 / math.pi)

S5_T = 16
OUT_SUB = 256
MOE_TB = 1536
MOE_CH = 256
MOE_TF = 512
MOE_TN = 512
VMEM_LIMIT = 56 * 1024 * 1024
HI = lax.Precision.HIGHEST


def _cparams(n_axes):
    return pltpu.CompilerParams(dimension_semantics=("arbitrary",) * n_axes,
                                vmem_limit_bytes=VMEM_LIMIT)


def _sigmoid(x):
    return 1.0 / (1.0 + jnp.exp(-x))


def _pack_bf16_pair(lo, hi):
    lo_bits = lax.bitcast_convert_type(lo, jnp.uint32) >> 16
    hi_bits = lax.bitcast_convert_type(hi, jnp.uint32) & jnp.uint32(0xFFFF0000)
    return lo_bits | hi_bits


def _unpack_bf16_pair(packed):
    lo = lax.bitcast_convert_type(packed << 16, F32)
    hi = lax.bitcast_convert_type(packed & jnp.uint32(0xFFFF0000), F32)
    return lo.astype(BF16), hi.astype(BF16)


def _resident(shape, index_map):
    return pl.BlockSpec(shape, index_map, pipeline_mode=pl.Buffered(1))


def _mod_body(c_ref, w_ref, b_ref, o_ref):
    c = c_ref[...]
    s = c * _sigmoid(c)
    o_ref[...] = jnp.dot(s.astype(BF16), w_ref[...].astype(BF16),
                         preferred_element_type=F32) + b_ref[...]


def _modulation(c_all, w_mod, b_mod):
    rows, d = c_all.shape
    n = w_mod.shape[1]
    tn = 1024
    return pl.pallas_call(
        _mod_body,
        grid=(n // tn,),
        in_specs=[pl.BlockSpec((rows, d), lambda j: (0, 0)),
                  pl.BlockSpec((d, tn), lambda j: (0, j)),
                  pl.BlockSpec((1, tn), lambda j: (0, j))],
        out_specs=pl.BlockSpec((rows, tn), lambda j: (0, j)),
        out_shape=jax.ShapeDtypeStruct((rows, n), F32),
        compiler_params=_cparams(1),
        name="modulation",
    )(c_all, w_mod, b_mod.reshape(1, n))


def _mod_spec(per_row, tm, rows_per_batch, d, col):
    if per_row:
        return pl.BlockSpec((tm, d), lambda i: (i, col))
    return pl.BlockSpec((None, 1, d), lambda i: ((i * tm) // rows_per_batch, 0, col))


def _inproj_body(x_ref, g_ref, sc_ref, sh_ref, w_ref, o_ref):
    x = x_ref[...]
    ms = jnp.mean(x * x, axis=-1, keepdims=True)
    h = x * lax.rsqrt(ms + EPS) * g_ref[...]
    h = h * (1.0 + sc_ref[...]) + sh_ref[...]
    o_ref[...] = jnp.dot(h.astype(BF16), w_ref[...], preferred_element_type=F32)


def _inproj(x, mod, per_row, rows_per_batch, g, w_bf, tm):
    n, d = x.shape
    dn = w_bf.shape[1]
    return pl.pallas_call(
        _inproj_body,
        grid=(n // tm,),
        in_specs=[pl.BlockSpec((tm, d), lambda i: (i, 0)),
                  pl.BlockSpec((1, d), lambda i: (0, 0)),
                  _mod_spec(per_row, tm, rows_per_batch, d, 1),
                  _mod_spec(per_row, tm, rows_per_batch, d, 0),
                  _resident((d, dn), lambda i: (0, 0))],
        out_specs=pl.BlockSpec((tm, dn), lambda i: (i, 0)),
        out_shape=jax.ShapeDtypeStruct((n, dn), F32),
        compiler_params=_cparams(1),
        name="inproj",
    )(x, g.reshape(1, d), mod, mod, w_bf)


def _conv_prompt_body(b_ref, c_ref, v_ref, w_ref, y_ref, tail_ref, carry_ref):
    @pl.when(pl.program_id(1) == 0)
    def _():
        carry_ref[...] = jnp.zeros_like(carry_ref)

    z = c_ref[...] * v_ref[...]
    tt = z.shape[0]
    zc = jnp.concatenate([carry_ref[...], z], axis=0)
    z1 = pltpu.roll(zc, 1, 0)[8:]
    z2 = pltpu.roll(zc, 2, 0)[8:]
    w = w_ref[...]
    y = w[0:1] * z2 + w[1:2] * z1 + w[2:3] * z
    y_ref[...] = (b_ref[...] * y).astype(y_ref.dtype)
    carry_ref[...] = z[tt - 8:]
    tail_ref[...] = z[tt - 8:]


def _conv_prompt(proj, conv_w, bsz, seq, dc, tt):
    nt = seq // tt
    spec = lambda col: pl.BlockSpec((tt, dc), lambda b, t: (b * nt + t, col))
    return pl.pallas_call(
        _conv_prompt_body,
        grid=(bsz, nt),
        in_specs=[spec(0), spec(1), spec(2),
                  pl.BlockSpec(conv_w.shape, lambda b, t: (0, 0))],
        out_specs=[pl.BlockSpec((tt, dc), lambda b, t: (b * nt + t, 0)),
                   pl.BlockSpec((None, 8, dc), lambda b, t: (b, 0, 0))],
        out_shape=[jax.ShapeDtypeStruct((bsz * seq, dc), BF16),
                   jax.ShapeDtypeStruct((bsz, 8, dc), F32)],
        scratch_shapes=[pltpu.VMEM((8, dc), F32)],
        compiler_params=_cparams(2),
        name="conv_prompt",
    )(proj, proj, proj, conv_w)


def _conv_sample_body(b_ref, c_ref, v_ref, e_ref, w_ref, y_ref, z_ref):
    z = c_ref[...] * v_ref[...]
    e = e_ref[...]
    rows = z.shape[0]
    tpos = lax.broadcasted_iota(I32, z.shape, 0) & 7
    z1 = jnp.where(tpos == 0, pltpu.roll(e, rows - 1, 0), pltpu.roll(z, 1, 0))
    z2 = jnp.where(tpos < 2, e, pltpu.roll(z, 2, 0))
    w = w_ref[...]
    y = w[0:1] * z2 + w[1:2] * z1 + w[2:3] * z
    y_ref[...] = (b_ref[...] * y).astype(y_ref.dtype)
    z_ref[...] = z


def _conv_sample(proj, e, conv_w, dc, tm):
    n = proj.shape[0]
    spec = lambda col: pl.BlockSpec((tm, dc), lambda i: (i, col))
    return pl.pallas_call(
        _conv_sample_body,
        grid=(n // tm,),
        in_specs=[spec(0), spec(1), spec(2), spec(0),
                  pl.BlockSpec(conv_w.shape, lambda i: (0, 0))],
        out_specs=[spec(0), spec(0)],
        out_shape=[jax.ShapeDtypeStruct((n, dc), BF16),
                   jax.ShapeDtypeStruct((n, dc), F32)],
        compiler_params=_cparams(1),
        name="conv_sample",
    )(proj, proj, proj, e, conv_w)


def _s5_operators(a_re, a_im, log_dt, b_re, b_im, c_re, c_im, d_skip, glu_w, glu_b):
    g, p, ch = b_re.shape
    t = S5_T
    dt = jnp.exp(log_dt)[:, None]
    mag = jnp.exp(a_re * dt)
    ar = mag * jnp.cos(a_im * dt)
    ai = mag * jnp.sin(a_im * dt)
    den = a_re * a_re + a_im * a_im
    qr = ((ar - 1.0) * a_re + ai * a_im) / den
    qi = (ai * a_re - (ar - 1.0) * a_im) / den
    bbr = qr[..., None] * b_re - qi[..., None] * b_im
    bbi = qr[..., None] * b_im + qi[..., None] * b_re
    pr, pi = [jnp.ones_like(ar)], [jnp.zeros_like(ar)]
    for _ in range(t):
        pr, pi = pr + [pr[-1] * ar - pi[-1] * ai], pi + [pr[-1] * ai + pi[-1] * ar]
    pw_r, pw_i = jnp.stack(pr), jnp.stack(pi)
    car = c_re[None] * pw_r[:, :, None, :] - c_im[None] * pw_i[:, :, None, :]
    cai = c_re[None] * pw_i[:, :, None, :] + c_im[None] * pw_r[:, :, None, :]
    rev_r, rev_i = pw_r[t - 1::-1][:t], pw_i[t - 1::-1][:t]
    wsr = rev_r[..., None] * bbr[None] - rev_i[..., None] * bbi[None]
    wsi = rev_r[..., None] * bbi[None] + rev_i[..., None] * bbr[None]
    ws_op = jnp.concatenate([wsr.transpose(1, 0, 3, 2), wsi.transpose(1, 0, 3, 2)],
                            axis=-1).reshape(g, t * ch, 2 * p)
    wcr = car[1:].transpose(1, 3, 0, 2).reshape(g, p, t * ch)
    wci = cai[1:].transpose(1, 3, 0, 2).reshape(g, p, t * ch)
    wc_op = jnp.concatenate([wcr, -wci], axis=1)
    cmat_t = jnp.concatenate([c_re, -c_im], axis=-1)
    glu_t = jnp.swapaxes(glu_w, 1, 2)
    d_flat = jnp.tile(d_skip, (1, t))[:, None, :]
    gb_flat = jnp.tile(glu_b, (1, t))[:, None, :]

    def rot_tables(xr, xi):
        return jnp.concatenate([xr, xr], axis=-1), jnp.concatenate([-xi, xi], axis=-1)

    sr, si = pw_r[t], pw_i[t]
    p1s, p2s = [], []
    for _ in range(8):
        t1, t2 = rot_tables(sr, si)
        p1s.append(t1)
        p2s.append(t2)
        sr, si = sr * sr - si * si, 2.0 * sr * si
    h = t // 2
    p1_half, p2_half = rot_tables(pw_r[h], pw_i[h])
    return dict(
        ws=ws_op, wc=wc_op.astype(BF16), cmat=cmat_t, glu=glu_t,
        d=d_flat, gb=gb_flat, p1=jnp.stack(p1s, axis=1), p2=jnp.stack(p2s, axis=1),
        p1_half=p1_half[:, None, :], p2_half=p2_half[:, None, :])


def _gelu_tanh(y):
    return 0.5 * y * (1.0 + jnp.tanh(GELU_C * (y + 0.044715 * (y * y * y))))


def _dot_t(x, w_t):
    return lax.dot_general(x, w_t, (((1,), (1,)), ((), ())), preferred_element_type=F32)


def _block_toeplitz_t(seq, steps):
    ch, n = seq.shape
    lane = lax.broadcasted_iota(I32, seq.shape, 1)
    rows = []
    for t in range(steps):
        shift = (ch * (t + 1)) % n
        r = pltpu.roll(seq, shift, 1) if shift else seq
        rows.append(jnp.where(lane < ch * (t + 1), r, 0.0))
    return jnp.concatenate(rows, axis=0).astype(BF16)


def _s5_group_operators(ws, cmat, glu_t, steps=S5_T):
    n = ws.shape[0]
    seq = lax.dot_general(cmat, ws, (((1,), (1,)), ((), ())), precision=HI,
                          preferred_element_type=F32)
    ch = cmat.shape[0]
    glu_seq = jnp.concatenate([jnp.zeros((ch, n - ch), F32), glu_t], axis=1)
    return _block_toeplitz_t(seq, steps), _block_toeplitz_t(glu_seq, steps)


def _s5_tail(y, gl_t, gb):
    y = _gelu_tanh(y)
    gate = _dot_t(y.astype(BF16), gl_t) + gb
    return y * _sigmoid(gate)


S5_CH = 16
S5_OCT = LANES // S5_CH


def _fold_time(rows, blk):
    halves = [_block_transpose(rows[h:h + S5_OCT], blk) for h in range(0, len(rows), S5_OCT)]
    if len(halves) == 1:
        return halves[0]
    return [jnp.concatenate([hv[q] for hv in halves], axis=1) for q in range(S5_OCT)]


def _unfold_time(ys, blk):
    rows = []
    for h in range(ys[0].shape[1] // LANES):
        rows += _block_transpose([y[:, LANES * h:LANES * (h + 1)] for y in ys], blk)
    return rows


def _block_transpose(vs, blk):
    vs = list(vs)
    for d in (4, 2, 1):
        upper = (blk & d) != 0
        new = list(vs)
        for i in range(S5_OCT):
            if i & d:
                continue
            a, b = vs[i], vs[i + d]
            new[i] = jnp.where(upper, pltpu.roll(b, S5_CH * d, 1), a)
            new[i + d] = jnp.where(upper, b, pltpu.roll(a, LANES - S5_CH * d, 1))
        vs = new
    return vs


def _s5_prompt_body(u_ref, ws_ref, wc_ref, cm_ref, gl_ref, d_ref, gb_ref, p1_ref, p2_ref,
                    y_ref, hf_ref, uf, yf, *, bsz, nc):
    half = ws_ref.shape[-1] // 2
    seq = nc * S5_T
    blk = lax.broadcasted_iota(I32, (nc, LANES), 1) // S5_CH

    def fold(b, carry):
        rows = [u_ref[pl.ds(pl.multiple_of(b * seq, seq) + t, nc, stride=S5_T), :]
                for t in range(S5_T)]
        for q, uq in enumerate(_fold_time(rows, blk)):
            uf[q, pl.ds(pl.multiple_of(b * nc, nc), nc), :] = uq
        return carry

    lax.fori_loop(0, bsz, fold, 0)

    for g in range(S5_OCT):
        u = uf[g]
        ub = u.astype(BF16)
        m_t, glu_t = _s5_group_operators(ws_ref[g], cm_ref[g], gl_ref[g])
        yi = _dot_t(ub, m_t)
        e = jnp.dot(ub, ws_ref[g].astype(BF16), preferred_element_type=F32)
        cpos = lax.broadcasted_iota(I32, e.shape, 0) & (nc - 1)
        p1 = p1_ref[g]
        p2 = p2_ref[g]
        z = jnp.where(cpos >= 1, pltpu.roll(e, 1, 0), 0.0)
        d, k = 1, 0
        while d < nc:
            zs = jnp.where(cpos >= d, pltpu.roll(z, d, 0), 0.0)
            z = z + zs * p1[k:k + 1] + pltpu.roll(zs, half, 1) * p2[k:k + 1]
            d, k = d * 2, k + 1
        yc = jnp.dot(z.astype(BF16), wc_ref[g], preferred_element_type=F32)
        yf[g] = _s5_tail(yi + yc + d_ref[g] * u, glu_t, gb_ref[g])
        hfin = e + z * p1[0:1] + pltpu.roll(z, half, 1) * p2[0:1]
        for b in range(bsz):
            r = b * nc + nc - 1
            hf_ref[g, b:b + 1, :] = hfin[r:r + 1, :]

    def unfold(b, carry):
        ys = [yf[q, pl.ds(pl.multiple_of(b * nc, nc), nc), :] for q in range(S5_OCT)]
        for t, row in enumerate(_unfold_time(ys, blk)):
            y_ref[pl.ds(pl.multiple_of(b * seq, seq) + t, nc, stride=S5_T), :] = row
        return carry

    lax.fori_loop(0, bsz, unfold, 0)


def _s5_prompt(proj, col0, ops, bsz, nc):
    n = proj.shape[0]
    g, w, p2x = ops['ws'].shape
    ch = ops['cmat'].shape[1]
    blk = lambda s1, s2: pl.BlockSpec((S5_OCT, s1, s2), lambda i: (i, 0, 0))
    return pl.pallas_call(
        functools.partial(_s5_prompt_body, bsz=bsz, nc=nc),
        grid=(g // S5_OCT,),
        in_specs=[pl.BlockSpec((n, LANES), lambda i: (0, col0 + i)),
                  blk(w, p2x), blk(p2x, w), blk(ch, p2x), blk(ch, ch),
                  blk(1, w), blk(1, w), blk(8, p2x), blk(8, p2x)],
        out_specs=[pl.BlockSpec((n, LANES), lambda i: (0, i)), blk(bsz, p2x)],
        out_shape=[jax.ShapeDtypeStruct((n, g * S5_CH), F32),
                   jax.ShapeDtypeStruct((g, bsz, p2x), F32)],
        scratch_shapes=[pltpu.VMEM((S5_OCT, bsz * nc, w), F32),
                        pltpu.VMEM((S5_OCT, bsz * nc, w), F32)],
        compiler_params=_cparams(1),
        name="s5_prompt",
    )(proj, ops['ws'], ops['wc'], ops['cmat'], ops['glu'], ops['d'], ops['gb'], ops['p1'], ops['p2'])


def _s5_sample_body(u_ref, h0_ref, ws_ref, wc_ref, cm_ref, gl_ref, d_ref, gb_ref, p1_ref, p2_ref,
                    y_ref, hf_ref, *, steps):
    half = ws_ref.shape[-1] // 2
    nseq = u_ref.shape[0] // steps
    w = steps * cm_ref.shape[1]
    blk = lax.broadcasted_iota(I32, (nseq, LANES), 1) // S5_CH
    us = _fold_time([u_ref[pl.ds(t, nseq, stride=steps), :] for t in range(steps)], blk)
    ys = []
    for g in range(S5_OCT):
        u = us[g]
        h0 = h0_ref[g]
        ub = u.astype(BF16)
        m_t, glu_t = _s5_group_operators(ws_ref[g], cm_ref[g], gl_ref[g], steps)
        ws = ws_ref[g]
        yi = _dot_t(ub, m_t[:, :w])
        e = jnp.dot(ub, ws[ws.shape[0] - w:, :].astype(BF16), preferred_element_type=F32)
        yc = jnp.dot(h0.astype(BF16), wc_ref[g], preferred_element_type=F32)
        ys.append(_s5_tail(yi + yc + d_ref[g] * u, glu_t[:, :w], gb_ref[g]))
        hf_ref[g] = e + h0 * p1_ref[g] + pltpu.roll(h0, half, 1) * p2_ref[g]
    for t, row in enumerate(_unfold_time(ys, blk)):
        y_ref[pl.ds(t, nseq, stride=steps), :] = row


def _s5_sample(proj, col0, h0, ops, steps):
    n = proj.shape[0]
    g, r, p2x = h0.shape
    wfull = ops['ws'].shape[1]
    ch = ops['cmat'].shape[1]
    w = steps * ch
    blk = lambda s1, s2: pl.BlockSpec((S5_OCT, s1, s2), lambda i: (i, 0, 0))
    return pl.pallas_call(
        functools.partial(_s5_sample_body, steps=steps),
        grid=(g // S5_OCT,),
        in_specs=[pl.BlockSpec((n, LANES), lambda i: (0, col0 + i)),
                  blk(r, p2x), blk(wfull, p2x), blk(p2x, w), blk(ch, p2x), blk(ch, ch),
                  blk(1, w), blk(1, w), blk(1, p2x), blk(1, p2x)],
        out_specs=[pl.BlockSpec((n, LANES), lambda i: (0, i)), blk(r, p2x)],
        out_shape=[jax.ShapeDtypeStruct((n, g * S5_CH), F32),
                   jax.ShapeDtypeStruct((g, r, p2x), F32)],
        compiler_params=_cparams(1),
        name="s5_sample",
    )(proj, h0, ops['ws'], ops['wc'], ops['cmat'], ops['glu'], ops['d'], ops['gb'],
      ops['p1'], ops['p2'])


def _outproj_body(cnt0_ref, x_ref, yc_ref, ys_ref, wo_ref, gt_ref, g_ref, sc_ref, sh_ref,
                  wr_ref, br_ref, *rest, has_prev):
    x1_ref, h2_ref, ti_ref, gate_ref, rank_ref, cnt_ref, run_ref = rest[1:] if has_prev else rest

    @pl.when(pl.program_id(0) == 0)
    def _():
        run_ref[...] = cnt0_ref[...]

    subs = [slice(r0, r0 + OUT_SUB) for r0 in range(0, x_ref.shape[0], OUT_SUB)]
    dc = yc_ref.shape[1]

    def mod(ref, rows):
        return ref[...] if ref.shape[0] == 1 else ref[rows, :]

    x1s = []
    for rows in subs:
        mix = (jnp.dot(yc_ref[rows, :], wo_ref[0:dc, :], preferred_element_type=F32)
               + jnp.dot(ys_ref[rows, :].astype(BF16), wo_ref[dc:, :], preferred_element_type=F32))
        x1s.append(x_ref[rows, :] + mod(gt_ref, rows) * mix)
    for rows, x1 in zip(subs, x1s):
        _route_rows(rows, x1, mod(sc_ref, rows), mod(sh_ref, rows), g_ref, wr_ref, br_ref,
                    x1_ref, h2_ref, ti_ref, gate_ref, rank_ref, run_ref)
    cnt_ref[...] = run_ref[...]


def _route_rows(rows, x1, sc, sh, g_ref, wr_ref, br_ref, x1_ref, h2_ref, ti_ref, gate_ref, rank_ref,
                run_ref):
    x1_ref[rows, :] = x1
    ms = jnp.mean(x1 * x1, axis=-1, keepdims=True)
    h = x1 * lax.rsqrt(ms + EPS) * g_ref[...]
    hb = (h * (1.0 + sc) + sh).astype(BF16)
    hb32 = hb.astype(F32)
    half = h2_ref.shape[1]
    h2_ref[rows, :] = _pack_bf16_pair(hb32[:, :half], hb32[:, half:])
    logits = jnp.dot(hb, wr_ref[...], preferred_element_type=F32) + br_ref[...]
    tm, ne = logits.shape
    lane = lax.broadcasted_iota(I32, logits.shape, 1).astype(F32)
    work = logits
    vals, ids, sels = [], [], []
    for _ in range(TOP_K):
        m = jnp.max(work, axis=1, keepdims=True)
        idx = jnp.min(jnp.where(work == m, lane, float(ne)), axis=1, keepdims=True)
        sel = lane == idx
        vals.append(m)
        ids.append(idx)
        sels.append(sel)
        work = jnp.where(sel, -jnp.inf, work)
    exps = [jnp.exp(v - vals[0]) for v in vals]
    tot = exps[0]
    for ex in exps[1:]:
        tot = tot + ex
    gates = [ex / tot for ex in exps]
    onehot = sels[0]
    for s in sels[1:]:
        onehot = onehot | s
    onehot = onehot.astype(F32)
    row = lax.broadcasted_iota(I32, (tm, tm), 0)
    col = lax.broadcasted_iota(I32, (tm, tm), 1)
    below = (col < row).astype(BF16)
    before = jnp.dot(below, onehot.astype(BF16), preferred_element_type=F32) + run_ref[...]
    ranks = [jnp.sum(jnp.where(s, before, 0.0), axis=1, keepdims=True) for s in sels]
    run_ref[...] = run_ref[...] + jnp.sum(onehot, axis=0, keepdims=True)

    wide = lax.broadcasted_iota(I32, (tm, ti_ref.shape[1]), 1)

    def spread(cols):
        out = cols[TOP_K - 1]
        for k in range(TOP_K - 2, -1, -1):
            out = jnp.where(wide == k, cols[k], out)
        return out

    ti_ref[rows, :] = spread(ids).astype(I32)
    gate_ref[rows, :] = spread(gates)
    rank_ref[rows, :] = spread(ranks).astype(I32)


def _outproj(cnt0, x, yc, ys, wo_bf, mod, per_row, rows_per_batch, g, wr_bf, br, h2_prev, n_all,
             row0, tm):
    n, d = x.shape
    dc = yc.shape[1]
    ne = wr_bf.shape[1]
    blk0 = row0 // tm
    row = lambda w: pl.BlockSpec((tm, w), lambda i: (i, 0))
    const = lambda s: pl.BlockSpec(s, lambda i: (0, 0))
    has_prev = h2_prev is not None
    in_specs = [const((1, ne)), row(d), row(dc), row(dc),
                _resident((d, d), lambda i: (0, 0)),
                _mod_spec(per_row, tm, rows_per_batch, d, 2), const((1, d)),
                _mod_spec(per_row, tm, rows_per_batch, d, 4),
                _mod_spec(per_row, tm, rows_per_batch, d, 3),
                const((d, ne)), const((1, ne))]
    args = [cnt0, x, yc, ys, wo_bf, mod, g.reshape(1, d), mod, mod, wr_bf, br.reshape(1, ne)]
    if has_prev:
        in_specs.append(pl.BlockSpec(memory_space=pl.ANY))
        args.append(h2_prev)
    return pl.pallas_call(
        functools.partial(_outproj_body, has_prev=has_prev),
        grid=(n // tm,),
        in_specs=in_specs,
        out_specs=[row(d), pl.BlockSpec((tm, d // 2), lambda i: (blk0 + i, 0)),
                   row(128), row(128), row(128), const((1, ne))],
        out_shape=[jax.ShapeDtypeStruct((n, d), F32),
                   jax.ShapeDtypeStruct((n_all, d // 2), jnp.uint32),
                   jax.ShapeDtypeStruct((n, 128), I32), jax.ShapeDtypeStruct((n, 128), F32),
                   jax.ShapeDtypeStruct((n, 128), I32), jax.ShapeDtypeStruct((1, ne), F32)],
        scratch_shapes=[pltpu.VMEM((1, ne), F32)],
        input_output_aliases={len(args) - 1: 1} if has_prev else {},
        compiler_params=_cparams(1),
        name="outproj_router",
    )(*args)


def _moe_body(be_ref, rb_ref, nch_ref, nrow_ref, tok0_ref, tokn_ref, h2_hbm,
              w1g_ref, w1l_ref, b1g_ref, b1l_ref, w2_ref, b2_ref,
              o_ref, xp, xb, act, sem, *, nf1):
    del be_ref, rb_ref
    b = pl.program_id(0)
    s = pl.program_id(1)
    n_chunks = nch_ref[b]

    def issue_rows(tok_ref, n_rows):
        def body(i, carry):
            r0 = pl.multiple_of(i * 8, 8)
            for u in range(8):
                pltpu.make_async_copy(h2_hbm.at[pl.ds(tok_ref[0, r0 + u], 1), :],
                                      xp.at[pl.ds(r0 + u, 1), :], sem.at[0]).start()
            return carry

        lax.fori_loop(0, n_rows // 8, body, 0)

    def wait_rows(n_rows):
        size = 8
        while size <= MOE_TB:
            @pl.when((n_rows & size) != 0)
            def _(size=size):
                pltpu.make_async_copy(h2_hbm.at[pl.ds(0, size), :], xp.at[pl.ds(0, size), :],
                                      sem.at[0]).wait()

            size *= 2

    @pl.when((b == 0) & (s == 0))
    def _():
        xp[...] = jnp.zeros_like(xp)
        issue_rows(tok0_ref, nrow_ref[0])

    @pl.when(s == 0)
    def _():
        wait_rows(nrow_ref[b])
        half = xp.shape[1]

        def unpack(i, carry):
            r0 = pl.multiple_of(i * MOE_CH, MOE_CH)
            xb[pl.ds(r0, MOE_CH), :half], xb[pl.ds(r0, MOE_CH), half:] = _unpack_bf16_pair(
                xp[pl.ds(r0, MOE_CH), :])
            return carry

        lax.fori_loop(0, n_chunks, unpack, 0)
        issue_rows(tokn_ref, nrow_ref[b + 1])

    def chunk_groups(one):
        assert MOE_TB // MOE_CH < 8
        for size in (4, 2, 1):
            @pl.when((n_chunks & size) != 0)
            def _(size=size):
                r0 = (n_chunks & ~(2 * size - 1)) * MOE_CH
                one(pl.multiple_of(r0, size * MOE_CH), size * MOE_CH)

    @pl.when((s < nf1) & (n_chunks > 0))
    def _():
        def one(r0, rows):
            x = xb[pl.ds(r0, rows), :]
            hg = jnp.dot(x, w1g_ref[...].astype(BF16), preferred_element_type=F32)
            hl = jnp.dot(x, w1l_ref[...].astype(BF16), preferred_element_type=F32)
            gl = jnp.minimum(hg + b1g_ref[...], SWIGLU_LIMIT)
            ln = jnp.clip(hl + b1l_ref[...], -SWIGLU_LIMIT, SWIGLU_LIMIT)
            a = gl * _sigmoid(SWIGLU_ALPHA * gl) * (ln + 1.0)
            act[s, pl.ds(r0, rows), :] = a.astype(BF16)

        chunk_groups(one)

    @pl.when((s >= nf1) & (n_chunks > 0))
    def _():
        def one(r0, rows):
            a = jnp.concatenate([act[k, pl.ds(r0, rows), :] for k in range(nf1)], axis=1)
            o_ref[pl.ds(r0, rows), :] = (
                jnp.dot(a, w2_ref[...].astype(BF16), preferred_element_type=F32) + b2_ref[...])

        chunk_groups(one)

        def zero(i, carry):
            r0 = pl.multiple_of(i * MOE_CH, MOE_CH)
            o_ref[pl.ds(r0, MOE_CH), :] = jnp.zeros((MOE_CH, o_ref.shape[1]), F32)
            return carry

        lax.fori_loop(n_chunks, MOE_TB // MOE_CH, zero, 0)


def _moe_experts(h2_all, slot_tok, w1, b1, w2, b2, blk_expert, blk_rows, blk_chunks, blk_nrow,
                 n_grid_blocks):
    ne, d, f2 = w1.shape
    f = f2 // 2
    nf1 = f // MOE_TF
    nf2 = d // MOE_TN
    nb = slot_tok.shape[0]

    def s1(b, s, nch):
        return jnp.where(nch[b] > 0, jnp.minimum(s, nf1 - 1), nf1 - 1)

    def s2(b, s, nch):
        return jnp.where(nch[b] > 0, jnp.maximum(s - nf1, 0), nf2 - 1)

    grid_spec = pltpu.PrefetchScalarGridSpec(
        num_scalar_prefetch=4,
        grid=(n_grid_blocks, nf1 + nf2),
        in_specs=[
            pl.BlockSpec((None, 1, MOE_TB), lambda b, s, be, rb, nch, nr: (0, 0, 0),
                         memory_space=pltpu.SMEM),
            pl.BlockSpec((None, 1, MOE_TB),
                         lambda b, s, be, rb, nch, nr: (jnp.minimum(b + 1, nb - 1), 0, 0),
                         memory_space=pltpu.SMEM),
            pl.BlockSpec(memory_space=pl.ANY),
            pl.BlockSpec((None, d, MOE_TF), lambda b, s, be, rb, nch, nr: (be[b], 0, s1(b, s, nch))),
            pl.BlockSpec((None, d, MOE_TF),
                         lambda b, s, be, rb, nch, nr: (be[b], 0, nf1 + s1(b, s, nch))),
            pl.BlockSpec((None, 1, MOE_TF), lambda b, s, be, rb, nch, nr: (be[b], 0, s1(b, s, nch))),
            pl.BlockSpec((None, 1, MOE_TF),
                         lambda b, s, be, rb, nch, nr: (be[b], 0, nf1 + s1(b, s, nch))),
            pl.BlockSpec((None, f, MOE_TN), lambda b, s, be, rb, nch, nr: (be[b], 0, s2(b, s, nch))),
            pl.BlockSpec((None, 1, MOE_TN), lambda b, s, be, rb, nch, nr: (be[b], 0, s2(b, s, nch))),
        ],
        out_specs=pl.BlockSpec((MOE_TB, MOE_TN),
                               lambda b, s, be, rb, nch, nr: (rb[b], s2(b, s, nch))),
        scratch_shapes=[pltpu.VMEM((MOE_TB, d // 2), jnp.uint32), pltpu.VMEM((MOE_TB, d), BF16),
                        pltpu.VMEM((nf1, MOE_TB, MOE_TF), BF16),
                        pltpu.SemaphoreType.DMA((1,))],
    )
    return pl.pallas_call(
        functools.partial(_moe_body, nf1=nf1),
        grid_spec=grid_spec,
        out_shape=jax.ShapeDtypeStruct((nb * MOE_TB, d), F32),
        compiler_params=_cparams(2),
        name="moe_experts",
    )(blk_expert, blk_rows, blk_chunks, blk_nrow, slot_tok, slot_tok, h2_all,
      w1, w1, b1.reshape(ne, 1, f2), b1.reshape(ne, 1, f2), w2, b2.reshape(ne, 1, d))


def _moe_plan(counts, top_i, rank, n_tokens, n_blocks):
    ne = counts.shape[0]
    nblk = (counts + MOE_TB - 1) // MOE_TB
    blk_end = jnp.cumsum(nblk)
    blk_start = blk_end - nblk
    n_used = blk_end[-1]
    dest = blk_start[top_i] * MOE_TB + rank
    bidx = jnp.arange(n_blocks + 1, dtype=I32)
    be = jnp.minimum(jnp.searchsorted(blk_end, bidx, side='right'), ne - 1).astype(I32)
    active = bidx < n_used
    valid = jnp.where(active, jnp.clip(counts[be] - (bidx - blk_start[be]) * MOE_TB, 0, MOE_TB), 0)
    chunks = ((valid + MOE_CH - 1) // MOE_CH).astype(I32)
    nrow = ((valid + 7) // 8 * 8).astype(I32)
    last = jnp.maximum(n_used - 1, 0)
    be = jnp.where(active, be, be[last]).astype(I32)
    rows = jnp.where(active, bidx, last).astype(I32)
    tok = jnp.arange(n_tokens * TOP_K, dtype=I32) // TOP_K
    slot_tok = jnp.zeros((n_blocks * MOE_TB,), I32).at[dest.reshape(-1)].set(tok)
    return dest, slot_tok.reshape(n_blocks, 1, MOE_TB), be, rows, chunks, nrow, n_used


def _combine_body(d0_ref, dn_ref, x1_ref, gate_ref, gt_ref, gf_ref, ys_hbm, o_ref, ybuf, sem):
    i = pl.program_id(0)
    tm = x1_ref.shape[0]
    slot = lax.rem(i, 2)

    def issue_rows(dest_ref, sl):
        def body(i, carry):
            r0 = pl.multiple_of(i * 8, 8)
            for u in range(8):
                for k in range(TOP_K):
                    pltpu.make_async_copy(
                        ys_hbm.at[pl.ds(dest_ref[0, (r0 + u) * TOP_K + k], 1), :],
                        ybuf.at[sl, k, pl.ds(r0 + u, 1), :], sem.at[sl]).start()
            return carry

        lax.fori_loop(0, tm // 8, body, 0)

    @pl.when(i == 0)
    def _():
        issue_rows(d0_ref, 0)

    @pl.when(i + 1 < pl.num_programs(0))
    def _():
        issue_rows(dn_ref, 1 - slot)

    pltpu.make_async_copy(ybuf.at[slot], ybuf.at[slot], sem.at[slot]).wait()
    gates = gate_ref[...]
    y = gates[:, 0:1] * ybuf[slot, 0]
    for k in range(1, TOP_K):
        y = y + gates[:, k:k + 1] * ybuf[slot, k]
    x2 = x1_ref[...] + gt_ref[...] * y
    ms = jnp.mean(x2 * x2, axis=-1, keepdims=True)
    o_ref[...] = x2 * lax.rsqrt(ms + EPS) * gf_ref[...]


def _combine(x1, dest, ys, gates, mod, per_row, rows_per_batch, g_final, tm):
    n, d = x1.shape
    nt = n // tm
    dest3 = dest.reshape(nt, 1, tm * TOP_K)
    return pl.pallas_call(
        _combine_body,
        grid=(nt,),
        in_specs=[pl.BlockSpec((None, 1, tm * TOP_K), lambda i: (0, 0, 0), memory_space=pltpu.SMEM),
                  pl.BlockSpec((None, 1, tm * TOP_K), lambda i: (jnp.minimum(i + 1, nt - 1), 0, 0),
                               memory_space=pltpu.SMEM),
                  pl.BlockSpec((tm, d), lambda i: (i, 0)),
                  pl.BlockSpec((tm, 128), lambda i: (i, 0)),
                  _mod_spec(per_row, tm, rows_per_batch, d, 5),
                  pl.BlockSpec((1, d), lambda i: (0, 0)),
                  pl.BlockSpec(memory_space=pl.ANY)],
        out_specs=pl.BlockSpec((tm, d), lambda i: (i, 0)),
        out_shape=jax.ShapeDtypeStruct((n, d), F32),
        scratch_shapes=[pltpu.VMEM((2, TOP_K, tm, d), F32), pltpu.SemaphoreType.DMA((2,))],
        compiler_params=_cparams(1),
        name="combine_norm",
    )(dest3, dest3, x1, gates, mod, g_final.reshape(1, d), ys)


def kernel(x_prompt, x_sample, c_prompt, c_sample, state_conv, state_ssm_re, state_ssm_im, g_mix, g_ffn, w_mod, b_mod, w_in, conv_w, ssm_a_re, ssm_a_im, ssm_log_dt, ssm_b_re, ssm_b_im, ssm_c_re, ssm_c_im, ssm_d, glu_w, glu_b, w_out, w_router, b_router, w1, b1, w2, b2, g_final):
    bp, lp, d = x_prompt.shape
    bs, ls, _ = x_sample.shape
    depth = g_mix.shape[0]
    dc = conv_w.shape[-1]
    n_grp, p_st, ch = ssm_b_re.shape[1:]
    ne = w_router.shape[-1]
    np_, ns_ = bp * lp, bs * ls
    n_tok = np_ + ns_
    assert ls == S5_T // 2 and lp % S5_T == 0 and conv_w.shape[1] == 3
    nc = lp // S5_T
    n_blocks = -(-(n_tok * TOP_K) // MOE_TB) + ne

    xp = x_prompt.reshape(np_, d)
    xs = x_sample.reshape(ns_, d)
    c_all = jnp.concatenate([c_prompt, c_sample], axis=0)
    pad = (-c_all.shape[0]) % 8
    c_all = jnp.pad(c_all, ((0, pad), (0, 0)))

    assert depth == 1
    outs = [[] for _ in range(6)]
    for l in range(depth):
        m = _modulation(c_all, w_mod[l], b_mod[l])
        mod_p = m[:bp].reshape(bp, 1, N_MOD * d)
        mod_s = jnp.repeat(m[bp:bp + bs], ls, axis=0)

        w_in_bf = w_in[l].astype(BF16)
        proj_p = _inproj(xp, mod_p, False, lp, g_mix[l], w_in_bf, 512)
        proj_s = _inproj(xs, mod_s, True, ls, g_mix[l], w_in_bf, 256)

        yc_p, tail_p = _conv_prompt(proj_p, conv_w[l], bp, lp, dc, 512)
        e_s = jnp.pad(state_conv[l], ((0, 0), (0, ls - 2), (0, 0))).reshape(ns_, dc)
        yc_s, z_s = _conv_sample(proj_s, e_s, conv_w[l], dc, 256)
        new_conv_p = tail_p[:, 6:8, :]
        new_conv_s = z_s.reshape(bs, ls, dc)[:, ls - 2:, :]

        ops = _s5_operators(ssm_a_re[l], ssm_a_im[l], ssm_log_dt[l], ssm_b_re[l], ssm_b_im[l],
                            ssm_c_re[l], ssm_c_im[l], ssm_d[l], glu_w[l], glu_b[l])
        w8 = ls * ch
        ucol0 = (3 * dc) // 128
        ys_p, hf_p = _s5_prompt(proj_p, ucol0, ops, bp, nc)
        new_re_p = hf_p[:, :, :p_st].transpose(1, 0, 2)
        new_im_p = hf_p[:, :, p_st:].transpose(1, 0, 2)

        h0 = jnp.concatenate([state_ssm_re[l], state_ssm_im[l]], axis=-1).transpose(1, 0, 2)
        ops_s = dict(ops, wc=ops['wc'][:, :, :w8], d=ops['d'][:, :, :w8], gb=ops['gb'][:, :, :w8],
                     p1=ops['p1_half'], p2=ops['p2_half'])
        ys_s, hf_s = _s5_sample(proj_s, ucol0, h0, ops_s, ls)
        new_re_s = hf_s[:, :, :p_st].transpose(1, 0, 2)
        new_im_s = hf_s[:, :, p_st:].transpose(1, 0, 2)

        wo_bf = w_out[l].astype(BF16)
        wr_bf = w_router[l].astype(BF16)
        cnt0 = jnp.zeros((1, ne), F32)
        x1_p, h2_all, ti_p, gate_p, rank_p, cnt1 = _outproj(
            cnt0, xp, yc_p, ys_p, wo_bf, mod_p, False, lp, g_ffn[l], wr_bf, b_router[l],
            None, n_tok, 0, 2 * OUT_SUB)
        x1_s, h2_all, ti_s, gate_s, rank_s, cnt2 = _outproj(
            cnt1, xs, yc_s, ys_s, wo_bf, mod_s, True, ls, g_ffn[l], wr_bf, b_router[l],
            h2_all, n_tok, np_, OUT_SUB)

        counts = cnt2[0].astype(I32)
        top_i = jnp.concatenate([ti_p[:, :TOP_K], ti_s[:, :TOP_K]], axis=0)
        rank = jnp.concatenate([rank_p[:, :TOP_K], rank_s[:, :TOP_K]], axis=0)
        dest, slot_tok, blk_e, blk_rows, blk_chunks, blk_nrow, n_used = _moe_plan(
            counts, top_i, rank, n_tok, n_blocks)
        y_sorted = _moe_experts(h2_all, slot_tok, w1[l], b1[l], w2[l], b2[l],
                                blk_e, blk_rows, blk_chunks, blk_nrow, n_used)

        xp = _combine(x1_p, dest[:np_], y_sorted, gate_p, mod_p, False, lp, g_final, 128)
        xs = _combine(x1_s, dest[np_:], y_sorted, gate_s, mod_s, True, ls, g_final, 128)
        for lst, val in zip(outs, (new_conv_p, new_re_p, new_im_p, new_conv_s, new_re_s, new_im_s)):
            lst.append(val)

    y_prompt = xp.reshape(bp, lp, d)
    y_sample = xs.reshape(bs, ls, d)
    return (y_prompt, y_sample) + tuple(jnp.stack(o) for o in outs)
```

```python
import functools
import math

import jax
import jax.numpy as jnp
from jax import lax
from jax.experimental import pallas as pl
from jax.experimental.pallas import tpu as pltpu

F32 = jnp.float32
BF16 = jnp.bfloat16
I32 = jnp.int32

LANES = 128
EPS = 1e-6
N_MOD = 6
TOP_K = 4
SWIGLU_LIMIT = 7.0
SWIGLU_ALPHA = 1.702
GELU_C = math.sqrt(2.0 / math.pi)

S5_T = 16
OUT_SUB = 256
MOE_TB = 1536
MOE_CH = 256
MOE_TF = 512
MOE_TN = 512
VMEM_LIMIT = 56 * 1024 * 1024
HI = lax.Precision.HIGHEST


def _cparams(n_axes):
    return pltpu.CompilerParams(dimension_semantics=("arbitrary",) * n_axes,
                                vmem_limit_bytes=VMEM_LIMIT)


def _sigmoid(x):
    return 1.0 / (1.0 + jnp.exp(-x))


def _pack_bf16_pair(lo, hi):
    lo_bits = lax.bitcast_convert_type(lo, jnp.uint32) >> 16
    hi_bits = lax.bitcast_convert_type(hi, jnp.uint32) & jnp.uint32(0xFFFF0000)
    return lo_bits | hi_bits


def _unpack_bf16_pair(packed):
    lo = lax.bitcast_convert_type(packed << 16, F32)
    hi = lax.bitcast_convert_type(packed & jnp.uint32(0xFFFF0000), F32)
    return lo.astype(BF16), hi.astype(BF16)


def _resident(shape, index_map):
    return pl.BlockSpec(shape, index_map, pipeline_mode=pl.Buffered(1))


def _mod_body(c_ref, w_ref, b_ref, o_ref):
    c = c_ref[...]
    s = c * _sigmoid(c)
    o_ref[...] = jnp.dot(s.astype(BF16), w_ref[...].astype(BF16),
                         preferred_element_type=F32) + b_ref[...]


def _modulation(c_all, w_mod, b_mod):
    rows, d = c_all.shape
    n = w_mod.shape[1]
    tn = 1024
    return pl.pallas_call(
        _mod_body,
        grid=(n // tn,),
        in_specs=[pl.BlockSpec((rows, d), lambda j: (0, 0)),
                  pl.BlockSpec((d, tn), lambda j: (0, j)),
                  pl.BlockSpec((1, tn), lambda j: (0, j))],
        out_specs=pl.BlockSpec((rows, tn), lambda j: (0, j)),
        out_shape=jax.ShapeDtypeStruct((rows, n), F32),
        compiler_params=_cparams(1),
        name="modulation",
    )(c_all, w_mod, b_mod.reshape(1, n))


def _mod_spec(per_row, tm, rows_per_batch, d, col):
    if per_row:
        return pl.BlockSpec((tm, d), lambda i: (i, col))
    return pl.BlockSpec((None, 1, d), lambda i: ((i * tm) // rows_per_batch, 0, col))


def _inproj_pieces(x_ref, g_ref, sc_ref, sh_ref, w_ref, dc):
    x = x_ref[...]
    ms = jnp.mean(x * x, axis=-1, keepdims=True)
    h = x * lax.rsqrt(ms + EPS) * g_ref[...]
    hb = (h * (1.0 + sc_ref[...]) + sh_ref[...]).astype(BF16)
    b, c, v = (jnp.dot(hb, w_ref[:, k * dc:(k + 1) * dc], preferred_element_type=F32)
               for k in range(3))
    u = jnp.dot(hb, w_ref[:, 3 * dc:], preferred_element_type=F32)
    return b, c * v, u


def _conv_taps(bgate, z, z1, z2, w_ref):
    w = w_ref[...]
    return bgate * (w[0:1] * z2 + w[1:2] * z1 + w[2:3] * z)


def _inproj_conv_prompt_body(x_ref, g_ref, sc_ref, sh_ref, w_ref, cw_ref, y_ref, u_ref, tail_ref,
                             carry_ref, *, tiles_per_seq):
    @pl.when(pl.program_id(0) % tiles_per_seq == 0)
    def _():
        carry_ref[...] = jnp.zeros_like(carry_ref)

    bgate, z, u = _inproj_pieces(x_ref, g_ref, sc_ref, sh_ref, w_ref, y_ref.shape[1])
    u_ref[...] = u
    tt = z.shape[0]
    zc = jnp.concatenate([carry_ref[...], z], axis=0)
    z1 = pltpu.roll(zc, 1, 0)[8:]
    z2 = pltpu.roll(zc, 2, 0)[8:]
    y_ref[...] = _conv_taps(bgate, z, z1, z2, cw_ref).astype(y_ref.dtype)
    carry_ref[...] = z[tt - 8:]
    tail_ref[...] = z[tt - 8:]


def _inproj_conv_prompt(x, mod, seq, g, w_bf, conv_w, tm):
    n, d = x.shape
    dc = conv_w.shape[1]
    du = w_bf.shape[1] - 3 * dc
    row = lambda w: pl.BlockSpec((tm, w), lambda i: (i, 0))
    return pl.pallas_call(
        functools.partial(_inproj_conv_prompt_body, tiles_per_seq=seq // tm),
        grid=(n // tm,),
        in_specs=[row(d), pl.BlockSpec((1, d), lambda i: (0, 0)),
                  _mod_spec(False, tm, seq, d, 1), _mod_spec(False, tm, seq, d, 0),
                  _resident(w_bf.shape, lambda i: (0, 0)),
                  pl.BlockSpec(conv_w.shape, lambda i: (0, 0))],
        out_specs=[row(dc), row(du), pl.BlockSpec((None, 8, dc), lambda i: ((i * tm) // seq, 0, 0))],
        out_shape=[jax.ShapeDtypeStruct((n, dc), BF16), jax.ShapeDtypeStruct((n, du), F32),
                   jax.ShapeDtypeStruct((n // seq, 8, dc), F32)],
        scratch_shapes=[pltpu.VMEM((8, dc), F32)],
        compiler_params=_cparams(1),
        name="inproj_conv_prompt",
    )(x, g.reshape(1, d), mod, mod, w_bf, conv_w)


def _inproj_conv_sample_body(x_ref, g_ref, sc_ref, sh_ref, w_ref, cw_ref, e_ref, y_ref, u_ref, z_ref):
    bgate, z, u = _inproj_pieces(x_ref, g_ref, sc_ref, sh_ref, w_ref, y_ref.shape[1])
    u_ref[...] = u
    e = e_ref[...]
    rows = z.shape[0]
    tpos = lax.broadcasted_iota(I32, z.shape, 0) & 7
    z1 = jnp.where(tpos == 0, pltpu.roll(e, rows - 1, 0), pltpu.roll(z, 1, 0))
    z2 = jnp.where(tpos < 2, e, pltpu.roll(z, 2, 0))
    y_ref[...] = _conv_taps(bgate, z, z1, z2, cw_ref).astype(y_ref.dtype)
    z_ref[...] = z


def _inproj_conv_sample(x, mod, g, w_bf, conv_w, e, tm):
    n, d = x.shape
    dc = conv_w.shape[1]
    du = w_bf.shape[1] - 3 * dc
    row = lambda w: pl.BlockSpec((tm, w), lambda i: (i, 0))
    return pl.pallas_call(
        _inproj_conv_sample_body,
        grid=(n // tm,),
        in_specs=[row(d), pl.BlockSpec((1, d), lambda i: (0, 0)),
                  _mod_spec(True, tm, 0, d, 1), _mod_spec(True, tm, 0, d, 0),
                  _resident(w_bf.shape, lambda i: (0, 0)),
                  pl.BlockSpec(conv_w.shape, lambda i: (0, 0)), row(dc)],
        out_specs=[row(dc), row(du), row(dc)],
        out_shape=[jax.ShapeDtypeStruct((n, dc), BF16), jax.ShapeDtypeStruct((n, du), F32),
                   jax.ShapeDtypeStruct((n, dc), F32)],
        compiler_params=_cparams(1),
        name="inproj_conv_sample",
    )(x, g.reshape(1, d), mod, mod, w_bf, conv_w, e)


def _s5_operators(a_re, a_im, log_dt, b_re, b_im, c_re, c_im, d_skip, glu_w, glu_b):
    g, p, ch = b_re.shape
    t = S5_T
    dt = jnp.exp(log_dt)[:, None]
    mag = jnp.exp(a_re * dt)
    ar = mag * jnp.cos(a_im * dt)
    ai = mag * jnp.sin(a_im * dt)
    den = a_re * a_re + a_im * a_im
    qr = ((ar - 1.0) * a_re + ai * a_im) / den
    qi = (ai * a_re - (ar - 1.0) * a_im) / den
    bbr = qr[..., None] * b_re - qi[..., None] * b_im
    bbi = qr[..., None] * b_im + qi[..., None] * b_re
    pr, pi = [jnp.ones_like(ar)], [jnp.zeros_like(ar)]
    for _ in range(t):
        pr, pi = pr + [pr[-1] * ar - pi[-1] * ai], pi + [pr[-1] * ai + pi[-1] * ar]
    pw_r, pw_i = jnp.stack(pr), jnp.stack(pi)
    car = c_re[None] * pw_r[:, :, None, :] - c_im[None] * pw_i[:, :, None, :]
    cai = c_re[None] * pw_i[:, :, None, :] + c_im[None] * pw_r[:, :, None, :]
    rev_r, rev_i = pw_r[t - 1::-1][:t], pw_i[t - 1::-1][:t]
    wsr = rev_r[..., None] * bbr[None] - rev_i[..., None] * bbi[None]
    wsi = rev_r[..., None] * bbi[None] + rev_i[..., None] * bbr[None]
    ws_op = jnp.concatenate([wsr.transpose(1, 0, 3, 2), wsi.transpose(1, 0, 3, 2)],
                            axis=-1).reshape(g, t * ch, 2 * p)
    wcr = car[1:].transpose(1, 3, 0, 2).reshape(g, p, t * ch)
    wci = cai[1:].transpose(1, 3, 0, 2).reshape(g, p, t * ch)
    wc_op = jnp.concatenate([wcr, -wci], axis=1)
    cmat_t = jnp.concatenate([c_re, -c_im], axis=-1)
    glu_t = jnp.swapaxes(glu_w, 1, 2)
    d_flat = jnp.tile(d_skip, (1, t))[:, None, :]
    gb_flat = jnp.tile(glu_b, (1, t))[:, None, :]

    def rot_tables(xr, xi):
        return jnp.concatenate([xr, xr], axis=-1), jnp.concatenate([-xi, xi], axis=-1)

    sr, si = pw_r[t], pw_i[t]
    p1s, p2s = [], []
    for _ in range(8):
        t1, t2 = rot_tables(sr, si)
        p1s.append(t1)
        p2s.append(t2)
        sr, si = sr * sr - si * si, 2.0 * sr * si
    h = t // 2
    p1_half, p2_half = rot_tables(pw_r[h], pw_i[h])
    return dict(
        ws=ws_op, wc=wc_op.astype(BF16), cmat=cmat_t, glu=glu_t,
        d=d_flat, gb=gb_flat, p1=jnp.stack(p1s, axis=1), p2=jnp.stack(p2s, axis=1),
        p1_half=p1_half[:, None, :], p2_half=p2_half[:, None, :])


def _gelu_tanh(y):
    return 0.5 * y * (1.0 + jnp.tanh(GELU_C * (y + 0.044715 * (y * y * y))))


def _dot_t(x, w_t):
    return lax.dot_general(x, w_t, (((1,), (1,)), ((), ())), preferred_element_type=F32)


def _block_toeplitz_t(seq, steps):
    ch, n = seq.shape
    lane = lax.broadcasted_iota(I32, seq.shape, 1)
    rows = []
    for t in range(steps):
        shift = (ch * (t + 1)) % n
        r = pltpu.roll(seq, shift, 1) if shift else seq
        rows.append(jnp.where(lane < ch * (t + 1), r, 0.0))
    return jnp.concatenate(rows, axis=0).astype(BF16)


def _s5_group_operators(ws, cmat, glu_t, steps=S5_T):
    n = ws.shape[0]
    seq = lax.dot_general(cmat, ws, (((1,), (1,)), ((), ())), precision=HI,
                          preferred_element_type=F32)
    ch = cmat.shape[0]
    glu_seq = jnp.concatenate([jnp.zeros((ch, n - ch), F32), glu_t], axis=1)
    return _block_toeplitz_t(seq, steps), _block_toeplitz_t(glu_seq, steps)


def _s5_tail(y, gl_t, gb):
    y = _gelu_tanh(y)
    gate = _dot_t(y.astype(BF16), gl_t) + gb
    return y * _sigmoid(gate)


S5_CH = 16
S5_OCT = LANES // S5_CH


def _fold_time(rows, blk):
    halves = [_block_transpose(rows[h:h + S5_OCT], blk) for h in range(0, len(rows), S5_OCT)]
    if len(halves) == 1:
        return halves[0]
    return [jnp.concatenate([hv[q] for hv in halves], axis=1) for q in range(S5_OCT)]


def _unfold_time(ys, blk):
    rows = []
    for h in range(ys[0].shape[1] // LANES):
        rows += _block_transpose([y[:, LANES * h:LANES * (h + 1)] for y in ys], blk)
    return rows


def _block_transpose(vs, blk):
    vs = list(vs)
    for d in (4, 2, 1):
        upper = (blk & d) != 0
        new = list(vs)
        for i in range(S5_OCT):
            if i & d:
                continue
            a, b = vs[i], vs[i + d]
            new[i] = jnp.where(upper, pltpu.roll(b, S5_CH * d, 1), a)
            new[i + d] = jnp.where(upper, b, pltpu.roll(a, LANES - S5_CH * d, 1))
        vs = new
    return vs


def _s5_prompt_body(u_ref, ws_ref, wc_ref, cm_ref, gl_ref, d_ref, gb_ref, p1_ref, p2_ref,
                    y_ref, hf_ref, uf, yf, *, bsz, nc):
    half = ws_ref.shape[-1] // 2
    seq = nc * S5_T
    blk = lax.broadcasted_iota(I32, (nc, LANES), 1) // S5_CH

    def fold(b, carry):
        rows = [u_ref[pl.ds(pl.multiple_of(b * seq, seq) + t, nc, stride=S5_T), :]
                for t in range(S5_T)]
        for q, uq in enumerate(_fold_time(rows, blk)):
            uf[q, pl.ds(pl.multiple_of(b * nc, nc), nc), :] = uq
        return carry

    lax.fori_loop(0, bsz, fold, 0)

    for g in range(S5_OCT):
        u = uf[g]
        ub = u.astype(BF16)
        m_t, glu_t = _s5_group_operators(ws_ref[g], cm_ref[g], gl_ref[g])
        yi = _dot_t(ub, m_t)
        e = jnp.dot(ub, ws_ref[g].astype(BF16), preferred_element_type=F32)
        cpos = lax.broadcasted_iota(I32, e.shape, 0) & (nc - 1)
        p1 = p1_ref[g]
        p2 = p2_ref[g]
        z = jnp.where(cpos >= 1, pltpu.roll(e, 1, 0), 0.0)
        d, k = 1, 0
        while d < nc:
            zs = jnp.where(cpos >= d, pltpu.roll(z, d, 0), 0.0)
            z = z + zs * p1[k:k + 1] + pltpu.roll(zs, half, 1) * p2[k:k + 1]
            d, k = d * 2, k + 1
        yc = jnp.dot(z.astype(BF16), wc_ref[g], preferred_element_type=F32)
        yf[g] = _s5_tail(yi + yc + d_ref[g] * u, glu_t, gb_ref[g])
        hfin = e + z * p1[0:1] + pltpu.roll(z, half, 1) * p2[0:1]
        for b in range(bsz):
            r = b * nc + nc - 1
            hf_ref[g, b:b + 1, :] = hfin[r:r + 1, :]

    def unfold(b, carry):
        ys = [yf[q, pl.ds(pl.multiple_of(b * nc, nc), nc), :] for q in range(S5_OCT)]
        for t, row in enumerate(_unfold_time(ys, blk)):
            y_ref[pl.ds(pl.multiple_of(b * seq, seq) + t, nc, stride=S5_T), :] = row
        return carry

    lax.fori_loop(0, bsz, unfold, 0)


def _s5_prompt(proj, col0, ops, bsz, nc):
    n = proj.shape[0]
    g, w, p2x = ops['ws'].shape
    ch = ops['cmat'].shape[1]
    blk = lambda s1, s2: pl.BlockSpec((S5_OCT, s1, s2), lambda i: (i, 0, 0))
    return pl.pallas_call(
        functools.partial(_s5_prompt_body, bsz=bsz, nc=nc),
        grid=(g // S5_OCT,),
        in_specs=[pl.BlockSpec((n, LANES), lambda i: (0, col0 + i)),
                  blk(w, p2x), blk(p2x, w), blk(ch, p2x), blk(ch, ch),
                  blk(1, w), blk(1, w), blk(8, p2x), blk(8, p2x)],
        out_specs=[pl.BlockSpec((n, LANES), lambda i: (0, i)), blk(bsz, p2x)],
        out_shape=[jax.ShapeDtypeStruct((n, g * S5_CH), F32),
                   jax.ShapeDtypeStruct((g, bsz, p2x), F32)],
        scratch_shapes=[pltpu.VMEM((S5_OCT, bsz * nc, w), F32),
                        pltpu.VMEM((S5_OCT, bsz * nc, w), F32)],
        compiler_params=_cparams(1),
        name="s5_prompt",
    )(proj, ops['ws'], ops['wc'], ops['cmat'], ops['glu'], ops['d'], ops['gb'], ops['p1'], ops['p2'])


def _s5_sample_body(u_ref, h0_ref, ws_ref, wc_ref, cm_ref, gl_ref, d_ref, gb_ref, p1_ref, p2_ref,
                    y_ref, hf_ref, *, steps):
    half = ws_ref.shape[-1] // 2
    nseq = u_ref.shape[0] // steps
    w = steps * cm_ref.shape[1]
    blk = lax.broadcasted_iota(I32, (nseq, LANES), 1) // S5_CH
    us = _fold_time([u_ref[pl.ds(t, nseq, stride=steps), :] for t in range(steps)], blk)
    ys = []
    for g in range(S5_OCT):
        u = us[g]
        h0 = h0_ref[g]
        ub = u.astype(BF16)
        m_t, glu_t = _s5_group_operators(ws_ref[g], cm_ref[g], gl_ref[g], steps)
        ws = ws_ref[g]
        yi = _dot_t(ub, m_t[:, :w])
        e = jnp.dot(ub, ws[ws.shape[0] - w:, :].astype(BF16), preferred_element_type=F32)
        yc = jnp.dot(h0.astype(BF16), wc_ref[g], preferred_element_type=F32)
        ys.append(_s5_tail(yi + yc + d_ref[g] * u, glu_t[:, :w], gb_ref[g]))
        hf_ref[g] = e + h0 * p1_ref[g] + pltpu.roll(h0, half, 1) * p2_ref[g]
    for t, row in enumerate(_unfold_time(ys, blk)):
        y_ref[pl.ds(t, nseq, stride=steps), :] = row


def _s5_sample(proj, col0, h0, ops, steps):
    n = proj.shape[0]
    g, r, p2x = h0.shape
    wfull = ops['ws'].shape[1]
    ch = ops['cmat'].shape[1]
    w = steps * ch
    blk = lambda s1, s2: pl.BlockSpec((S5_OCT, s1, s2), lambda i: (i, 0, 0))
    return pl.pallas_call(
        functools.partial(_s5_sample_body, steps=steps),
        grid=(g // S5_OCT,),
        in_specs=[pl.BlockSpec((n, LANES), lambda i: (0, col0 + i)),
                  blk(r, p2x), blk(wfull, p2x), blk(p2x, w), blk(ch, p2x), blk(ch, ch),
                  blk(1, w), blk(1, w), blk(1, p2x), blk(1, p2x)],
        out_specs=[pl.BlockSpec((n, LANES), lambda i: (0, i)), blk(r, p2x)],
        out_shape=[jax.ShapeDtypeStruct((n, g * S5_CH), F32),
                   jax.ShapeDtypeStruct((g, r, p2x), F32)],
        compiler_params=_cparams(1),
        name="s5_sample",
    )(proj, h0, ops['ws'], ops['wc'], ops['cmat'], ops['glu'], ops['d'], ops['gb'],
      ops['p1'], ops['p2'])


def _outproj_body(cnt0_ref, x_ref, yc_ref, ys_ref, wo_ref, gt_ref, g_ref, sc_ref, sh_ref,
                  wr_ref, br_ref, *rest, has_prev):
    x1_ref, h2_ref, ti_ref, gate_ref, rank_ref, cnt_ref, run_ref = rest[1:] if has_prev else rest

    @pl.when(pl.program_id(0) == 0)
    def _():
        run_ref[...] = cnt0_ref[...]

    subs = [slice(r0, r0 + OUT_SUB) for r0 in range(0, x_ref.shape[0], OUT_SUB)]
    dc = yc_ref.shape[1]

    def mod(ref, rows):
        return ref[...] if ref.shape[0] == 1 else ref[rows, :]

    x1s = []
    for rows in subs:
        mix = (jnp.dot(yc_ref[rows, :], wo_ref[0:dc, :], preferred_element_type=F32)
               + jnp.dot(ys_ref[rows, :].astype(BF16), wo_ref[dc:, :], preferred_element_type=F32))
        x1s.append(x_ref[rows, :] + mod(gt_ref, rows) * mix)
    for rows, x1 in zip(subs, x1s):
        _route_rows(rows, x1, mod(sc_ref, rows), mod(sh_ref, rows), g_ref, wr_ref, br_ref,
                    x1_ref, h2_ref, ti_ref, gate_ref, rank_ref, run_ref)
    cnt_ref[...] = run_ref[...]


def _route_rows(rows, x1, sc, sh, g_ref, wr_ref, br_ref, x1_ref, h2_ref, ti_ref, gate_ref, rank_ref,
                run_ref):
    x1_ref[rows, :] = x1
    ms = jnp.mean(x1 * x1, axis=-1, keepdims=True)
    h = x1 * lax.rsqrt(ms + EPS) * g_ref[...]
    hb = (h * (1.0 + sc) + sh).astype(BF16)
    hb32 = hb.astype(F32)
    half = h2_ref.shape[1]
    h2_ref[rows, :] = _pack_bf16_pair(hb32[:, :half], hb32[:, half:])
    logits = jnp.dot(hb, wr_ref[...], preferred_element_type=F32) + br_ref[...]
    tm, ne = logits.shape
    lane = lax.broadcasted_iota(I32, logits.shape, 1).astype(F32)
    work = logits
    vals, ids, sels = [], [], []
    for _ in range(TOP_K):
        m = jnp.max(work, axis=1, keepdims=True)
        idx = jnp.min(jnp.where(work == m, lane, float(ne)), axis=1, keepdims=True)
        sel = lane == idx
        vals.append(m)
        ids.append(idx)
        sels.append(sel)
        work = jnp.where(sel, -jnp.inf, work)
    exps = [jnp.exp(v - vals[0]) for v in vals]
    tot = exps[0]
    for ex in exps[1:]:
        tot = tot + ex
    gates = [ex / tot for ex in exps]
    onehot = sels[0]
    for s in sels[1:]:
        onehot = onehot | s
    onehot = onehot.astype(F32)
    row = lax.broadcasted_iota(I32, (tm, tm), 0)
    col = lax.broadcasted_iota(I32, (tm, tm), 1)
    below = (col < row).astype(BF16)
    before = jnp.dot(below, onehot.astype(BF16), preferred_element_type=F32) + run_ref[...]
    ranks = [jnp.sum(jnp.where(s, before, 0.0), axis=1, keepdims=True) for s in sels]
    run_ref[...] = run_ref[...] + jnp.sum(onehot, axis=0, keepdims=True)

    wide = lax.broadcasted_iota(I32, (tm, ti_ref.shape[1]), 1)

    def spread(cols):
        out = cols[TOP_K - 1]
        for k in range(TOP_K - 2, -1, -1):
            out = jnp.where(wide == k, cols[k], out)
        return out

    ti_ref[rows, :] = spread(ids).astype(I32)
    gate_ref[rows, :] = spread(gates)
    rank_ref[rows, :] = spread(ranks).astype(I32)


def _outproj(cnt0, x, yc, ys, wo_bf, mod, per_row, rows_per_batch, g, wr_bf, br, h2_prev, n_all,
             row0, tm):
    n, d = x.shape
    dc = yc.shape[1]
    ne = wr_bf.shape[1]
    blk0 = row0 // tm
    row = lambda w: pl.BlockSpec((tm, w), lambda i: (i, 0))
    const = lambda s: pl.BlockSpec(s, lambda i: (0, 0))
    has_prev = h2_prev is not None
    in_specs = [const((1, ne)), row(d), row(dc), row(dc),
                _resident((d, d), lambda i: (0, 0)),
                _mod_spec(per_row, tm, rows_per_batch, d, 2), const((1, d)),
                _mod_spec(per_row, tm, rows_per_batch, d, 4),
                _mod_spec(per_row, tm, rows_per_batch, d, 3),
                const((d, ne)), const((1, ne))]
    args = [cnt0, x, yc, ys, wo_bf, mod, g.reshape(1, d), mod, mod, wr_bf, br.reshape(1, ne)]
    if has_prev:
        in_specs.append(pl.BlockSpec(memory_space=pl.ANY))
        args.append(h2_prev)
    return pl.pallas_call(
        functools.partial(_outproj_body, has_prev=has_prev),
        grid=(n // tm,),
        in_specs=in_specs,
        out_specs=[row(d), pl.BlockSpec((tm, d // 2), lambda i: (blk0 + i, 0)),
                   row(128), row(128), row(128), const((1, ne))],
        out_shape=[jax.ShapeDtypeStruct((n, d), F32),
                   jax.ShapeDtypeStruct((n_all, d // 2), jnp.uint32),
                   jax.ShapeDtypeStruct((n, 128), I32), jax.ShapeDtypeStruct((n, 128), F32),
                   jax.ShapeDtypeStruct((n, 128), I32), jax.ShapeDtypeStruct((1, ne), F32)],
        scratch_shapes=[pltpu.VMEM((1, ne), F32)],
        input_output_aliases={len(args) - 1: 1} if has_prev else {},
        compiler_params=_cparams(1),
        name="outproj_router",
    )(*args)


def _moe_body(be_ref, rb_ref, nch_ref, nrow_ref, tok0_ref, tokn_ref, h2_hbm,
              w1g_ref, w1l_ref, b1g_ref, b1l_ref, w2_ref, b2_ref,
              o_ref, xp, xb, act, sem, *, nf1):
    del be_ref, rb_ref
    b = pl.program_id(0)
    s = pl.program_id(1)
    n_chunks = nch_ref[b]

    def issue_rows(tok_ref, n_rows):
        def body(i, carry):
            r0 = pl.multiple_of(i * 8, 8)
            for u in range(8):
                pltpu.make_async_copy(h2_hbm.at[pl.ds(tok_ref[0, r0 + u], 1), :],
                                      xp.at[pl.ds(r0 + u, 1), :], sem.at[0]).start()
            return carry

        lax.fori_loop(0, n_rows // 8, body, 0)

    def wait_rows(n_rows):
        size = 8
        while size <= MOE_TB:
            @pl.when((n_rows & size) != 0)
            def _(size=size):
                pltpu.make_async_copy(h2_hbm.at[pl.ds(0, size), :], xp.at[pl.ds(0, size), :],
                                      sem.at[0]).wait()

            size *= 2

    @pl.when((b == 0) & (s == 0))
    def _():
        xp[...] = jnp.zeros_like(xp)
        issue_rows(tok0_ref, nrow_ref[0])

    @pl.when(s == 0)
    def _():
        wait_rows(nrow_ref[b])
        half = xp.shape[1]

        def unpack(i, carry):
            r0 = pl.multiple_of(i * MOE_CH, MOE_CH)
            xb[pl.ds(r0, MOE_CH), :half], xb[pl.ds(r0, MOE_CH), half:] = _unpack_bf16_pair(
                xp[pl.ds(r0, MOE_CH), :])
            return carry

        lax.fori_loop(0, n_chunks, unpack, 0)
        issue_rows(tokn_ref, nrow_ref[b + 1])

    def chunk_groups(one):
        assert MOE_TB // MOE_CH < 8
        for size in (4, 2, 1):
            @pl.when((n_chunks & size) != 0)
            def _(size=size):
                r0 = (n_chunks & ~(2 * size - 1)) * MOE_CH
                one(pl.multiple_of(r0, size * MOE_CH), size * MOE_CH)

    @pl.when((s < nf1) & (n_chunks > 0))
    def _():
        def one(r0, rows):
            x = xb[pl.ds(r0, rows), :]
            hg = jnp.dot(x, w1g_ref[...].astype(BF16), preferred_element_type=F32)
            hl = jnp.dot(x, w1l_ref[...].astype(BF16), preferred_element_type=F32)
            gl = jnp.minimum(hg + b1g_ref[...], SWIGLU_LIMIT)
            ln = jnp.clip(hl + b1l_ref[...], -SWIGLU_LIMIT, SWIGLU_LIMIT)
            a = gl * _sigmoid(SWIGLU_ALPHA * gl) * (ln + 1.0)
            act[s, pl.ds(r0, rows), :] = a.astype(BF16)

        chunk_groups(one)

    @pl.when((s >= nf1) & (n_chunks > 0))
    def _():
        def one(r0, rows):
            a = jnp.concatenate([act[k, pl.ds(r0, rows), :] for k in range(nf1)], axis=1)
            o_ref[pl.ds(r0, rows), :] = (
                jnp.dot(a, w2_ref[...].astype(BF16), preferred_element_type=F32) + b2_ref[...])

        chunk_groups(one)

        def zero(i, carry):
            r0 = pl.multiple_of(i * MOE_CH, MOE_CH)
            o_ref[pl.ds(r0, MOE_CH), :] = jnp.zeros((MOE_CH, o_ref.shape[1]), F32)
            return carry

        lax.fori_loop(n_chunks, MOE_TB // MOE_CH, zero, 0)


def _moe_experts(h2_all, slot_tok, w1, b1, w2, b2, blk_expert, blk_rows, blk_chunks, blk_nrow,
                 n_grid_blocks):
    ne, d, f2 = w1.shape
    f = f2 // 2
    nf1 = f // MOE_TF
    nf2 = d // MOE_TN
    nb = slot_tok.shape[0]

    def s1(b, s, nch):
        return jnp.where(nch[b] > 0, jnp.minimum(s, nf1 - 1), nf1 - 1)

    def s2(b, s, nch):
        return jnp.where(nch[b] > 0, jnp.maximum(s - nf1, 0), nf2 - 1)

    def w2_index(b, s, be, rb, nch, nr):
        early = (s < nf1 - 1) & (b > 0)
        e = jnp.where(early, be[jnp.maximum(b - 1, 0)], be[b])
        return e, 0, jnp.where(early, nf2 - 1, s2(b, s, nch))

    grid_spec = pltpu.PrefetchScalarGridSpec(
        num_scalar_prefetch=4,
        grid=(n_grid_blocks, nf1 + nf2),
        in_specs=[
            pl.BlockSpec((None, 1, MOE_TB), lambda b, s, be, rb, nch, nr: (0, 0, 0),
                         memory_space=pltpu.SMEM),
            pl.BlockSpec((None, 1, MOE_TB),
                         lambda b, s, be, rb, nch, nr: (jnp.minimum(b + 1, nb - 1), 0, 0),
                         memory_space=pltpu.SMEM),
            pl.BlockSpec(memory_space=pl.ANY),
            pl.BlockSpec((None, d, MOE_TF), lambda b, s, be, rb, nch, nr: (be[b], 0, s1(b, s, nch))),
            pl.BlockSpec((None, d, MOE_TF),
                         lambda b, s, be, rb, nch, nr: (be[b], 0, nf1 + s1(b, s, nch))),
            pl.BlockSpec((None, 1, MOE_TF), lambda b, s, be, rb, nch, nr: (be[b], 0, s1(b, s, nch))),
            pl.BlockSpec((None, 1, MOE_TF),
                         lambda b, s, be, rb, nch, nr: (be[b], 0, nf1 + s1(b, s, nch))),
            pl.BlockSpec((None, f, MOE_TN), w2_index),
            pl.BlockSpec((None, 1, MOE_TN), w2_index),
        ],
        out_specs=pl.BlockSpec((MOE_TB, MOE_TN),
                               lambda b, s, be, rb, nch, nr: (rb[b], s2(b, s, nch))),
        scratch_shapes=[pltpu.VMEM((MOE_TB, d // 2), jnp.uint32), pltpu.VMEM((MOE_TB, d), BF16),
                        pltpu.VMEM((nf1, MOE_TB, MOE_TF), BF16),
                        pltpu.SemaphoreType.DMA((1,))],
    )
    return pl.pallas_call(
        functools.partial(_moe_body, nf1=nf1),
        grid_spec=grid_spec,
        out_shape=jax.ShapeDtypeStruct((nb * MOE_TB, d), F32),
        compiler_params=_cparams(2),
        name="moe_experts",
    )(blk_expert, blk_rows, blk_chunks, blk_nrow, slot_tok, slot_tok, h2_all,
      w1, w1, b1.reshape(ne, 1, f2), b1.reshape(ne, 1, f2), w2, b2.reshape(ne, 1, d))


def _moe_plan(counts, top_i, rank, n_tokens, n_blocks):
    ne = counts.shape[0]
    nblk = (counts + MOE_TB - 1) // MOE_TB
    blk_end = jnp.cumsum(nblk)
    blk_start = blk_end - nblk
    n_used = blk_end[-1]
    dest = blk_start[top_i] * MOE_TB + rank
    bidx = jnp.arange(n_blocks + 1, dtype=I32)
    be = jnp.minimum(jnp.searchsorted(blk_end, bidx, side='right'), ne - 1).astype(I32)
    active = bidx < n_used
    valid = jnp.where(active, jnp.clip(counts[be] - (bidx - blk_start[be]) * MOE_TB, 0, MOE_TB), 0)
    chunks = ((valid + MOE_CH - 1) // MOE_CH).astype(I32)
    nrow = ((valid + 7) // 8 * 8).astype(I32)
    last = jnp.maximum(n_used - 1, 0)
    be = jnp.where(active, be, be[last]).astype(I32)
    rows = jnp.where(active, bidx, last).astype(I32)
    tok = jnp.arange(n_tokens * TOP_K, dtype=I32) // TOP_K
    slot_tok = jnp.zeros((n_blocks * MOE_TB,), I32).at[dest.reshape(-1)].set(tok)
    return dest, slot_tok.reshape(n_blocks, 1, MOE_TB), be, rows, chunks, nrow, n_used


def _combine_body(d0_ref, dn_ref, x1_ref, gate_ref, gt_ref, gf_ref, ys_hbm, o_ref, ybuf, sem):
    i = pl.program_id(0)
    tm = x1_ref.shape[0]
    slot = lax.rem(i, 2)

    def issue_rows(dest_ref, sl):
        def body(i, carry):
            r0 = pl.multiple_of(i * 8, 8)
            for u in range(8):
                for k in range(TOP_K):
                    pltpu.make_async_copy(
                        ys_hbm.at[pl.ds(dest_ref[0, (r0 + u) * TOP_K + k], 1), :],
                        ybuf.at[sl, k, pl.ds(r0 + u, 1), :], sem.at[sl]).start()
            return carry

        lax.fori_loop(0, tm // 8, body, 0)

    @pl.when(i == 0)
    def _():
        issue_rows(d0_ref, 0)

    @pl.when(i + 1 < pl.num_programs(0))
    def _():
        issue_rows(dn_ref, 1 - slot)

    pltpu.make_async_copy(ybuf.at[slot], ybuf.at[slot], sem.at[slot]).wait()
    gates = gate_ref[...]
    y = gates[:, 0:1] * ybuf[slot, 0]
    for k in range(1, TOP_K):
        y = y + gates[:, k:k + 1] * ybuf[slot, k]
    x2 = x1_ref[...] + gt_ref[...] * y
    ms = jnp.mean(x2 * x2, axis=-1, keepdims=True)
    o_ref[...] = x2 * lax.rsqrt(ms + EPS) * gf_ref[...]


def _combine(x1, dest, ys, gates, mod, per_row, rows_per_batch, g_final, tm):
    n, d = x1.shape
    nt = n // tm
    dest3 = dest.reshape(nt, 1, tm * TOP_K)
    return pl.pallas_call(
        _combine_body,
        grid=(nt,),
        in_specs=[pl.BlockSpec((None, 1, tm * TOP_K), lambda i: (0, 0, 0), memory_space=pltpu.SMEM),
                  pl.BlockSpec((None, 1, tm * TOP_K), lambda i: (jnp.minimum(i + 1, nt - 1), 0, 0),
                               memory_space=pltpu.SMEM),
                  pl.BlockSpec((tm, d), lambda i: (i, 0)),
                  pl.BlockSpec((tm, 128), lambda i: (i, 0)),
                  _mod_spec(per_row, tm, rows_per_batch, d, 5),
                  pl.BlockSpec((1, d), lambda i: (0, 0)),
                  pl.BlockSpec(memory_space=pl.ANY)],
        out_specs=pl.BlockSpec((tm, d), lambda i: (i, 0)),
        out_shape=jax.ShapeDtypeStruct((n, d), F32),
        scratch_shapes=[pltpu.VMEM((2, TOP_K, tm, d), F32), pltpu.SemaphoreType.DMA((2,))],
        compiler_params=_cparams(1),
        name="combine_norm",
    )(dest3, dest3, x1, gates, mod, g_final.reshape(1, d), ys)


def kernel(x_prompt, x_sample, c_prompt, c_sample, state_conv, state_ssm_re, state_ssm_im, g_mix, g_ffn, w_mod, b_mod, w_in, conv_w, ssm_a_re, ssm_a_im, ssm_log_dt, ssm_b_re, ssm_b_im, ssm_c_re, ssm_c_im, ssm_d, glu_w, glu_b, w_out, w_router, b_router, w1, b1, w2, b2, g_final):
    bp, lp, d = x_prompt.shape
    bs, ls, _ = x_sample.shape
    depth = g_mix.shape[0]
    dc = conv_w.shape[-1]
    n_grp, p_st, ch = ssm_b_re.shape[1:]
    ne = w_router.shape[-1]
    np_, ns_ = bp * lp, bs * ls
    n_tok = np_ + ns_
    assert ls == S5_T // 2 and lp % S5_T == 0 and conv_w.shape[1] == 3
    nc = lp // S5_T
    n_blocks = -(-(n_tok * TOP_K) // MOE_TB) + ne

    xp = x_prompt.reshape(np_, d)
    xs = x_sample.reshape(ns_, d)
    c_all = jnp.concatenate([c_prompt, c_sample], axis=0)
    pad = (-c_all.shape[0]) % 8
    c_all = jnp.pad(c_all, ((0, pad), (0, 0)))

    assert depth == 1
    outs = [[] for _ in range(6)]
    for l in range(depth):
        m = _modulation(c_all, w_mod[l], b_mod[l])
        mod_p = m[:bp].reshape(bp, 1, N_MOD * d)
        mod_s = jnp.repeat(m[bp:bp + bs], ls, axis=0)

        w_in_bf = w_in[l].astype(BF16)
        yc_p, u_p, tail_p = _inproj_conv_prompt(xp, mod_p, lp, g_mix[l], w_in_bf, conv_w[l], 512)
        e_s = jnp.pad(state_conv[l], ((0, 0), (0, ls - 2), (0, 0))).reshape(ns_, dc)
        yc_s, u_s, z_s = _inproj_conv_sample(xs, mod_s, g_mix[l], w_in_bf, conv_w[l], e_s, 256)
        new_conv_p = tail_p[:, 6:8, :]
        new_conv_s = z_s.reshape(bs, ls, dc)[:, ls - 2:, :]

        ops = _s5_operators(ssm_a_re[l], ssm_a_im[l], ssm_log_dt[l], ssm_b_re[l], ssm_b_im[l],
                            ssm_c_re[l], ssm_c_im[l], ssm_d[l], glu_w[l], glu_b[l])
        w8 = ls * ch
        ys_p, hf_p = _s5_prompt(u_p, 0, ops, bp, nc)
        new_re_p = hf_p[:, :, :p_st].transpose(1, 0, 2)
        new_im_p = hf_p[:, :, p_st:].transpose(1, 0, 2)

        h0 = jnp.concatenate([state_ssm_re[l], state_ssm_im[l]], axis=-1).transpose(1, 0, 2)
        ops_s = dict(ops, wc=ops['wc'][:, :, :w8], d=ops['d'][:, :, :w8], gb=ops['gb'][:, :, :w8],
                     p1=ops['p1_half'], p2=ops['p2_half'])
        ys_s, hf_s = _s5_sample(u_s, 0, h0, ops_s, ls)
        new_re_s = hf_s[:, :, :p_st].transpose(1, 0, 2)
        new_im_s = hf_s[:, :, p_st:].transpose(1, 0, 2)

        wo_bf = w_out[l].astype(BF16)
        wr_bf = w_router[l].astype(BF16)
        cnt0 = jnp.zeros((1, ne), F32)
        x1_p, h2_all, ti_p, gate_p, rank_p, cnt1 = _outproj(
            cnt0, xp, yc_p, ys_p, wo_bf, mod_p, False, lp, g_ffn[l], wr_bf, b_router[l],
            None, n_tok, 0, 2 * OUT_SUB)
        x1_s, h2_all, ti_s, gate_s, rank_s, cnt2 = _outproj(
            cnt1, xs, yc_s, ys_s, wo_bf, mod_s, True, ls, g_ffn[l], wr_bf, b_router[l],
            h2_all, n_tok, np_, OUT_SUB)

        counts = cnt2[0].astype(I32)
        top_i = jnp.concatenate([ti_p[:, :TOP_K], ti_s[:, :TOP_K]], axis=0)
        rank = jnp.concatenate([rank_p[:, :TOP_K], rank_s[:, :TOP_K]], axis=0)
        dest, slot_tok, blk_e, blk_rows, blk_chunks, blk_nrow, n_used = _moe_plan(
            counts, top_i, rank, n_tok, n_blocks)
        y_sorted = _moe_experts(h2_all, slot_tok, w1[l], b1[l], w2[l], b2[l],
                                blk_e, blk_rows, blk_chunks, blk_nrow, n_used)

        xp = _combine(x1_p, dest[:np_], y_sorted, gate_p, mod_p, False, lp, g_final, 128)
        xs = _combine(x1_s, dest[np_:], y_sorted, gate_s, mod_s, True, ls, g_final, 128)
        for lst, val in zip(outs, (new_conv_p, new_re_p, new_im_p, new_conv_s, new_re_s, new_im_s)):
            lst.append(val)

    y_prompt = xp.reshape(bp, lp, d)
    y_sample = xs.reshape(bs, ls, d)
    return (y_prompt, y_sample) + tuple(jnp.stack(o) for o in outs)
```

```python
import functools
import math

import jax
import jax.numpy as jnp
from jax import lax
from jax.experimental import pallas as pl
from jax.experimental.pallas import tpu as pltpu

F32 = jnp.float32
BF16 = jnp.bfloat16
I32 = jnp.int32

LANES = 128
EPS = 1e-6
N_MOD = 6
TOP_K = 4
SWIGLU_LIMIT = 7.0
SWIGLU_ALPHA = 1.702
GELU_C = math.sqrt(2.0 / math.pi)

S5_T = 16
OUT_SUB = 256
MOE_TB = 1536
MOE_CH = 256
MOE_TF = 512
MOE_TN = 512
VMEM_LIMIT = 56 * 1024 * 1024
HI = lax.Precision.HIGHEST


def _cparams(n_axes):
    return pltpu.CompilerParams(dimension_semantics=("arbitrary",) * n_axes,
                                vmem_limit_bytes=VMEM_LIMIT)


def _sigmoid(x):
    return 1.0 / (1.0 + jnp.exp(-x))


def _pack_bf16_pair(lo, hi):
    lo_bits = lax.bitcast_convert_type(lo, jnp.uint32) >> 16
    hi_bits = lax.bitcast_convert_type(hi, jnp.uint32) & jnp.uint32(0xFFFF0000)
    return lo_bits | hi_bits


def _unpack_bf16_pair(packed):
    lo = lax.bitcast_convert_type(packed << 16, F32)
    hi = lax.bitcast_convert_type(packed & jnp.uint32(0xFFFF0000), F32)
    return lo.astype(BF16), hi.astype(BF16)


def _resident(shape, index_map):
    return pl.BlockSpec(shape, index_map, pipeline_mode=pl.Buffered(1))


def _mod_body(c_ref, w_ref, b_ref, o_ref):
    c = c_ref[...]
    s = c * _sigmoid(c)
    o_ref[...] = jnp.dot(s.astype(BF16), w_ref[...].astype(BF16),
                         preferred_element_type=F32) + b_ref[...]


def _modulation(c_all, w_mod, b_mod):
    rows, d = c_all.shape
    n = w_mod.shape[1]
    tn = 1024
    return pl.pallas_call(
        _mod_body,
        grid=(n // tn,),
        in_specs=[pl.BlockSpec((rows, d), lambda j: (0, 0)),
                  pl.BlockSpec((d, tn), lambda j: (0, j)),
                  pl.BlockSpec((1, tn), lambda j: (0, j))],
        out_specs=pl.BlockSpec((rows, tn), lambda j: (0, j)),
        out_shape=jax.ShapeDtypeStruct((rows, n), F32),
        compiler_params=_cparams(1),
        name="modulation",
    )(c_all, w_mod, b_mod.reshape(1, n))


def _mod_spec(per_row, tm, rows_per_batch, d, col):
    if per_row:
        return pl.BlockSpec((tm, d), lambda i: (i, col))
    return pl.BlockSpec((None, 1, d), lambda i: ((i * tm) // rows_per_batch, 0, col))


def _inproj_pieces(x_ref, g_ref, sc_ref, sh_ref, w_ref, dc):
    x = x_ref[...]
    ms = jnp.mean(x * x, axis=-1, keepdims=True)
    h = x * lax.rsqrt(ms + EPS) * g_ref[...]
    hb = (h * (1.0 + sc_ref[...]) + sh_ref[...]).astype(BF16)
    b, c, v = (jnp.dot(hb, w_ref[:, k * dc:(k + 1) * dc], preferred_element_type=F32)
               for k in range(3))
    u = jnp.dot(hb, w_ref[:, 3 * dc:], preferred_element_type=F32)
    return b, c * v, u


def _conv_taps(bgate, z, z1, z2, w_ref):
    w = w_ref[...]
    return bgate * (w[0:1] * z2 + w[1:2] * z1 + w[2:3] * z)


def _inproj_conv_prompt_body(x_ref, g_ref, sc_ref, sh_ref, w_ref, cw_ref, y_ref, u_ref, tail_ref,
                             carry_ref, *, tiles_per_seq):
    @pl.when(pl.program_id(0) % tiles_per_seq == 0)
    def _():
        carry_ref[...] = jnp.zeros_like(carry_ref)

    bgate, z, u = _inproj_pieces(x_ref, g_ref, sc_ref, sh_ref, w_ref, y_ref.shape[1])
    u_ref[...] = u
    tt = z.shape[0]
    zc = jnp.concatenate([carry_ref[...], z], axis=0)
    z1 = pltpu.roll(zc, 1, 0)[8:]
    z2 = pltpu.roll(zc, 2, 0)[8:]
    y_ref[...] = _conv_taps(bgate, z, z1, z2, cw_ref).astype(y_ref.dtype)
    carry_ref[...] = z[tt - 8:]
    tail_ref[...] = z[tt - 8:]


def _inproj_conv_prompt(x, mod, seq, g, w_bf, conv_w, tm):
    n, d = x.shape
    dc = conv_w.shape[1]
    du = w_bf.shape[1] - 3 * dc
    row = lambda w: pl.BlockSpec((tm, w), lambda i: (i, 0))
    return pl.pallas_call(
        functools.partial(_inproj_conv_prompt_body, tiles_per_seq=seq // tm),
        grid=(n // tm,),
        in_specs=[row(d), pl.BlockSpec((1, d), lambda i: (0, 0)),
                  _mod_spec(False, tm, seq, d, 1), _mod_spec(False, tm, seq, d, 0),
                  _resident(w_bf.shape, lambda i: (0, 0)),
                  pl.BlockSpec(conv_w.shape, lambda i: (0, 0))],
        out_specs=[row(dc), row(du), pl.BlockSpec((None, 8, dc), lambda i: ((i * tm) // seq, 0, 0))],
        out_shape=[jax.ShapeDtypeStruct((n, dc), BF16), jax.ShapeDtypeStruct((n, du), F32),
                   jax.ShapeDtypeStruct((n // seq, 8, dc), F32)],
        scratch_shapes=[pltpu.VMEM((8, dc), F32)],
        compiler_params=_cparams(1),
        name="inproj_conv_prompt",
    )(x, g.reshape(1, d), mod, mod, w_bf, conv_w)


def _inproj_conv_sample_body(x_ref, g_ref, sc_ref, sh_ref, w_ref, cw_ref, e_ref, y_ref, u_ref, z_ref):
    bgate, z, u = _inproj_pieces(x_ref, g_ref, sc_ref, sh_ref, w_ref, y_ref.shape[1])
    u_ref[...] = u
    e = e_ref[...]
    rows = z.shape[0]
    tpos = lax.broadcasted_iota(I32, z.shape, 0) & 7
    z1 = jnp.where(tpos == 0, pltpu.roll(e, rows - 1, 0), pltpu.roll(z, 1, 0))
    z2 = jnp.where(tpos < 2, e, pltpu.roll(z, 2, 0))
    y_ref[...] = _conv_taps(bgate, z, z1, z2, cw_ref).astype(y_ref.dtype)
    z_ref[...] = z


def _inproj_conv_sample(x, mod, g, w_bf, conv_w, e, tm):
    n, d = x.shape
    dc = conv_w.shape[1]
    du = w_bf.shape[1] - 3 * dc
    row = lambda w: pl.BlockSpec((tm, w), lambda i: (i, 0))
    return pl.pallas_call(
        _inproj_conv_sample_body,
        grid=(n // tm,),
        in_specs=[row(d), pl.BlockSpec((1, d), lambda i: (0, 0)),
                  _mod_spec(True, tm, 0, d, 1), _mod_spec(True, tm, 0, d, 0),
                  _resident(w_bf.shape, lambda i: (0, 0)),
                  pl.BlockSpec(conv_w.shape, lambda i: (0, 0)), row(dc)],
        out_specs=[row(dc), row(du), row(dc)],
        out_shape=[jax.ShapeDtypeStruct((n, dc), BF16), jax.ShapeDtypeStruct((n, du), F32),
                   jax.ShapeDtypeStruct((n, dc), F32)],
        compiler_params=_cparams(1),
        name="inproj_conv_sample",
    )(x, g.reshape(1, d), mod, mod, w_bf, conv_w, e)


def _s5_operators(a_re, a_im, log_dt, b_re, b_im, c_re, c_im, d_skip, glu_w, glu_b):
    g, p, ch = b_re.shape
    t = S5_T
    dt = jnp.exp(log_dt)[:, None]
    mag = jnp.exp(a_re * dt)
    ar = mag * jnp.cos(a_im * dt)
    ai = mag * jnp.sin(a_im * dt)
    den = a_re * a_re + a_im * a_im
    qr = ((ar - 1.0) * a_re + ai * a_im) / den
    qi = (ai * a_re - (ar - 1.0) * a_im) / den
    bbr = qr[..., None] * b_re - qi[..., None] * b_im
    bbi = qr[..., None] * b_im + qi[..., None] * b_re
    pr, pi = [jnp.ones_like(ar)], [jnp.zeros_like(ar)]
    for _ in range(t):
        pr, pi = pr + [pr[-1] * ar - pi[-1] * ai], pi + [pr[-1] * ai + pi[-1] * ar]
    pw_r, pw_i = jnp.stack(pr), jnp.stack(pi)
    car = c_re[None] * pw_r[:, :, None, :] - c_im[None] * pw_i[:, :, None, :]
    cai = c_re[None] * pw_i[:, :, None, :] + c_im[None] * pw_r[:, :, None, :]
    rev_r, rev_i = pw_r[t - 1::-1][:t], pw_i[t - 1::-1][:t]
    wsr = rev_r[..., None] * bbr[None] - rev_i[..., None] * bbi[None]
    wsi = rev_r[..., None] * bbi[None] + rev_i[..., None] * bbr[None]
    ws_op = jnp.concatenate([wsr.transpose(1, 0, 3, 2), wsi.transpose(1, 0, 3, 2)],
                            axis=-1).reshape(g, t * ch, 2 * p)
    wcr = car[1:].transpose(1, 3, 0, 2).reshape(g, p, t * ch)
    wci = cai[1:].transpose(1, 3, 0, 2).reshape(g, p, t * ch)
    wc_op = jnp.concatenate([wcr, -wci], axis=1)
    cmat_t = jnp.concatenate([c_re, -c_im], axis=-1)
    glu_t = jnp.swapaxes(glu_w, 1, 2)
    d_flat = jnp.tile(d_skip, (1, t))[:, None, :]
    gb_flat = jnp.tile(glu_b, (1, t))[:, None, :]

    def rot_tables(xr, xi):
        return jnp.concatenate([xr, xr], axis=-1), jnp.concatenate([-xi, xi], axis=-1)

    sr, si = pw_r[t], pw_i[t]
    p1s, p2s = [], []
    for _ in range(8):
        t1, t2 = rot_tables(sr, si)
        p1s.append(t1)
        p2s.append(t2)
        sr, si = sr * sr - si * si, 2.0 * sr * si
    h = t // 2
    p1_half, p2_half = rot_tables(pw_r[h], pw_i[h])
    return dict(
        ws=ws_op, wc=wc_op.astype(BF16), cmat=cmat_t, glu=glu_t,
        d=d_flat, gb=gb_flat, p1=jnp.stack(p1s, axis=1), p2=jnp.stack(p2s, axis=1),
        p1_half=p1_half[:, None, :], p2_half=p2_half[:, None, :])


def _gelu_tanh(y):
    return 0.5 * y * (1.0 + jnp.tanh(GELU_C * (y + 0.044715 * (y * y * y))))


def _dot_t(x, w_t):
    return lax.dot_general(x, w_t, (((1,), (1,)), ((), ())), preferred_element_type=F32)


def _block_toeplitz_t(seq, steps):
    ch, n = seq.shape
    lane = lax.broadcasted_iota(I32, seq.shape, 1)
    rows = []
    for t in range(steps):
        shift = (ch * (t + 1)) % n
        r = pltpu.roll(seq, shift, 1) if shift else seq
        rows.append(jnp.where(lane < ch * (t + 1), r, 0.0))
    return jnp.concatenate(rows, axis=0).astype(BF16)


def _s5_group_operators(ws, cmat, glu_t, steps=S5_T):
    n = ws.shape[0]
    seq = lax.dot_general(cmat, ws, (((1,), (1,)), ((), ())), precision=HI,
                          preferred_element_type=F32)
    ch = cmat.shape[0]
    glu_seq = jnp.concatenate([jnp.zeros((ch, n - ch), F32), glu_t], axis=1)
    return _block_toeplitz_t(seq, steps), _block_toeplitz_t(glu_seq, steps)


def _s5_tail(y, gl_t, gb):
    y = _gelu_tanh(y)
    gate = _dot_t(y.astype(BF16), gl_t) + gb
    return y * _sigmoid(gate)


S5_CH = 16
S5_OCT = LANES // S5_CH


def _fold_time(rows, blk):
    halves = [_block_transpose(rows[h:h + S5_OCT], blk) for h in range(0, len(rows), S5_OCT)]
    if len(halves) == 1:
        return halves[0]
    return [jnp.concatenate([hv[q] for hv in halves], axis=1) for q in range(S5_OCT)]


def _unfold_time(ys, blk):
    rows = []
    for h in range(ys[0].shape[1] // LANES):
        rows += _block_transpose([y[:, LANES * h:LANES * (h + 1)] for y in ys], blk)
    return rows


def _block_transpose(vs, blk):
    vs = list(vs)
    for d in (4, 2, 1):
        upper = (blk & d) != 0
        new = list(vs)
        for i in range(S5_OCT):
            if i & d:
                continue
            a, b = vs[i], vs[i + d]
            new[i] = jnp.where(upper, pltpu.roll(b, S5_CH * d, 1), a)
            new[i + d] = jnp.where(upper, b, pltpu.roll(a, LANES - S5_CH * d, 1))
        vs = new
    return vs


def _s5_body(u_ref, us_ref, h0_ref, ws_ref, wc_ref, cm_ref, gl_ref, d_ref, gb_ref, p1_ref, p2_ref,
             p1s_ref, p2s_ref, y_ref, hf_ref, ys_ref, hfs_ref, uf, yf, *, bsz, nc, steps_s):
    half = ws_ref.shape[-1] // 2
    seq = nc * S5_T
    blk = lax.broadcasted_iota(I32, (nc, LANES), 1) // S5_CH
    nseq = us_ref.shape[0] // steps_s
    ws_w = steps_s * S5_CH
    blk_s = lax.broadcasted_iota(I32, (nseq, LANES), 1) // S5_CH
    us = _fold_time([us_ref[pl.ds(t, nseq, stride=steps_s), :] for t in range(steps_s)], blk_s)
    ys_s = []

    def fold(b, carry):
        rows = [u_ref[pl.ds(pl.multiple_of(b * seq, seq) + t, nc, stride=S5_T), :]
                for t in range(S5_T)]
        for q, uq in enumerate(_fold_time(rows, blk)):
            uf[q, pl.ds(pl.multiple_of(b * nc, nc), nc), :] = uq
        return carry

    lax.fori_loop(0, bsz, fold, 0)

    for g in range(S5_OCT):
        u = uf[g]
        ub = u.astype(BF16)
        m_t, glu_t = _s5_group_operators(ws_ref[g], cm_ref[g], gl_ref[g])
        wsb = ws_ref[g].astype(BF16)
        yi = _dot_t(ub, m_t)
        e = jnp.dot(ub, wsb, preferred_element_type=F32)
        cpos = lax.broadcasted_iota(I32, e.shape, 0) & (nc - 1)
        p1 = p1_ref[g]
        p2 = p2_ref[g]
        z = jnp.where(cpos >= 1, pltpu.roll(e, 1, 0), 0.0)
        d, k = 1, 0
        while d < nc:
            zs = jnp.where(cpos >= d, pltpu.roll(z, d, 0), 0.0)
            z = z + zs * p1[k:k + 1] + pltpu.roll(zs, half, 1) * p2[k:k + 1]
            d, k = d * 2, k + 1
        yc = jnp.dot(z.astype(BF16), wc_ref[g], preferred_element_type=F32)
        yf[g] = _s5_tail(yi + yc + d_ref[g] * u, glu_t, gb_ref[g])
        hfin = e + z * p1[0:1] + pltpu.roll(z, half, 1) * p2[0:1]
        for b in range(bsz):
            r = b * nc + nc - 1
            hf_ref[g, b:b + 1, :] = hfin[r:r + 1, :]

        u8 = us[g]
        h0 = h0_ref[g]
        ub8 = u8.astype(BF16)
        yi8 = _dot_t(ub8, m_t[:ws_w, :ws_w])
        e8 = jnp.dot(ub8, wsb[wsb.shape[0] - ws_w:, :], preferred_element_type=F32)
        yc8 = jnp.dot(h0.astype(BF16), wc_ref[g][:, :ws_w], preferred_element_type=F32)
        ys_s.append(_s5_tail(yi8 + yc8 + d_ref[g][:, :ws_w] * u8, glu_t[:ws_w, :ws_w],
                             gb_ref[g][:, :ws_w]))
        hfs_ref[g] = e8 + h0 * p1s_ref[g] + pltpu.roll(h0, half, 1) * p2s_ref[g]

    for t, row in enumerate(_unfold_time(ys_s, blk_s)):
        ys_ref[pl.ds(t, nseq, stride=steps_s), :] = row

    def unfold(b, carry):
        ys = [yf[q, pl.ds(pl.multiple_of(b * nc, nc), nc), :] for q in range(S5_OCT)]
        for t, row in enumerate(_unfold_time(ys, blk)):
            y_ref[pl.ds(pl.multiple_of(b * seq, seq) + t, nc, stride=S5_T), :] = row
        return carry

    lax.fori_loop(0, bsz, unfold, 0)


def _s5(u, u_s, h0, ops, bsz, nc, steps_s):
    n, n_s = u.shape[0], u_s.shape[0]
    g, w, p2x = ops['ws'].shape
    ch = ops['cmat'].shape[1]
    r_s = h0.shape[1]
    blk = lambda s1, s2: pl.BlockSpec((S5_OCT, s1, s2), lambda i: (i, 0, 0))
    col = lambda rows: pl.BlockSpec((rows, LANES), lambda i: (0, i))
    return pl.pallas_call(
        functools.partial(_s5_body, bsz=bsz, nc=nc, steps_s=steps_s),
        grid=(g // S5_OCT,),
        in_specs=[col(n), col(n_s), blk(r_s, p2x),
                  blk(w, p2x), blk(p2x, w), blk(ch, p2x), blk(ch, ch),
                  blk(1, w), blk(1, w), blk(8, p2x), blk(8, p2x), blk(1, p2x), blk(1, p2x)],
        out_specs=[col(n), blk(bsz, p2x), col(n_s), blk(r_s, p2x)],
        out_shape=[jax.ShapeDtypeStruct((n, g * S5_CH), F32),
                   jax.ShapeDtypeStruct((g, bsz, p2x), F32),
                   jax.ShapeDtypeStruct((n_s, g * S5_CH), F32),
                   jax.ShapeDtypeStruct((g, r_s, p2x), F32)],
        scratch_shapes=[pltpu.VMEM((S5_OCT, bsz * nc, w), F32),
                        pltpu.VMEM((S5_OCT, bsz * nc, w), F32)],
        compiler_params=_cparams(1),
        name="s5",
    )(u, u_s, h0, ops['ws'], ops['wc'], ops['cmat'], ops['glu'], ops['d'], ops['gb'],
      ops['p1'], ops['p2'], ops['p1_half'], ops['p2_half'])


def _outproj_body(cnt0_ref, x_ref, yc_ref, ys_ref, wo_ref, gt_ref, g_ref, sc_ref, sh_ref,
                  wr_ref, br_ref, *rest, has_prev):
    x1_ref, h2_ref, ti_ref, gate_ref, rank_ref, cnt_ref, run_ref = rest[1:] if has_prev else rest

    @pl.when(pl.program_id(0) == 0)
    def _():
        run_ref[...] = cnt0_ref[...]

    subs = [slice(r0, r0 + OUT_SUB) for r0 in range(0, x_ref.shape[0], OUT_SUB)]
    dc = yc_ref.shape[1]

    def mod(ref, rows):
        return ref[...] if ref.shape[0] == 1 else ref[rows, :]

    x1s = []
    for rows in subs:
        mix = (jnp.dot(yc_ref[rows, :], wo_ref[0:dc, :], preferred_element_type=F32)
               + jnp.dot(ys_ref[rows, :].astype(BF16), wo_ref[dc:, :], preferred_element_type=F32))
        x1s.append(x_ref[rows, :] + mod(gt_ref, rows) * mix)
    for rows, x1 in zip(subs, x1s):
        _route_rows(rows, x1, mod(sc_ref, rows), mod(sh_ref, rows), g_ref, wr_ref, br_ref,
                    x1_ref, h2_ref, ti_ref, gate_ref, rank_ref, run_ref)
    cnt_ref[...] = run_ref[...]


def _route_rows(rows, x1, sc, sh, g_ref, wr_ref, br_ref, x1_ref, h2_ref, ti_ref, gate_ref, rank_ref,
                run_ref):
    x1_ref[rows, :] = x1
    ms = jnp.mean(x1 * x1, axis=-1, keepdims=True)
    h = x1 * lax.rsqrt(ms + EPS) * g_ref[...]
    hb = (h * (1.0 + sc) + sh).astype(BF16)
    hb32 = hb.astype(F32)
    half = h2_ref.shape[1]
    h2_ref[rows, :] = _pack_bf16_pair(hb32[:, :half], hb32[:, half:])
    logits = jnp.dot(hb, wr_ref[...], preferred_element_type=F32) + br_ref[...]
    tm, ne = logits.shape
    lane = lax.broadcasted_iota(I32, logits.shape, 1).astype(F32)
    work = logits
    vals, ids, sels = [], [], []
    for _ in range(TOP_K):
        m = jnp.max(work, axis=1, keepdims=True)
        idx = jnp.min(jnp.where(work == m, lane, float(ne)), axis=1, keepdims=True)
        sel = lane == idx
        vals.append(m)
        ids.append(idx)
        sels.append(sel)
        work = jnp.where(sel, -jnp.inf, work)
    exps = [jnp.exp(v - vals[0]) for v in vals]
    tot = exps[0]
    for ex in exps[1:]:
        tot = tot + ex
    gates = [ex / tot for ex in exps]
    onehot = sels[0]
    for s in sels[1:]:
        onehot = onehot | s
    onehot = onehot.astype(F32)
    row = lax.broadcasted_iota(I32, (tm, tm), 0)
    col = lax.broadcasted_iota(I32, (tm, tm), 1)
    below = (col < row).astype(BF16)
    before = jnp.dot(below, onehot.astype(BF16), preferred_element_type=F32) + run_ref[...]
    ranks = [jnp.sum(jnp.where(s, before, 0.0), axis=1, keepdims=True) for s in sels]
    run_ref[...] = run_ref[...] + jnp.sum(onehot, axis=0, keepdims=True)

    wide = lax.broadcasted_iota(I32, (tm, ti_ref.shape[1]), 1)

    def spread(cols):
        out = cols[TOP_K - 1]
        for k in range(TOP_K - 2, -1, -1):
            out = jnp.where(wide == k, cols[k], out)
        return out

    ti_ref[rows, :] = spread(ids).astype(I32)
    gate_ref[rows, :] = spread(gates)
    rank_ref[rows, :] = spread(ranks).astype(I32)


def _outproj(cnt0, x, yc, ys, wo_bf, mod, per_row, rows_per_batch, g, wr_bf, br, h2_prev, n_all,
             row0, tm):
    n, d = x.shape
    dc = yc.shape[1]
    ne = wr_bf.shape[1]
    blk0 = row0 // tm
    row = lambda w: pl.BlockSpec((tm, w), lambda i: (i, 0))
    const = lambda s: pl.BlockSpec(s, lambda i: (0, 0))
    has_prev = h2_prev is not None
    in_specs = [const((1, ne)), row(d), row(dc), row(dc),
                _resident((d, d), lambda i: (0, 0)),
                _mod_spec(per_row, tm, rows_per_batch, d, 2), const((1, d)),
                _mod_spec(per_row, tm, rows_per_batch, d, 4),
                _mod_spec(per_row, tm, rows_per_batch, d, 3),
                const((d, ne)), const((1, ne))]
    args = [cnt0, x, yc, ys, wo_bf, mod, g.reshape(1, d), mod, mod, wr_bf, br.reshape(1, ne)]
    if has_prev:
        in_specs.append(pl.BlockSpec(memory_space=pl.ANY))
        args.append(h2_prev)
    return pl.pallas_call(
        functools.partial(_outproj_body, has_prev=has_prev),
        grid=(n // tm,),
        in_specs=in_specs,
        out_specs=[row(d), pl.BlockSpec((tm, d // 2), lambda i: (blk0 + i, 0)),
                   row(128), row(128), row(128), const((1, ne))],
        out_shape=[jax.ShapeDtypeStruct((n, d), F32),
                   jax.ShapeDtypeStruct((n_all, d // 2), jnp.uint32),
                   jax.ShapeDtypeStruct((n, 128), I32), jax.ShapeDtypeStruct((n, 128), F32),
                   jax.ShapeDtypeStruct((n, 128), I32), jax.ShapeDtypeStruct((1, ne), F32)],
        scratch_shapes=[pltpu.VMEM((1, ne), F32)],
        input_output_aliases={len(args) - 1: 1} if has_prev else {},
        compiler_params=_cparams(1),
        name="outproj_router",
    )(*args)


def _moe_body(be_ref, rb_ref, nch_ref, nrow_ref, tok0_ref, tokn_ref, h2_hbm,
              w1g_ref, w1l_ref, b1g_ref, b1l_ref, w2_ref, b2_ref,
              o_ref, xp, xb, act, sem, *, nf1):
    del be_ref, rb_ref
    b = pl.program_id(0)
    s = pl.program_id(1)
    n_chunks = nch_ref[b]

    def issue_rows(tok_ref, n_rows):
        def body(i, carry):
            r0 = pl.multiple_of(i * 8, 8)
            for u in range(8):
                pltpu.make_async_copy(h2_hbm.at[pl.ds(tok_ref[0, r0 + u], 1), :],
                                      xp.at[pl.ds(r0 + u, 1), :], sem.at[0]).start()
            return carry

        lax.fori_loop(0, n_rows // 8, body, 0)

    def wait_rows(n_rows):
        size = 8
        while size <= MOE_TB:
            @pl.when((n_rows & size) != 0)
            def _(size=size):
                pltpu.make_async_copy(h2_hbm.at[pl.ds(0, size), :], xp.at[pl.ds(0, size), :],
                                      sem.at[0]).wait()

            size *= 2

    @pl.when((b == 0) & (s == 0))
    def _():
        xp[...] = jnp.zeros_like(xp)
        issue_rows(tok0_ref, nrow_ref[0])

    @pl.when(s == 0)
    def _():
        wait_rows(nrow_ref[b])
        half = xp.shape[1]

        def unpack(i, carry):
            r0 = pl.multiple_of(i * MOE_CH, MOE_CH)
            xb[pl.ds(r0, MOE_CH), :half], xb[pl.ds(r0, MOE_CH), half:] = _unpack_bf16_pair(
                xp[pl.ds(r0, MOE_CH), :])
            return carry

        lax.fori_loop(0, n_chunks, unpack, 0)
        issue_rows(tokn_ref, nrow_ref[b + 1])

    def chunk_groups(one):
        assert MOE_TB // MOE_CH < 8
        for size in (4, 2, 1):
            @pl.when((n_chunks & size) != 0)
            def _(size=size):
                r0 = (n_chunks & ~(2 * size - 1)) * MOE_CH
                one(pl.multiple_of(r0, size * MOE_CH), size * MOE_CH)

    @pl.when((s < nf1) & (n_chunks > 0))
    def _():
        def one(r0, rows):
            x = xb[pl.ds(r0, rows), :]
            hg = jnp.dot(x, w1g_ref[...].astype(BF16), preferred_element_type=F32)
            hl = jnp.dot(x, w1l_ref[...].astype(BF16), preferred_element_type=F32)
            gl = jnp.minimum(hg + b1g_ref[...], SWIGLU_LIMIT)
            ln = jnp.clip(hl + b1l_ref[...], -SWIGLU_LIMIT, SWIGLU_LIMIT)
            a = gl * _sigmoid(SWIGLU_ALPHA * gl) * (ln + 1.0)
            act[s, pl.ds(r0, rows), :] = a.astype(BF16)

        chunk_groups(one)

    @pl.when((s >= nf1) & (n_chunks > 0))
    def _():
        def one(r0, rows):
            a = jnp.concatenate([act[k, pl.ds(r0, rows), :] for k in range(nf1)], axis=1)
            o_ref[pl.ds(r0, rows), :] = (
                jnp.dot(a, w2_ref[...].astype(BF16), preferred_element_type=F32) + b2_ref[...])

        chunk_groups(one)

        def zero(i, carry):
            r0 = pl.multiple_of(i * MOE_CH, MOE_CH)
            o_ref[pl.ds(r0, MOE_CH), :] = jnp.zeros((MOE_CH, o_ref.shape[1]), F32)
            return carry

        lax.fori_loop(n_chunks, MOE_TB // MOE_CH, zero, 0)


def _moe_experts(h2_all, slot_tok, w1, b1, w2, b2, blk_expert, blk_rows, blk_chunks, blk_nrow,
                 n_grid_blocks):
    ne, d, f2 = w1.shape
    f = f2 // 2
    nf1 = f // MOE_TF
    nf2 = d // MOE_TN
    nb = slot_tok.shape[0]

    def s1(b, s, nch):
        return jnp.where(nch[b] > 0, jnp.minimum(s, nf1 - 1), nf1 - 1)

    def s2(b, s, nch):
        return jnp.where(nch[b] > 0, jnp.maximum(s - nf1, 0), nf2 - 1)

    def w2_index(b, s, be, rb, nch, nr):
        early = (s < nf1 - 1) & (b > 0)
        e = jnp.where(early, be[jnp.maximum(b - 1, 0)], be[b])
        return e, 0, jnp.where(early, nf2 - 1, s2(b, s, nch))

    grid_spec = pltpu.PrefetchScalarGridSpec(
        num_scalar_prefetch=4,
        grid=(n_grid_blocks, nf1 + nf2),
        in_specs=[
            pl.BlockSpec((None, 1, MOE_TB), lambda b, s, be, rb, nch, nr: (0, 0, 0),
                         memory_space=pltpu.SMEM),
            pl.BlockSpec((None, 1, MOE_TB),
                         lambda b, s, be, rb, nch, nr: (jnp.minimum(b + 1, nb - 1), 0, 0),
                         memory_space=pltpu.SMEM),
            pl.BlockSpec(memory_space=pl.ANY),
            pl.BlockSpec((None, d, MOE_TF), lambda b, s, be, rb, nch, nr: (be[b], 0, s1(b, s, nch))),
            pl.BlockSpec((None, d, MOE_TF),
                         lambda b, s, be, rb, nch, nr: (be[b], 0, nf1 + s1(b, s, nch))),
            pl.BlockSpec((None, 1, MOE_TF), lambda b, s, be, rb, nch, nr: (be[b], 0, s1(b, s, nch))),
            pl.BlockSpec((None, 1, MOE_TF),
                         lambda b, s, be, rb, nch, nr: (be[b], 0, nf1 + s1(b, s, nch))),
            pl.BlockSpec((None, f, MOE_TN), w2_index),
            pl.BlockSpec((None, 1, MOE_TN), w2_index),
        ],
        out_specs=pl.BlockSpec((MOE_TB, MOE_TN),
                               lambda b, s, be, rb, nch, nr: (rb[b], s2(b, s, nch))),
        scratch_shapes=[pltpu.VMEM((MOE_TB, d // 2), jnp.uint32), pltpu.VMEM((MOE_TB, d), BF16),
                        pltpu.VMEM((nf1, MOE_TB, MOE_TF), BF16),
                        pltpu.SemaphoreType.DMA((1,))],
    )
    return pl.pallas_call(
        functools.partial(_moe_body, nf1=nf1),
        grid_spec=grid_spec,
        out_shape=jax.ShapeDtypeStruct((nb * MOE_TB, d), F32),
        compiler_params=_cparams(2),
        name="moe_experts",
    )(blk_expert, blk_rows, blk_chunks, blk_nrow, slot_tok, slot_tok, h2_all,
      w1, w1, b1.reshape(ne, 1, f2), b1.reshape(ne, 1, f2), w2, b2.reshape(ne, 1, d))


def _moe_plan(counts, top_i, rank, n_tokens, n_blocks):
    ne = counts.shape[0]
    nblk = (counts + MOE_TB - 1) // MOE_TB
    blk_end = jnp.cumsum(nblk)
    blk_start = blk_end - nblk
    n_used = blk_end[-1]
    dest = blk_start[top_i] * MOE_TB + rank
    bidx = jnp.arange(n_blocks + 1, dtype=I32)
    be = jnp.minimum(jnp.searchsorted(blk_end, bidx, side='right'), ne - 1).astype(I32)
    active = bidx < n_used
    valid = jnp.where(active, jnp.clip(counts[be] - (bidx - blk_start[be]) * MOE_TB, 0, MOE_TB), 0)
    chunks = ((valid + MOE_CH - 1) // MOE_CH).astype(I32)
    nrow = ((valid + 7) // 8 * 8).astype(I32)
    last = jnp.maximum(n_used - 1, 0)
    be = jnp.where(active, be, be[last]).astype(I32)
    rows = jnp.where(active, bidx, last).astype(I32)
    tok = jnp.arange(n_tokens * TOP_K, dtype=I32) // TOP_K
    slot_tok = jnp.zeros((n_blocks * MOE_TB,), I32).at[dest.reshape(-1)].set(
        tok, unique_indices=True, mode='promise_in_bounds')
    return dest, slot_tok.reshape(n_blocks, 1, MOE_TB), be, rows, chunks, nrow, n_used


def _combine_body(d0_ref, dn_ref, x1_ref, gate_ref, gt_ref, gf_ref, ys_hbm, o_ref, ybuf, sem):
    i = pl.program_id(0)
    tm = x1_ref.shape[0]
    slot = lax.rem(i, 2)

    def issue_rows(dest_ref, sl):
        def body(i, carry):
            r0 = pl.multiple_of(i * 8, 8)
            for u in range(8):
                for k in range(TOP_K):
                    pltpu.make_async_copy(
                        ys_hbm.at[pl.ds(dest_ref[0, (r0 + u) * TOP_K + k], 1), :],
                        ybuf.at[sl, k, pl.ds(r0 + u, 1), :], sem.at[sl]).start()
            return carry

        lax.fori_loop(0, tm // 8, body, 0)

    @pl.when(i == 0)
    def _():
        issue_rows(d0_ref, 0)

    @pl.when(i + 1 < pl.num_programs(0))
    def _():
        issue_rows(dn_ref, 1 - slot)

    pltpu.make_async_copy(ybuf.at[slot], ybuf.at[slot], sem.at[slot]).wait()
    gates = gate_ref[...]
    y = gates[:, 0:1] * ybuf[slot, 0]
    for k in range(1, TOP_K):
        y = y + gates[:, k:k + 1] * ybuf[slot, k]
    x2 = x1_ref[...] + gt_ref[...] * y
    ms = jnp.mean(x2 * x2, axis=-1, keepdims=True)
    o_ref[...] = x2 * lax.rsqrt(ms + EPS) * gf_ref[...]


def _combine(x1, dest, ys, gates, mod, per_row, rows_per_batch, g_final, tm):
    n, d = x1.shape
    nt = n // tm
    dest3 = dest.reshape(nt, 1, tm * TOP_K)
    return pl.pallas_call(
        _combine_body,
        grid=(nt,),
        in_specs=[pl.BlockSpec((None, 1, tm * TOP_K), lambda i: (0, 0, 0), memory_space=pltpu.SMEM),
                  pl.BlockSpec((None, 1, tm * TOP_K), lambda i: (jnp.minimum(i + 1, nt - 1), 0, 0),
                               memory_space=pltpu.SMEM),
                  pl.BlockSpec((tm, d), lambda i: (i, 0)),
                  pl.BlockSpec((tm, 128), lambda i: (i, 0)),
                  _mod_spec(per_row, tm, rows_per_batch, d, 5),
                  pl.BlockSpec((1, d), lambda i: (0, 0)),
                  pl.BlockSpec(memory_space=pl.ANY)],
        out_specs=pl.BlockSpec((tm, d), lambda i: (i, 0)),
        out_shape=jax.ShapeDtypeStruct((n, d), F32),
        scratch_shapes=[pltpu.VMEM((2, TOP_K, tm, d), F32), pltpu.SemaphoreType.DMA((2,))],
        compiler_params=_cparams(1),
        name="combine_norm",
    )(dest3, dest3, x1, gates, mod, g_final.reshape(1, d), ys)


def kernel(x_prompt, x_sample, c_prompt, c_sample, state_conv, state_ssm_re, state_ssm_im, g_mix, g_ffn, w_mod, b_mod, w_in, conv_w, ssm_a_re, ssm_a_im, ssm_log_dt, ssm_b_re, ssm_b_im, ssm_c_re, ssm_c_im, ssm_d, glu_w, glu_b, w_out, w_router, b_router, w1, b1, w2, b2, g_final):
    bp, lp, d = x_prompt.shape
    bs, ls, _ = x_sample.shape
    depth = g_mix.shape[0]
    dc = conv_w.shape[-1]
    n_grp, p_st, ch = ssm_b_re.shape[1:]
    ne = w_router.shape[-1]
    np_, ns_ = bp * lp, bs * ls
    n_tok = np_ + ns_
    assert ls == S5_T // 2 and lp % S5_T == 0 and conv_w.shape[1] == 3
    nc = lp // S5_T
    n_blocks = -(-(n_tok * TOP_K) // MOE_TB) + ne

    xp = x_prompt.reshape(np_, d)
    xs = x_sample.reshape(ns_, d)
    c_all = jnp.concatenate([c_prompt, c_sample], axis=0)
    pad = (-c_all.shape[0]) % 8
    c_all = jnp.pad(c_all, ((0, pad), (0, 0)))

    assert depth == 1
    outs = [[] for _ in range(6)]
    for l in range(depth):
        m = _modulation(c_all, w_mod[l], b_mod[l])
        mod_p = m[:bp].reshape(bp, 1, N_MOD * d)
        mod_s = jnp.repeat(m[bp:bp + bs], ls, axis=0)

        w_in_bf = w_in[l].astype(BF16)
        yc_p, u_p, tail_p = _inproj_conv_prompt(xp, mod_p, lp, g_mix[l], w_in_bf, conv_w[l], 512)
        e_s = jnp.pad(state_conv[l], ((0, 0), (0, ls - 2), (0, 0))).reshape(ns_, dc)
        yc_s, u_s, z_s = _inproj_conv_sample(xs, mod_s, g_mix[l], w_in_bf, conv_w[l], e_s, 256)
        new_conv_p = tail_p[:, 6:8, :]
        new_conv_s = z_s.reshape(bs, ls, dc)[:, ls - 2:, :]

        ops = _s5_operators(ssm_a_re[l], ssm_a_im[l], ssm_log_dt[l], ssm_b_re[l], ssm_b_im[l],
                            ssm_c_re[l], ssm_c_im[l], ssm_d[l], glu_w[l], glu_b[l])
        h0 = jnp.concatenate([state_ssm_re[l], state_ssm_im[l]], axis=-1).transpose(1, 0, 2)
        ys_p, hf_p, ys_s, hf_s = _s5(u_p, u_s, h0, ops, bp, nc, ls)
        new_re_p = hf_p[:, :, :p_st].transpose(1, 0, 2)
        new_im_p = hf_p[:, :, p_st:].transpose(1, 0, 2)
        new_re_s = hf_s[:, :, :p_st].transpose(1, 0, 2)
        new_im_s = hf_s[:, :, p_st:].transpose(1, 0, 2)

        wo_bf = w_out[l].astype(BF16)
        wr_bf = w_router[l].astype(BF16)
        cnt0 = jnp.zeros((1, ne), F32)
        x1_p, h2_all, ti_p, gate_p, rank_p, cnt1 = _outproj(
            cnt0, xp, yc_p, ys_p, wo_bf, mod_p, False, lp, g_ffn[l], wr_bf, b_router[l],
            None, n_tok, 0, 2 * OUT_SUB)
        x1_s, h2_all, ti_s, gate_s, rank_s, cnt2 = _outproj(
            cnt1, xs, yc_s, ys_s, wo_bf, mod_s, True, ls, g_ffn[l], wr_bf, b_router[l],
            h2_all, n_tok, np_, OUT_SUB)

        counts = cnt2[0].astype(I32)
        top_i = jnp.concatenate([ti_p[:, :TOP_K], ti_s[:, :TOP_K]], axis=0)
        rank = jnp.concatenate([rank_p[:, :TOP_K], rank_s[:, :TOP_K]], axis=0)
        dest, slot_tok, blk_e, blk_rows, blk_chunks, blk_nrow, n_used = _moe_plan(
            counts, top_i, rank, n_tok, n_blocks)
        y_sorted = _moe_experts(h2_all, slot_tok, w1[l], b1[l], w2[l], b2[l],
                                blk_e, blk_rows, blk_chunks, blk_nrow, n_used)

        xp = _combine(x1_p, dest[:np_], y_sorted, gate_p, mod_p, False, lp, g_final, 128)
        xs = _combine(x1_s, dest[np_:], y_sorted, gate_s, mod_s, True, ls, g_final, 128)
        for lst, val in zip(outs, (new_conv_p, new_re_p, new_im_p, new_conv_s, new_re_s, new_im_s)):
            lst.append(val)

    y_prompt = xp.reshape(bp, lp, d)
    y_sample = xs.reshape(bs, ls, d)
    return (y_prompt, y_sample) + tuple(jnp.stack(o) for o in outs)
```

```python
import functools
import math

import jax
import jax.numpy as jnp
from jax import lax
from jax.experimental import pallas as pl
from jax.experimental.pallas import tpu as pltpu

F32 = jnp.float32
BF16 = jnp.bfloat16
I32 = jnp.int32

LANES = 128
EPS = 1e-6
N_MOD = 6
TOP_K = 4
SWIGLU_LIMIT = 7.0
SWIGLU_ALPHA = 1.702
GELU_C = math.sqrt(2.0 / math.pi)

S5_T = 16
OUT_SUB = 256
MOE_TB = 1792
MOE_CH = 256
MOE_TF = 512
MOE_TN = 512
VMEM_LIMIT = 56 * 1024 * 1024
HI = lax.Precision.HIGHEST


def _cparams(n_axes):
    return pltpu.CompilerParams(dimension_semantics=("arbitrary",) * n_axes,
                                vmem_limit_bytes=VMEM_LIMIT)


def _sigmoid(x):
    return 1.0 / (1.0 + jnp.exp(-x))


def _pack_bf16_pair(lo, hi):
    lo_bits = lax.bitcast_convert_type(lo, jnp.uint32) >> 16
    hi_bits = lax.bitcast_convert_type(hi, jnp.uint32) & jnp.uint32(0xFFFF0000)
    return lo_bits | hi_bits


def _unpack_bf16_pair(packed):
    lo = lax.bitcast_convert_type(packed << 16, F32)
    hi = lax.bitcast_convert_type(packed & jnp.uint32(0xFFFF0000), F32)
    return lo.astype(BF16), hi.astype(BF16)


def _resident(shape, index_map):
    return pl.BlockSpec(shape, index_map, pipeline_mode=pl.Buffered(1))


def _mod_body(c_ref, w_ref, b_ref, o_ref):
    c = c_ref[...]
    s = c * _sigmoid(c)
    o_ref[...] = jnp.dot(s.astype(BF16), w_ref[...].astype(BF16),
                         preferred_element_type=F32) + b_ref[...]


def _modulation(c_all, w_mod, b_mod):
    rows, d = c_all.shape
    n = w_mod.shape[1]
    tn = 1024
    return pl.pallas_call(
        _mod_body,
        grid=(n // tn,),
        in_specs=[pl.BlockSpec((rows, d), lambda j: (0, 0)),
                  pl.BlockSpec((d, tn), lambda j: (0, j)),
                  pl.BlockSpec((1, tn), lambda j: (0, j))],
        out_specs=pl.BlockSpec((rows, tn), lambda j: (0, j)),
        out_shape=jax.ShapeDtypeStruct((rows, n), F32),
        compiler_params=_cparams(1),
        name="modulation",
    )(c_all, w_mod, b_mod.reshape(1, n))


def _mod_spec(per_row, tm, rows_per_batch, d, col):
    if per_row:
        return pl.BlockSpec((tm, d), lambda i: (i, col))
    return pl.BlockSpec((None, 1, d), lambda i: ((i * tm) // rows_per_batch, 0, col))


def _inproj_pieces(x_ref, g_ref, sc_ref, sh_ref, w_ref, dc):
    x = x_ref[...]
    ms = jnp.mean(x * x, axis=-1, keepdims=True)
    h = x * lax.rsqrt(ms + EPS) * g_ref[...]
    hb = (h * (1.0 + sc_ref[...]) + sh_ref[...]).astype(BF16)
    b, c, v = (jnp.dot(hb, w_ref[:, k * dc:(k + 1) * dc], preferred_element_type=F32)
               for k in range(3))
    u = jnp.dot(hb, w_ref[:, 3 * dc:], preferred_element_type=F32)
    return b, c * v, u


def _conv_taps(bgate, z, z1, z2, w_ref):
    w = w_ref[...]
    return bgate * (w[0:1] * z2 + w[1:2] * z1 + w[2:3] * z)


def _inproj_conv_prompt_body(x_ref, g_ref, sc_ref, sh_ref, w_ref, cw_ref, y_ref, u_ref, tail_ref,
                             carry_ref, *, tiles_per_seq):
    @pl.when(pl.program_id(0) % tiles_per_seq == 0)
    def _():
        carry_ref[...] = jnp.zeros_like(carry_ref)

    bgate, z, u = _inproj_pieces(x_ref, g_ref, sc_ref, sh_ref, w_ref, y_ref.shape[1])
    u_ref[...] = u
    tt = z.shape[0]
    zc = jnp.concatenate([carry_ref[...], z], axis=0)
    z1 = pltpu.roll(zc, 1, 0)[8:]
    z2 = pltpu.roll(zc, 2, 0)[8:]
    y_ref[...] = _conv_taps(bgate, z, z1, z2, cw_ref).astype(y_ref.dtype)
    carry_ref[...] = z[tt - 8:]
    tail_ref[...] = z[tt - 8:]


def _inproj_conv_prompt(x, mod, seq, g, w_bf, conv_w, tm):
    n, d = x.shape
    dc = conv_w.shape[1]
    du = w_bf.shape[1] - 3 * dc
    row = lambda w: pl.BlockSpec((tm, w), lambda i: (i, 0))
    return pl.pallas_call(
        functools.partial(_inproj_conv_prompt_body, tiles_per_seq=seq // tm),
        grid=(n // tm,),
        in_specs=[row(d), pl.BlockSpec((1, d), lambda i: (0, 0)),
                  _mod_spec(False, tm, seq, d, 1), _mod_spec(False, tm, seq, d, 0),
                  _resident(w_bf.shape, lambda i: (0, 0)),
                  pl.BlockSpec(conv_w.shape, lambda i: (0, 0))],
        out_specs=[row(dc), row(du), pl.BlockSpec((None, 8, dc), lambda i: ((i * tm) // seq, 0, 0))],
        out_shape=[jax.ShapeDtypeStruct((n, dc), BF16), jax.ShapeDtypeStruct((n, du), F32),
                   jax.ShapeDtypeStruct((n // seq, 8, dc), F32)],
        scratch_shapes=[pltpu.VMEM((8, dc), F32)],
        compiler_params=_cparams(1),
        name="inproj_conv_prompt",
    )(x, g.reshape(1, d), mod, mod, w_bf, conv_w)


def _inproj_conv_sample_body(x_ref, g_ref, sc_ref, sh_ref, w_ref, cw_ref, e_ref, y_ref, u_ref, z_ref):
    bgate, z, u = _inproj_pieces(x_ref, g_ref, sc_ref, sh_ref, w_ref, y_ref.shape[1])
    u_ref[...] = u
    e = e_ref[...]
    rows = z.shape[0]
    tpos = lax.broadcasted_iota(I32, z.shape, 0) & 7
    z1 = jnp.where(tpos == 0, pltpu.roll(e, rows - 1, 0), pltpu.roll(z, 1, 0))
    z2 = jnp.where(tpos < 2, e, pltpu.roll(z, 2, 0))
    y_ref[...] = _conv_taps(bgate, z, z1, z2, cw_ref).astype(y_ref.dtype)
    z_ref[...] = z


def _inproj_conv_sample(x, mod, g, w_bf, conv_w, e, tm):
    n, d = x.shape
    dc = conv_w.shape[1]
    du = w_bf.shape[1] - 3 * dc
    row = lambda w: pl.BlockSpec((tm, w), lambda i: (i, 0))
    return pl.pallas_call(
        _inproj_conv_sample_body,
        grid=(n // tm,),
        in_specs=[row(d), pl.BlockSpec((1, d), lambda i: (0, 0)),
                  _mod_spec(True, tm, 0, d, 1), _mod_spec(True, tm, 0, d, 0),
                  _resident(w_bf.shape, lambda i: (0, 0)),
                  pl.BlockSpec(conv_w.shape, lambda i: (0, 0)), row(dc)],
        out_specs=[row(dc), row(du), row(dc)],
        out_shape=[jax.ShapeDtypeStruct((n, dc), BF16), jax.ShapeDtypeStruct((n, du), F32),
                   jax.ShapeDtypeStruct((n, dc), F32)],
        compiler_params=_cparams(1),
        name="inproj_conv_sample",
    )(x, g.reshape(1, d), mod, mod, w_bf, conv_w, e)


def _s5_operators(a_re, a_im, log_dt, b_re, b_im, c_re, c_im, d_skip, glu_w, glu_b):
    g, p, ch = b_re.shape
    t = S5_T
    dt = jnp.exp(log_dt)[:, None]
    mag = jnp.exp(a_re * dt)
    ar = mag * jnp.cos(a_im * dt)
    ai = mag * jnp.sin(a_im * dt)
    den = a_re * a_re + a_im * a_im
    qr = ((ar - 1.0) * a_re + ai * a_im) / den
    qi = (ai * a_re - (ar - 1.0) * a_im) / den
    bbr = qr[..., None] * b_re - qi[..., None] * b_im
    bbi = qr[..., None] * b_im + qi[..., None] * b_re
    pr, pi = [jnp.ones_like(ar)], [jnp.zeros_like(ar)]
    for _ in range(t):
        pr, pi = pr + [pr[-1] * ar - pi[-1] * ai], pi + [pr[-1] * ai + pi[-1] * ar]
    pw_r, pw_i = jnp.stack(pr), jnp.stack(pi)
    car = c_re[None] * pw_r[:, :, None, :] - c_im[None] * pw_i[:, :, None, :]
    cai = c_re[None] * pw_i[:, :, None, :] + c_im[None] * pw_r[:, :, None, :]
    rev_r, rev_i = pw_r[t - 1::-1][:t], pw_i[t - 1::-1][:t]
    wsr = rev_r[..., None] * bbr[None] - rev_i[..., None] * bbi[None]
    wsi = rev_r[..., None] * bbi[None] + rev_i[..., None] * bbr[None]
    ws_op = jnp.concatenate([wsr.transpose(1, 0, 3, 2), wsi.transpose(1, 0, 3, 2)],
                            axis=-1).reshape(g, t * ch, 2 * p)
    wcr = car[1:].transpose(1, 3, 0, 2).reshape(g, p, t * ch)
    wci = cai[1:].transpose(1, 3, 0, 2).reshape(g, p, t * ch)
    wc_op = jnp.concatenate([wcr, -wci], axis=1)
    cmat_t = jnp.concatenate([c_re, -c_im], axis=-1)
    glu_t = jnp.swapaxes(glu_w, 1, 2)
    d_flat = jnp.tile(d_skip, (1, t))[:, None, :]
    gb_flat = jnp.tile(glu_b, (1, t))[:, None, :]

    def rot_tables(xr, xi):
        return jnp.concatenate([xr, xr], axis=-1), jnp.concatenate([-xi, xi], axis=-1)

    sr, si = pw_r[t], pw_i[t]
    p1s, p2s = [], []
    for _ in range(8):
        t1, t2 = rot_tables(sr, si)
        p1s.append(t1)
        p2s.append(t2)
        sr, si = sr * sr - si * si, 2.0 * sr * si
    h = t // 2
    p1_half, p2_half = rot_tables(pw_r[h], pw_i[h])
    return dict(
        ws=ws_op, wc=wc_op.astype(BF16), cmat=cmat_t, glu=glu_t,
        d=d_flat, gb=gb_flat, p1=jnp.stack(p1s, axis=1), p2=jnp.stack(p2s, axis=1),
        p1_half=p1_half[:, None, :], p2_half=p2_half[:, None, :])


def _gelu_tanh(y):
    return 0.5 * y * (1.0 + jnp.tanh(GELU_C * (y + 0.044715 * (y * y * y))))


def _dot_t(x, w_t):
    return lax.dot_general(x, w_t, (((1,), (1,)), ((), ())), preferred_element_type=F32)


def _block_toeplitz_t(seq, steps):
    ch, n = seq.shape
    lane = lax.broadcasted_iota(I32, seq.shape, 1)
    rows = []
    for t in range(steps):
        shift = (ch * (t + 1)) % n
        r = pltpu.roll(seq, shift, 1) if shift else seq
        rows.append(jnp.where(lane < ch * (t + 1), r, 0.0))
    return jnp.concatenate(rows, axis=0).astype(BF16)


def _s5_group_operators(ws, cmat, glu_t, steps=S5_T):
    n = ws.shape[0]
    seq = lax.dot_general(cmat, ws, (((1,), (1,)), ((), ())), precision=HI,
                          preferred_element_type=F32)
    ch = cmat.shape[0]
    glu_seq = jnp.concatenate([jnp.zeros((ch, n - ch), F32), glu_t], axis=1)
    return _block_toeplitz_t(seq, steps), _block_toeplitz_t(glu_seq, steps)


def _s5_tail(y, gl_t, gb):
    y = _gelu_tanh(y)
    gate = _dot_t(y.astype(BF16), gl_t) + gb
    return y * _sigmoid(gate)


S5_CH = 16
S5_OCT = LANES // S5_CH


def _fold_time(rows, blk):
    halves = [_block_transpose(rows[h:h + S5_OCT], blk) for h in range(0, len(rows), S5_OCT)]
    if len(halves) == 1:
        return halves[0]
    return [jnp.concatenate([hv[q] for hv in halves], axis=1) for q in range(S5_OCT)]


def _unfold_time(ys, blk):
    rows = []
    for h in range(ys[0].shape[1] // LANES):
        rows += _block_transpose([y[:, LANES * h:LANES * (h + 1)] for y in ys], blk)
    return rows


def _block_transpose(vs, blk):
    vs = list(vs)
    for d in (4, 2, 1):
        upper = (blk & d) != 0
        new = list(vs)
        for i in range(S5_OCT):
            if i & d:
                continue
            a, b = vs[i], vs[i + d]
            new[i] = jnp.where(upper, pltpu.roll(b, S5_CH * d, 1), a)
            new[i + d] = jnp.where(upper, b, pltpu.roll(a, LANES - S5_CH * d, 1))
        vs = new
    return vs


def _s5_body(u_ref, us_ref, h0_ref, ws_ref, wc_ref, cm_ref, gl_ref, d_ref, gb_ref, p1_ref, p2_ref,
             p1s_ref, p2s_ref, y_ref, hf_ref, ys_ref, hfs_ref, uf, yf, *, bsz, nc, steps_s):
    half = ws_ref.shape[-1] // 2
    seq = nc * S5_T
    blk = lax.broadcasted_iota(I32, (nc, LANES), 1) // S5_CH
    nseq = us_ref.shape[0] // steps_s
    ws_w = steps_s * S5_CH
    blk_s = lax.broadcasted_iota(I32, (nseq, LANES), 1) // S5_CH
    us = _fold_time([us_ref[pl.ds(t, nseq, stride=steps_s), :] for t in range(steps_s)], blk_s)
    ys_s = []

    def fold(b, carry):
        rows = [u_ref[pl.ds(pl.multiple_of(b * seq, seq) + t, nc, stride=S5_T), :]
                for t in range(S5_T)]
        for q, uq in enumerate(_fold_time(rows, blk)):
            uf[q, pl.ds(pl.multiple_of(b * nc, nc), nc), :] = uq
        return carry

    lax.fori_loop(0, bsz, fold, 0)

    for g in range(S5_OCT):
        u = uf[g]
        ub = u.astype(BF16)
        m_t, glu_t = _s5_group_operators(ws_ref[g], cm_ref[g], gl_ref[g])
        wsb = ws_ref[g].astype(BF16)
        yi = _dot_t(ub, m_t)
        e = jnp.dot(ub, wsb, preferred_element_type=F32)
        cpos = lax.broadcasted_iota(I32, e.shape, 0) & (nc - 1)
        p1 = p1_ref[g]
        p2 = p2_ref[g]
        z = jnp.where(cpos >= 1, pltpu.roll(e, 1, 0), 0.0)
        d, k = 1, 0
        while d < nc:
            zs = jnp.where(cpos >= d, pltpu.roll(z, d, 0), 0.0)
            z = z + zs * p1[k:k + 1] + pltpu.roll(zs, half, 1) * p2[k:k + 1]
            d, k = d * 2, k + 1
        yc = jnp.dot(z.astype(BF16), wc_ref[g], preferred_element_type=F32)
        yf[g] = _s5_tail(yi + yc + d_ref[g] * u, glu_t, gb_ref[g])
        hfin = e + z * p1[0:1] + pltpu.roll(z, half, 1) * p2[0:1]
        for b in range(bsz):
            r = b * nc + nc - 1
            hf_ref[g, b:b + 1, :] = hfin[r:r + 1, :]

        u8 = us[g]
        h0 = h0_ref[g]
        ub8 = u8.astype(BF16)
        yi8 = _dot_t(ub8, m_t[:ws_w, :ws_w])
        e8 = jnp.dot(ub8, wsb[wsb.shape[0] - ws_w:, :], preferred_element_type=F32)
        yc8 = jnp.dot(h0.astype(BF16), wc_ref[g][:, :ws_w], preferred_element_type=F32)
        ys_s.append(_s5_tail(yi8 + yc8 + d_ref[g][:, :ws_w] * u8, glu_t[:ws_w, :ws_w],
                             gb_ref[g][:, :ws_w]))
        hfs_ref[g] = e8 + h0 * p1s_ref[g] + pltpu.roll(h0, half, 1) * p2s_ref[g]

    for t, row in enumerate(_unfold_time(ys_s, blk_s)):
        ys_ref[pl.ds(t, nseq, stride=steps_s), :] = row

    def unfold(b, carry):
        ys = [yf[q, pl.ds(pl.multiple_of(b * nc, nc), nc), :] for q in range(S5_OCT)]
        for t, row in enumerate(_unfold_time(ys, blk)):
            y_ref[pl.ds(pl.multiple_of(b * seq, seq) + t, nc, stride=S5_T), :] = row
        return carry

    lax.fori_loop(0, bsz, unfold, 0)


def _s5(u, u_s, h0, ops, bsz, nc, steps_s):
    n, n_s = u.shape[0], u_s.shape[0]
    g, w, p2x = ops['ws'].shape
    ch = ops['cmat'].shape[1]
    r_s = h0.shape[1]
    blk = lambda s1, s2: pl.BlockSpec((S5_OCT, s1, s2), lambda i: (i, 0, 0))
    col = lambda rows: pl.BlockSpec((rows, LANES), lambda i: (0, i))
    return pl.pallas_call(
        functools.partial(_s5_body, bsz=bsz, nc=nc, steps_s=steps_s),
        grid=(g // S5_OCT,),
        in_specs=[col(n), col(n_s), blk(r_s, p2x),
                  blk(w, p2x), blk(p2x, w), blk(ch, p2x), blk(ch, ch),
                  blk(1, w), blk(1, w), blk(8, p2x), blk(8, p2x), blk(1, p2x), blk(1, p2x)],
        out_specs=[col(n), blk(bsz, p2x), col(n_s), blk(r_s, p2x)],
        out_shape=[jax.ShapeDtypeStruct((n, g * S5_CH), F32),
                   jax.ShapeDtypeStruct((g, bsz, p2x), F32),
                   jax.ShapeDtypeStruct((n_s, g * S5_CH), F32),
                   jax.ShapeDtypeStruct((g, r_s, p2x), F32)],
        scratch_shapes=[pltpu.VMEM((S5_OCT, bsz * nc, w), F32),
                        pltpu.VMEM((S5_OCT, bsz * nc, w), F32)],
        compiler_params=_cparams(1),
        name="s5",
    )(u, u_s, h0, ops['ws'], ops['wc'], ops['cmat'], ops['glu'], ops['d'], ops['gb'],
      ops['p1'], ops['p2'], ops['p1_half'], ops['p2_half'])


def _outproj_body(cnt0_ref, x_ref, yc_ref, ys_ref, wo_ref, gt_ref, g_ref, sc_ref, sh_ref,
                  wr_ref, br_ref, *rest, has_prev):
    x1_ref, h2_ref, ti_ref, gate_ref, rank_ref, cnt_ref, run_ref = rest[1:] if has_prev else rest

    @pl.when(pl.program_id(0) == 0)
    def _():
        run_ref[...] = cnt0_ref[...]

    subs = [slice(r0, r0 + OUT_SUB) for r0 in range(0, x_ref.shape[0], OUT_SUB)]
    dc = yc_ref.shape[1]

    def mod(ref, rows):
        return ref[...] if ref.shape[0] == 1 else ref[rows, :]

    x1s = []
    for rows in subs:
        mix = (jnp.dot(yc_ref[rows, :], wo_ref[0:dc, :], preferred_element_type=F32)
               + jnp.dot(ys_ref[rows, :].astype(BF16), wo_ref[dc:, :], preferred_element_type=F32))
        x1s.append(x_ref[rows, :] + mod(gt_ref, rows) * mix)
    for rows, x1 in zip(subs, x1s):
        _route_rows(rows, x1, mod(sc_ref, rows), mod(sh_ref, rows), g_ref, wr_ref, br_ref,
                    x1_ref, h2_ref, ti_ref, gate_ref, rank_ref, run_ref)
    cnt_ref[...] = run_ref[...]


def _route_rows(rows, x1, sc, sh, g_ref, wr_ref, br_ref, x1_ref, h2_ref, ti_ref, gate_ref, rank_ref,
                run_ref):
    x1_ref[rows, :] = x1
    ms = jnp.mean(x1 * x1, axis=-1, keepdims=True)
    h = x1 * lax.rsqrt(ms + EPS) * g_ref[...]
    hb = (h * (1.0 + sc) + sh).astype(BF16)
    hb32 = hb.astype(F32)
    half = h2_ref.shape[1]
    h2_ref[rows, :] = _pack_bf16_pair(hb32[:, :half], hb32[:, half:])
    logits = jnp.dot(hb, wr_ref[...], preferred_element_type=F32) + br_ref[...]
    tm, ne = logits.shape
    lane = lax.broadcasted_iota(I32, logits.shape, 1).astype(F32)
    work = logits
    vals, ids, sels = [], [], []
    for _ in range(TOP_K):
        m = jnp.max(work, axis=1, keepdims=True)
        idx = jnp.min(jnp.where(work == m, lane, float(ne)), axis=1, keepdims=True)
        sel = lane == idx
        vals.append(m)
        ids.append(idx)
        sels.append(sel)
        work = jnp.where(sel, -jnp.inf, work)
    exps = [jnp.exp(v - vals[0]) for v in vals]
    tot = exps[0]
    for ex in exps[1:]:
        tot = tot + ex
    gates = [ex / tot for ex in exps]
    onehot = sels[0]
    for s in sels[1:]:
        onehot = onehot | s
    onehot = onehot.astype(F32)
    row = lax.broadcasted_iota(I32, (tm, tm), 0)
    col = lax.broadcasted_iota(I32, (tm, tm), 1)
    below = (col < row).astype(BF16)
    before = jnp.dot(below, onehot.astype(BF16), preferred_element_type=F32) + run_ref[...]
    ranks = [jnp.sum(jnp.where(s, before, 0.0), axis=1, keepdims=True) for s in sels]
    run_ref[...] = run_ref[...] + jnp.sum(onehot, axis=0, keepdims=True)

    wide = lax.broadcasted_iota(I32, (tm, ti_ref.shape[1]), 1)

    def spread(cols):
        out = cols[TOP_K - 1]
        for k in range(TOP_K - 2, -1, -1):
            out = jnp.where(wide == k, cols[k], out)
        return out

    ti_ref[rows, :] = spread(ids).astype(I32)
    gate_ref[rows, :] = spread(gates)
    rank_ref[rows, :] = spread(ranks).astype(I32)


def _outproj(cnt0, x, yc, ys, wo_bf, mod, per_row, rows_per_batch, g, wr_bf, br, h2_prev, n_all,
             row0, tm):
    n, d = x.shape
    dc = yc.shape[1]
    ne = wr_bf.shape[1]
    blk0 = row0 // tm
    row = lambda w: pl.BlockSpec((tm, w), lambda i: (i, 0))
    const = lambda s: pl.BlockSpec(s, lambda i: (0, 0))
    has_prev = h2_prev is not None
    in_specs = [const((1, ne)), row(d), row(dc), row(dc),
                _resident((d, d), lambda i: (0, 0)),
                _mod_spec(per_row, tm, rows_per_batch, d, 2), const((1, d)),
                _mod_spec(per_row, tm, rows_per_batch, d, 4),
                _mod_spec(per_row, tm, rows_per_batch, d, 3),
                const((d, ne)), const((1, ne))]
    args = [cnt0, x, yc, ys, wo_bf, mod, g.reshape(1, d), mod, mod, wr_bf, br.reshape(1, ne)]
    if has_prev:
        in_specs.append(pl.BlockSpec(memory_space=pl.ANY))
        args.append(h2_prev)
    return pl.pallas_call(
        functools.partial(_outproj_body, has_prev=has_prev),
        grid=(n // tm,),
        in_specs=in_specs,
        out_specs=[row(d), pl.BlockSpec((tm, d // 2), lambda i: (blk0 + i, 0)),
                   row(128), row(128), row(128), const((1, ne))],
        out_shape=[jax.ShapeDtypeStruct((n, d), F32),
                   jax.ShapeDtypeStruct((n_all, d // 2), jnp.uint32),
                   jax.ShapeDtypeStruct((n, 128), I32), jax.ShapeDtypeStruct((n, 128), F32),
                   jax.ShapeDtypeStruct((n, 128), I32), jax.ShapeDtypeStruct((1, ne), F32)],
        scratch_shapes=[pltpu.VMEM((1, ne), F32)],
        input_output_aliases={len(args) - 1: 1} if has_prev else {},
        compiler_params=_cparams(1),
        name="outproj_router",
    )(*args)


def _moe_body(be_ref, rb_ref, nch_ref, nrow_ref, tok0_ref, tokn_ref, h2_hbm,
              w1g_ref, w1l_ref, b1g_ref, b1l_ref, w2_ref, b2_ref,
              o_ref, xp, act, sem, *, nf1):
    del be_ref, rb_ref
    b = pl.program_id(0)
    s = pl.program_id(1)
    n_chunks = nch_ref[b]

    def issue_rows(tok_ref, n_rows):
        def body(i, carry):
            r0 = pl.multiple_of(i * 8, 8)
            for u in range(8):
                pltpu.make_async_copy(h2_hbm.at[pl.ds(tok_ref[0, r0 + u], 1), :],
                                      xp.at[pl.ds(r0 + u, 1), :], sem.at[0]).start()
            return carry

        lax.fori_loop(0, n_rows // 8, body, 0)

    def wait_rows(n_rows):
        size = 8
        while size <= MOE_TB:
            @pl.when((n_rows & size) != 0)
            def _(size=size):
                pltpu.make_async_copy(h2_hbm.at[pl.ds(0, size), :], xp.at[pl.ds(0, size), :],
                                      sem.at[0]).wait()

            size *= 2

    @pl.when((b == 0) & (s == 0))
    def _():
        xp[...] = jnp.zeros_like(xp)
        issue_rows(tok0_ref, nrow_ref[0])

    @pl.when(s == 0)
    def _():
        wait_rows(nrow_ref[b])

    @pl.when(s == nf1)
    def _():
        issue_rows(tokn_ref, nrow_ref[b + 1])

    def chunk_groups(one):
        assert MOE_TB // MOE_CH < 8
        for size in (4, 2, 1):
            @pl.when((n_chunks & size) != 0)
            def _(size=size):
                r0 = (n_chunks & ~(2 * size - 1)) * MOE_CH
                one(pl.multiple_of(r0, size * MOE_CH), size * MOE_CH)

    @pl.when((s < nf1) & (n_chunks > 0))
    def _():
        def one(r0, rows):
            x = jnp.concatenate(_unpack_bf16_pair(xp[pl.ds(r0, rows), :]), axis=1)
            hg = jnp.dot(x, w1g_ref[...].astype(BF16), preferred_element_type=F32)
            hl = jnp.dot(x, w1l_ref[...].astype(BF16), preferred_element_type=F32)
            gl = jnp.minimum(hg + b1g_ref[...], SWIGLU_LIMIT)
            ln = jnp.clip(hl + b1l_ref[...], -SWIGLU_LIMIT, SWIGLU_LIMIT)
            a = gl * _sigmoid(SWIGLU_ALPHA * gl) * (ln + 1.0)
            act[s, pl.ds(r0, rows), :] = a.astype(BF16)

        chunk_groups(one)

    @pl.when((s >= nf1) & (n_chunks > 0))
    def _():
        def one(r0, rows):
            a = jnp.concatenate([act[k, pl.ds(r0, rows), :] for k in range(nf1)], axis=1)
            o_ref[pl.ds(r0, rows), :] = (
                jnp.dot(a, w2_ref[...].astype(BF16), preferred_element_type=F32) + b2_ref[...])

        chunk_groups(one)

        def zero(i, carry):
            r0 = pl.multiple_of(i * MOE_CH, MOE_CH)
            o_ref[pl.ds(r0, MOE_CH), :] = jnp.zeros((MOE_CH, o_ref.shape[1]), F32)
            return carry

        lax.fori_loop(n_chunks, MOE_TB // MOE_CH, zero, 0)


def _moe_experts(h2_all, slot_tok, w1, b1, w2, b2, blk_expert, blk_rows, blk_chunks, blk_nrow,
                 n_grid_blocks):
    ne, d, f2 = w1.shape
    f = f2 // 2
    nf1 = f // MOE_TF
    nf2 = d // MOE_TN
    nb = slot_tok.shape[0]

    def s1(b, s, nch):
        return jnp.where(nch[b] > 0, jnp.minimum(s, nf1 - 1), nf1 - 1)

    def s2(b, s, nch):
        return jnp.where(nch[b] > 0, jnp.maximum(s - nf1, 0), nf2 - 1)

    def w2_index(b, s, be, rb, nch, nr):
        early = (s < nf1 - 1) & (b > 0)
        e = jnp.where(early, be[jnp.maximum(b - 1, 0)], be[b])
        return e, 0, jnp.where(early, nf2 - 1, s2(b, s, nch))

    grid_spec = pltpu.PrefetchScalarGridSpec(
        num_scalar_prefetch=4,
        grid=(n_grid_blocks, nf1 + nf2),
        in_specs=[
            pl.BlockSpec((None, 1, MOE_TB), lambda b, s, be, rb, nch, nr: (0, 0, 0),
                         memory_space=pltpu.SMEM),
            pl.BlockSpec((None, 1, MOE_TB),
                         lambda b, s, be, rb, nch, nr: (jnp.minimum(b + 1, nb - 1), 0, 0),
                         memory_space=pltpu.SMEM),
            pl.BlockSpec(memory_space=pl.ANY),
            pl.BlockSpec((None, d, MOE_TF), lambda b, s, be, rb, nch, nr: (be[b], 0, s1(b, s, nch))),
            pl.BlockSpec((None, d, MOE_TF),
                         lambda b, s, be, rb, nch, nr: (be[b], 0, nf1 + s1(b, s, nch))),
            pl.BlockSpec((None, 1, MOE_TF), lambda b, s, be, rb, nch, nr: (be[b], 0, s1(b, s, nch))),
            pl.BlockSpec((None, 1, MOE_TF),
                         lambda b, s, be, rb, nch, nr: (be[b], 0, nf1 + s1(b, s, nch))),
            pl.BlockSpec((None, f, MOE_TN), w2_index),
            pl.BlockSpec((None, 1, MOE_TN), w2_index),
        ],
        out_specs=pl.BlockSpec((MOE_TB, MOE_TN),
                               lambda b, s, be, rb, nch, nr: (rb[b], s2(b, s, nch))),
        scratch_shapes=[pltpu.VMEM((MOE_TB, d // 2), jnp.uint32),
                        pltpu.VMEM((nf1, MOE_TB, MOE_TF), BF16),
                        pltpu.SemaphoreType.DMA((1,))],
    )
    return pl.pallas_call(
        functools.partial(_moe_body, nf1=nf1),
        grid_spec=grid_spec,
        out_shape=jax.ShapeDtypeStruct((nb * MOE_TB, d), F32),
        compiler_params=_cparams(2),
        name="moe_experts",
    )(blk_expert, blk_rows, blk_chunks, blk_nrow, slot_tok, slot_tok, h2_all,
      w1, w1, b1.reshape(ne, 1, f2), b1.reshape(ne, 1, f2), w2, b2.reshape(ne, 1, d))


def _moe_plan(counts, top_i, rank, n_tokens, n_blocks):
    ne = counts.shape[0]
    nblk = (counts + MOE_TB - 1) // MOE_TB
    blk_end = jnp.cumsum(nblk)
    blk_start = blk_end - nblk
    n_used = blk_end[-1]
    dest = blk_start[top_i] * MOE_TB + rank
    bidx = jnp.arange(n_blocks + 1, dtype=I32)
    be = jnp.minimum(jnp.searchsorted(blk_end, bidx, side='right'), ne - 1).astype(I32)
    active = bidx < n_used
    valid = jnp.where(active, jnp.clip(counts[be] - (bidx - blk_start[be]) * MOE_TB, 0, MOE_TB), 0)
    chunks = ((valid + MOE_CH - 1) // MOE_CH).astype(I32)
    nrow = ((valid + 7) // 8 * 8).astype(I32)
    last = jnp.maximum(n_used - 1, 0)
    be = jnp.where(active, be, be[last]).astype(I32)
    rows = jnp.where(active, bidx, last).astype(I32)
    tok = jnp.arange(n_tokens * TOP_K, dtype=I32) // TOP_K
    slot_tok = jnp.zeros((n_blocks * MOE_TB,), I32).at[dest.reshape(-1)].set(
        tok, unique_indices=True, mode='promise_in_bounds')
    return dest, slot_tok.reshape(n_blocks, 1, MOE_TB), be, rows, chunks, nrow, n_used


def _combine_body(d0_ref, dn_ref, x1_ref, gate_ref, gt_ref, gf_ref, ys_hbm, o_ref, ybuf, sem):
    i = pl.program_id(0)
    tm = x1_ref.shape[0]
    slot = lax.rem(i, 2)

    def issue_rows(dest_ref, sl):
        def body(i, carry):
            r0 = pl.multiple_of(i * 8, 8)
            for u in range(8):
                for k in range(TOP_K):
                    pltpu.make_async_copy(
                        ys_hbm.at[pl.ds(dest_ref[0, (r0 + u) * TOP_K + k], 1), :],
                        ybuf.at[sl, k, pl.ds(r0 + u, 1), :], sem.at[sl]).start()
            return carry

        lax.fori_loop(0, tm // 8, body, 0)

    @pl.when(i == 0)
    def _():
        issue_rows(d0_ref, 0)

    @pl.when(i + 1 < pl.num_programs(0))
    def _():
        issue_rows(dn_ref, 1 - slot)

    pltpu.make_async_copy(ybuf.at[slot], ybuf.at[slot], sem.at[slot]).wait()
    gates = gate_ref[...]
    y = gates[:, 0:1] * ybuf[slot, 0]
    for k in range(1, TOP_K):
        y = y + gates[:, k:k + 1] * ybuf[slot, k]
    x2 = x1_ref[...] + gt_ref[...] * y
    ms = jnp.mean(x2 * x2, axis=-1, keepdims=True)
    o_ref[...] = x2 * lax.rsqrt(ms + EPS) * gf_ref[...]


def _combine(x1, dest, ys, gates, mod, per_row, rows_per_batch, g_final, tm):
    n, d = x1.shape
    nt = n // tm
    dest3 = dest.reshape(nt, 1, tm * TOP_K)
    return pl.pallas_call(
        _combine_body,
        grid=(nt,),
        in_specs=[pl.BlockSpec((None, 1, tm * TOP_K), lambda i: (0, 0, 0), memory_space=pltpu.SMEM),
                  pl.BlockSpec((None, 1, tm * TOP_K), lambda i: (jnp.minimum(i + 1, nt - 1), 0, 0),
                               memory_space=pltpu.SMEM),
                  pl.BlockSpec((tm, d), lambda i: (i, 0)),
                  pl.BlockSpec((tm, 128), lambda i: (i, 0)),
                  _mod_spec(per_row, tm, rows_per_batch, d, 5),
                  pl.BlockSpec((1, d), lambda i: (0, 0)),
                  pl.BlockSpec(memory_space=pl.ANY)],
        out_specs=pl.BlockSpec((tm, d), lambda i: (i, 0)),
        out_shape=jax.ShapeDtypeStruct((n, d), F32),
        scratch_shapes=[pltpu.VMEM((2, TOP_K, tm, d), F32), pltpu.SemaphoreType.DMA((2,))],
        compiler_params=_cparams(1),
        name="combine_norm",
    )(dest3, dest3, x1, gates, mod, g_final.reshape(1, d), ys)


def kernel(x_prompt, x_sample, c_prompt, c_sample, state_conv, state_ssm_re, state_ssm_im, g_mix, g_ffn, w_mod, b_mod, w_in, conv_w, ssm_a_re, ssm_a_im, ssm_log_dt, ssm_b_re, ssm_b_im, ssm_c_re, ssm_c_im, ssm_d, glu_w, glu_b, w_out, w_router, b_router, w1, b1, w2, b2, g_final):
    bp, lp, d = x_prompt.shape
    bs, ls, _ = x_sample.shape
    depth = g_mix.shape[0]
    dc = conv_w.shape[-1]
    n_grp, p_st, ch = ssm_b_re.shape[1:]
    ne = w_router.shape[-1]
    np_, ns_ = bp * lp, bs * ls
    n_tok = np_ + ns_
    assert ls == S5_T // 2 and lp % S5_T == 0 and conv_w.shape[1] == 3
    nc = lp // S5_T
    n_blocks = -(-(n_tok * TOP_K) // MOE_TB) + ne

    xp = x_prompt.reshape(np_, d)
    xs = x_sample.reshape(ns_, d)
    c_all = jnp.concatenate([c_prompt, c_sample], axis=0)
    pad = (-c_all.shape[0]) % 8
    c_all = jnp.pad(c_all, ((0, pad), (0, 0)))

    assert depth == 1
    outs = [[] for _ in range(6)]
    for l in range(depth):
        m = _modulation(c_all, w_mod[l], b_mod[l])
        mod_p = m[:bp].reshape(bp, 1, N_MOD * d)
        mod_s = jnp.repeat(m[bp:bp + bs], ls, axis=0)

        w_in_bf = w_in[l].astype(BF16)
        yc_p, u_p, tail_p = _inproj_conv_prompt(xp, mod_p, lp, g_mix[l], w_in_bf, conv_w[l], 512)
        e_s = jnp.pad(state_conv[l], ((0, 0), (0, ls - 2), (0, 0))).reshape(ns_, dc)
        yc_s, u_s, z_s = _inproj_conv_sample(xs, mod_s, g_mix[l], w_in_bf, conv_w[l], e_s, 256)
        new_conv_p = tail_p[:, 6:8, :]
        new_conv_s = z_s.reshape(bs, ls, dc)[:, ls - 2:, :]

        ops = _s5_operators(ssm_a_re[l], ssm_a_im[l], ssm_log_dt[l], ssm_b_re[l], ssm_b_im[l],
                            ssm_c_re[l], ssm_c_im[l], ssm_d[l], glu_w[l], glu_b[l])
        h0 = jnp.concatenate([state_ssm_re[l], state_ssm_im[l]], axis=-1).transpose(1, 0, 2)
        ys_p, hf_p, ys_s, hf_s = _s5(u_p, u_s, h0, ops, bp, nc, ls)
        new_re_p = hf_p[:, :, :p_st].transpose(1, 0, 2)
        new_im_p = hf_p[:, :, p_st:].transpose(1, 0, 2)
        new_re_s = hf_s[:, :, :p_st].transpose(1, 0, 2)
        new_im_s = hf_s[:, :, p_st:].transpose(1, 0, 2)

        wo_bf = w_out[l].astype(BF16)
        wr_bf = w_router[l].astype(BF16)
        cnt0 = jnp.zeros((1, ne), F32)
        x1_p, h2_all, ti_p, gate_p, rank_p, cnt1 = _outproj(
            cnt0, xp, yc_p, ys_p, wo_bf, mod_p, False, lp, g_ffn[l], wr_bf, b_router[l],
            None, n_tok, 0, 2 * OUT_SUB)
        x1_s, h2_all, ti_s, gate_s, rank_s, cnt2 = _outproj(
            cnt1, xs, yc_s, ys_s, wo_bf, mod_s, True, ls, g_ffn[l], wr_bf, b_router[l],
            h2_all, n_tok, np_, OUT_SUB)

        counts = cnt2[0].astype(I32)
        top_i = jnp.concatenate([ti_p[:, :TOP_K], ti_s[:, :TOP_K]], axis=0)
        rank = jnp.concatenate([rank_p[:, :TOP_K], rank_s[:, :TOP_K]], axis=0)
        dest, slot_tok, blk_e, blk_rows, blk_chunks, blk_nrow, n_used = _moe_plan(
            counts, top_i, rank, n_tok, n_blocks)
        y_sorted = _moe_experts(h2_all, slot_tok, w1[l], b1[l], w2[l], b2[l],
                                blk_e, blk_rows, blk_chunks, blk_nrow, n_used)

        xp = _combine(x1_p, dest[:np_], y_sorted, gate_p, mod_p, False, lp, g_final, 256)
        xs = _combine(x1_s, dest[np_:], y_sorted, gate_s, mod_s, True, ls, g_final, 128)
        for lst, val in zip(outs, (new_conv_p, new_re_p, new_im_p, new_conv_s, new_re_s, new_im_s)):
            lst.append(val)

    y_prompt = xp.reshape(bp, lp, d)
    y_sample = xs.reshape(bs, ls, d)
    return (y_prompt, y_sample) + tuple(jnp.stack(o) for o in outs)
```

```python
import functools
import math

import jax
import jax.numpy as jnp
from jax import lax
from jax.experimental import pallas as pl
from jax.experimental.pallas import tpu as pltpu

F32 = jnp.float32
BF16 = jnp.bfloat16
I32 = jnp.int32

LANES = 128
EPS = 1e-6
N_MOD = 6
TOP_K = 4
SWIGLU_LIMIT = 7.0
SWIGLU_ALPHA = 1.702
GELU_C = math.sqrt(2.0 / math.pi)

S5_T = 16
OUT_SUB = 256
MOE_TB = 1792
MOE_CH = 256
MOE_TF = 512
MOE_TN = 512
VMEM_LIMIT = 56 * 1024 * 1024
HI = lax.Precision.HIGHEST


def _cparams(n_axes):
    return pltpu.CompilerParams(dimension_semantics=("arbitrary",) * n_axes,
                                vmem_limit_bytes=VMEM_LIMIT)


def _sigmoid(x):
    return 1.0 / (1.0 + jnp.exp(-x))


def _pack_bf16_pair(lo, hi):
    lo_bits = lax.bitcast_convert_type(lo, jnp.uint32) >> 16
    hi_bits = lax.bitcast_convert_type(hi, jnp.uint32) & jnp.uint32(0xFFFF0000)
    return lo_bits | hi_bits


def _unpack_bf16_pair(packed):
    lo = lax.bitcast_convert_type(packed << 16, F32)
    hi = lax.bitcast_convert_type(packed & jnp.uint32(0xFFFF0000), F32)
    return lo.astype(BF16), hi.astype(BF16)


def _resident(shape, index_map):
    return pl.BlockSpec(shape, index_map, pipeline_mode=pl.Buffered(1))


def _mod_body(c_ref, w_ref, b_ref, o_ref):
    c = c_ref[...]
    s = c * _sigmoid(c)
    o_ref[...] = jnp.dot(s.astype(BF16), w_ref[...].astype(BF16),
                         preferred_element_type=F32) + b_ref[...]


def _modulation(c_all, w_mod, b_mod):
    rows, d = c_all.shape
    n = w_mod.shape[1]
    tn = 1024
    return pl.pallas_call(
        _mod_body,
        grid=(n // tn,),
        in_specs=[pl.BlockSpec((rows, d), lambda j: (0, 0)),
                  pl.BlockSpec((d, tn), lambda j: (0, j)),
                  pl.BlockSpec((1, tn), lambda j: (0, j))],
        out_specs=pl.BlockSpec((rows, tn), lambda j: (0, j)),
        out_shape=jax.ShapeDtypeStruct((rows, n), F32),
        compiler_params=_cparams(1),
        name="modulation",
    )(c_all, w_mod, b_mod.reshape(1, n))


def _mod_spec(per_row, tm, rows_per_batch, d, col):
    if per_row:
        return pl.BlockSpec((tm, d), lambda i: (i, col))
    return pl.BlockSpec((None, 1, d), lambda i: ((i * tm) // rows_per_batch, 0, col))


def _inproj_pieces(x_ref, g_ref, sc_ref, sh_ref, w_ref, dc):
    x = x_ref[...]
    ms = jnp.mean(x * x, axis=-1, keepdims=True)
    h = x * lax.rsqrt(ms + EPS) * g_ref[...]
    hb = (h * (1.0 + sc_ref[...]) + sh_ref[...]).astype(BF16)
    b, c, v = (jnp.dot(hb, w_ref[:, k * dc:(k + 1) * dc], preferred_element_type=F32)
               for k in range(3))
    u = jnp.dot(hb, w_ref[:, 3 * dc:], preferred_element_type=F32)
    return b, c * v, u


def _conv_taps(bgate, z, z1, z2, w_ref):
    w = w_ref[...]
    return bgate * (w[0:1] * z2 + w[1:2] * z1 + w[2:3] * z)


def _inproj_conv_prompt_body(x_ref, g_ref, sc_ref, sh_ref, w_ref, cw_ref, y_ref, u_ref, tail_ref,
                             carry_ref, *, tiles_per_seq):
    @pl.when(pl.program_id(0) % tiles_per_seq == 0)
    def _():
        carry_ref[...] = jnp.zeros_like(carry_ref)

    bgate, z, u = _inproj_pieces(x_ref, g_ref, sc_ref, sh_ref, w_ref, y_ref.shape[1])
    u_ref[...] = u
    tt = z.shape[0]
    zc = jnp.concatenate([carry_ref[...], z], axis=0)
    z1 = pltpu.roll(zc, 1, 0)[8:]
    z2 = pltpu.roll(zc, 2, 0)[8:]
    y_ref[...] = _conv_taps(bgate, z, z1, z2, cw_ref).astype(y_ref.dtype)
    carry_ref[...] = z[tt - 8:]
    tail_ref[...] = z[tt - 8:]


def _inproj_conv_prompt(x, mod, seq, g, w_bf, conv_w, tm):
    n, d = x.shape
    dc = conv_w.shape[1]
    du = w_bf.shape[1] - 3 * dc
    row = lambda w: pl.BlockSpec((tm, w), lambda i: (i, 0))
    return pl.pallas_call(
        functools.partial(_inproj_conv_prompt_body, tiles_per_seq=seq // tm),
        grid=(n // tm,),
        in_specs=[row(d), pl.BlockSpec((1, d), lambda i: (0, 0)),
                  _mod_spec(False, tm, seq, d, 1), _mod_spec(False, tm, seq, d, 0),
                  _resident(w_bf.shape, lambda i: (0, 0)),
                  pl.BlockSpec(conv_w.shape, lambda i: (0, 0))],
        out_specs=[row(dc), row(du), pl.BlockSpec((None, 8, dc), lambda i: ((i * tm) // seq, 0, 0))],
        out_shape=[jax.ShapeDtypeStruct((n, dc), BF16), jax.ShapeDtypeStruct((n, du), F32),
                   jax.ShapeDtypeStruct((n // seq, 8, dc), F32)],
        scratch_shapes=[pltpu.VMEM((8, dc), F32)],
        compiler_params=_cparams(1),
        name="inproj_conv_prompt",
    )(x, g.reshape(1, d), mod, mod, w_bf, conv_w)


def _inproj_conv_sample_body(x_ref, g_ref, sc_ref, sh_ref, w_ref, cw_ref, e_ref, y_ref, u_ref, z_ref,
                             *, steps):
    bgate, z, u = _inproj_pieces(x_ref, g_ref, sc_ref, sh_ref, w_ref, y_ref.shape[1])
    u_ref[...] = u
    e = e_ref[...]
    rows = z.shape[0]
    assert steps & (steps - 1) == 0
    tpos = lax.broadcasted_iota(I32, z.shape, 0) & (steps - 1)
    z1 = jnp.where(tpos == 0, pltpu.roll(e, rows - 1, 0), pltpu.roll(z, 1, 0))
    z2 = jnp.where(tpos < 2, e, pltpu.roll(z, 2, 0))
    y_ref[...] = _conv_taps(bgate, z, z1, z2, cw_ref).astype(y_ref.dtype)
    z_ref[...] = z


def _inproj_conv_sample(x, mod, g, w_bf, conv_w, e, steps, tm):
    n, d = x.shape
    dc = conv_w.shape[1]
    du = w_bf.shape[1] - 3 * dc
    row = lambda w: pl.BlockSpec((tm, w), lambda i: (i, 0))
    return pl.pallas_call(
        functools.partial(_inproj_conv_sample_body, steps=steps),
        grid=(n // tm,),
        in_specs=[row(d), pl.BlockSpec((1, d), lambda i: (0, 0)),
                  _mod_spec(True, tm, 0, d, 1), _mod_spec(True, tm, 0, d, 0),
                  _resident(w_bf.shape, lambda i: (0, 0)),
                  pl.BlockSpec(conv_w.shape, lambda i: (0, 0)), row(dc)],
        out_specs=[row(dc), row(du), row(dc)],
        out_shape=[jax.ShapeDtypeStruct((n, dc), BF16), jax.ShapeDtypeStruct((n, du), F32),
                   jax.ShapeDtypeStruct((n, dc), F32)],
        compiler_params=_cparams(1),
        name="inproj_conv_sample",
    )(x, g.reshape(1, d), mod, mod, w_bf, conv_w, e)


def _s5_operators(a_re, a_im, log_dt, b_re, b_im, c_re, c_im, d_skip, glu_w, glu_b):
    g, p, ch = b_re.shape
    t = S5_T
    dt = jnp.exp(log_dt)[:, None]
    mag = jnp.exp(a_re * dt)
    ar = mag * jnp.cos(a_im * dt)
    ai = mag * jnp.sin(a_im * dt)
    den = a_re * a_re + a_im * a_im
    qr = ((ar - 1.0) * a_re + ai * a_im) / den
    qi = (ai * a_re - (ar - 1.0) * a_im) / den
    bbr = qr[..., None] * b_re - qi[..., None] * b_im
    bbi = qr[..., None] * b_im + qi[..., None] * b_re
    pr, pi = [jnp.ones_like(ar)], [jnp.zeros_like(ar)]
    for _ in range(t):
        pr, pi = pr + [pr[-1] * ar - pi[-1] * ai], pi + [pr[-1] * ai + pi[-1] * ar]
    pw_r, pw_i = jnp.stack(pr), jnp.stack(pi)
    car = c_re[None] * pw_r[:, :, None, :] - c_im[None] * pw_i[:, :, None, :]
    cai = c_re[None] * pw_i[:, :, None, :] + c_im[None] * pw_r[:, :, None, :]
    rev_r, rev_i = pw_r[t - 1::-1][:t], pw_i[t - 1::-1][:t]
    wsr = rev_r[..., None] * bbr[None] - rev_i[..., None] * bbi[None]
    wsi = rev_r[..., None] * bbi[None] + rev_i[..., None] * bbr[None]
    ws_op = jnp.concatenate([wsr.transpose(1, 0, 3, 2), wsi.transpose(1, 0, 3, 2)],
                            axis=-1).reshape(g, t * ch, 2 * p)
    wcr = car[1:].transpose(1, 3, 0, 2).reshape(g, p, t * ch)
    wci = cai[1:].transpose(1, 3, 0, 2).reshape(g, p, t * ch)
    wc_op = jnp.concatenate([wcr, -wci], axis=1)
    cmat_t = jnp.concatenate([c_re, -c_im], axis=-1)
    glu_t = jnp.swapaxes(glu_w, 1, 2)
    d_flat = jnp.tile(d_skip, (1, t))[:, None, :]
    gb_flat = jnp.tile(glu_b, (1, t))[:, None, :]

    def rot_tables(xr, xi):
        return jnp.concatenate([xr, xr], axis=-1), jnp.concatenate([-xi, xi], axis=-1)

    sr, si = pw_r[t], pw_i[t]
    p1s, p2s = [], []
    for _ in range(8):
        t1, t2 = rot_tables(sr, si)
        p1s.append(t1)
        p2s.append(t2)
        sr, si = sr * sr - si * si, 2.0 * sr * si
    h = t // 2
    p1_half, p2_half = rot_tables(pw_r[h], pw_i[h])
    return dict(
        ws=ws_op, wc=wc_op.astype(BF16), cmat=cmat_t, glu=glu_t,
        d=d_flat, gb=gb_flat, p1=jnp.stack(p1s, axis=1), p2=jnp.stack(p2s, axis=1),
        p1_half=p1_half[:, None, :], p2_half=p2_half[:, None, :])


def _gelu_tanh(y):
    return 0.5 * y * (1.0 + jnp.tanh(GELU_C * (y + 0.044715 * (y * y * y))))


def _dot_t(x, w_t):
    return lax.dot_general(x, w_t, (((1,), (1,)), ((), ())), preferred_element_type=F32)


def _block_toeplitz_t(seq, steps):
    ch, n = seq.shape
    lane = lax.broadcasted_iota(I32, seq.shape, 1)
    rows = []
    for t in range(steps):
        shift = (ch * (t + 1)) % n
        r = pltpu.roll(seq, shift, 1) if shift else seq
        rows.append(jnp.where(lane < ch * (t + 1), r, 0.0))
    return jnp.concatenate(rows, axis=0).astype(BF16)


def _s5_group_operators(ws, cmat, glu_t, steps=S5_T):
    n = ws.shape[0]
    seq = lax.dot_general(cmat, ws, (((1,), (1,)), ((), ())), precision=HI,
                          preferred_element_type=F32)
    ch = cmat.shape[0]
    glu_seq = jnp.concatenate([jnp.zeros((ch, n - ch), F32), glu_t], axis=1)
    return _block_toeplitz_t(seq, steps), _block_toeplitz_t(glu_seq, steps)


def _s5_tail(y, gl_t, gb):
    y = _gelu_tanh(y)
    gate = _dot_t(y.astype(BF16), gl_t) + gb
    return y * _sigmoid(gate)


S5_CH = 16
S5_OCT = LANES // S5_CH


def _fold_time(rows, blk):
    halves = [_block_transpose(rows[h:h + S5_OCT], blk) for h in range(0, len(rows), S5_OCT)]
    if len(halves) == 1:
        return halves[0]
    return [jnp.concatenate([hv[q] for hv in halves], axis=1) for q in range(S5_OCT)]


def _unfold_time(ys, blk):
    rows = []
    for h in range(ys[0].shape[1] // LANES):
        rows += _block_transpose([y[:, LANES * h:LANES * (h + 1)] for y in ys], blk)
    return rows


def _block_transpose(vs, blk):
    vs = list(vs)
    for d in (4, 2, 1):
        upper = (blk & d) != 0
        new = list(vs)
        for i in range(S5_OCT):
            if i & d:
                continue
            a, b = vs[i], vs[i + d]
            new[i] = jnp.where(upper, pltpu.roll(b, S5_CH * d, 1), a)
            new[i + d] = jnp.where(upper, b, pltpu.roll(a, LANES - S5_CH * d, 1))
        vs = new
    return vs


def _s5_body(u_ref, us_ref, h0_ref, ws_ref, wc_ref, cm_ref, gl_ref, d_ref, gb_ref, p1_ref, p2_ref,
             p1s_ref, p2s_ref, y_ref, hf_ref, ys_ref, hfs_ref, uf, yf, *, bsz, nc, steps_s):
    half = ws_ref.shape[-1] // 2
    seq = nc * S5_T
    blk = lax.broadcasted_iota(I32, (nc, LANES), 1) // S5_CH
    nseq = us_ref.shape[0] // steps_s
    ws_w = steps_s * S5_CH
    blk_s = lax.broadcasted_iota(I32, (nseq, LANES), 1) // S5_CH
    us = _fold_time([us_ref[pl.ds(t, nseq, stride=steps_s), :] for t in range(steps_s)], blk_s)
    ys_s = []

    def fold(b, carry):
        rows = [u_ref[pl.ds(pl.multiple_of(b * seq, seq) + t, nc, stride=S5_T), :]
                for t in range(S5_T)]
        for q, uq in enumerate(_fold_time(rows, blk)):
            uf[q, pl.ds(pl.multiple_of(b * nc, nc), nc), :] = uq
        return carry

    lax.fori_loop(0, bsz, fold, 0)

    for g in range(S5_OCT):
        u = uf[g]
        ub = u.astype(BF16)
        m_t, glu_t = _s5_group_operators(ws_ref[g], cm_ref[g], gl_ref[g])
        wsb = ws_ref[g].astype(BF16)
        yi = _dot_t(ub, m_t)
        e = jnp.dot(ub, wsb, preferred_element_type=F32)
        cpos = lax.broadcasted_iota(I32, e.shape, 0) & (nc - 1)
        p1 = p1_ref[g]
        p2 = p2_ref[g]
        z = jnp.where(cpos >= 1, pltpu.roll(e, 1, 0), 0.0)
        d, k = 1, 0
        while d < nc:
            zs = jnp.where(cpos >= d, pltpu.roll(z, d, 0), 0.0)
            z = z + zs * p1[k:k + 1] + pltpu.roll(zs, half, 1) * p2[k:k + 1]
            d, k = d * 2, k + 1
        yc = jnp.dot(z.astype(BF16), wc_ref[g], preferred_element_type=F32)
        yf[g] = _s5_tail(yi + yc + d_ref[g] * u, glu_t, gb_ref[g])
        hfin = e + z * p1[0:1] + pltpu.roll(z, half, 1) * p2[0:1]
        for b in range(bsz):
            r = b * nc + nc - 1
            hf_ref[g, b:b + 1, :] = hfin[r:r + 1, :]

        u8 = us[g]
        h0 = h0_ref[g]
        ub8 = u8.astype(BF16)
        yi8 = _dot_t(ub8, m_t[:ws_w, :ws_w])
        e8 = jnp.dot(ub8, wsb[wsb.shape[0] - ws_w:, :], preferred_element_type=F32)
        yc8 = jnp.dot(h0.astype(BF16), wc_ref[g][:, :ws_w], preferred_element_type=F32)
        ys_s.append(_s5_tail(yi8 + yc8 + d_ref[g][:, :ws_w] * u8, glu_t[:ws_w, :ws_w],
                             gb_ref[g][:, :ws_w]))
        hfs_ref[g] = e8 + h0 * p1s_ref[g] + pltpu.roll(h0, half, 1) * p2s_ref[g]

    for t, row in enumerate(_unfold_time(ys_s, blk_s)):
        ys_ref[pl.ds(t, nseq, stride=steps_s), :] = row

    def unfold(b, carry):
        ys = [yf[q, pl.ds(pl.multiple_of(b * nc, nc), nc), :] for q in range(S5_OCT)]
        for t, row in enumerate(_unfold_time(ys, blk)):
            y_ref[pl.ds(pl.multiple_of(b * seq, seq) + t, nc, stride=S5_T), :] = row
        return carry

    lax.fori_loop(0, bsz, unfold, 0)


def _s5(u, u_s, h0, ops, bsz, nc, steps_s):
    n, n_s = u.shape[0], u_s.shape[0]
    g, w, p2x = ops['ws'].shape
    assert nc & (nc - 1) == 0 and nc <= 2 ** (ops['p1'].shape[1] - 1) and steps_s == S5_OCT
    ch = ops['cmat'].shape[1]
    r_s = h0.shape[1]
    blk = lambda s1, s2: pl.BlockSpec((S5_OCT, s1, s2), lambda i: (i, 0, 0))
    col = lambda rows: pl.BlockSpec((rows, LANES), lambda i: (0, i))
    return pl.pallas_call(
        functools.partial(_s5_body, bsz=bsz, nc=nc, steps_s=steps_s),
        grid=(g // S5_OCT,),
        in_specs=[col(n), col(n_s), blk(r_s, p2x),
                  blk(w, p2x), blk(p2x, w), blk(ch, p2x), blk(ch, ch),
                  blk(1, w), blk(1, w), blk(8, p2x), blk(8, p2x), blk(1, p2x), blk(1, p2x)],
        out_specs=[col(n), blk(bsz, p2x), col(n_s), blk(r_s, p2x)],
        out_shape=[jax.ShapeDtypeStruct((n, g * S5_CH), F32),
                   jax.ShapeDtypeStruct((g, bsz, p2x), F32),
                   jax.ShapeDtypeStruct((n_s, g * S5_CH), F32),
                   jax.ShapeDtypeStruct((g, r_s, p2x), F32)],
        scratch_shapes=[pltpu.VMEM((S5_OCT, bsz * nc, w), F32),
                        pltpu.VMEM((S5_OCT, bsz * nc, w), F32)],
        compiler_params=_cparams(1),
        name="s5",
    )(u, u_s, h0, ops['ws'], ops['wc'], ops['cmat'], ops['glu'], ops['d'], ops['gb'],
      ops['p1'], ops['p2'], ops['p1_half'], ops['p2_half'])


def _outproj_body(cnt0_ref, x_ref, yc_ref, ys_ref, wo_ref, gt_ref, g_ref, sc_ref, sh_ref,
                  wr_ref, br_ref, *rest, has_prev):
    x1_ref, h2_ref, ti_ref, gate_ref, rank_ref, cnt_ref, run_ref = rest[1:] if has_prev else rest

    @pl.when(pl.program_id(0) == 0)
    def _():
        run_ref[...] = cnt0_ref[...]

    subs = [slice(r0, r0 + OUT_SUB) for r0 in range(0, x_ref.shape[0], OUT_SUB)]
    dc = yc_ref.shape[1]

    def mod(ref, rows):
        return ref[...] if ref.shape[0] == 1 else ref[rows, :]

    x1s = []
    for rows in subs:
        mix = (jnp.dot(yc_ref[rows, :], wo_ref[0:dc, :], preferred_element_type=F32)
               + jnp.dot(ys_ref[rows, :].astype(BF16), wo_ref[dc:, :], preferred_element_type=F32))
        x1s.append(x_ref[rows, :] + mod(gt_ref, rows) * mix)
    for rows, x1 in zip(subs, x1s):
        _route_rows(rows, x1, mod(sc_ref, rows), mod(sh_ref, rows), g_ref, wr_ref, br_ref,
                    x1_ref, h2_ref, ti_ref, gate_ref, rank_ref, run_ref)
    cnt_ref[...] = run_ref[...]


def _route_rows(rows, x1, sc, sh, g_ref, wr_ref, br_ref, x1_ref, h2_ref, ti_ref, gate_ref, rank_ref,
                run_ref):
    x1_ref[rows, :] = x1
    ms = jnp.mean(x1 * x1, axis=-1, keepdims=True)
    h = x1 * lax.rsqrt(ms + EPS) * g_ref[...]
    hb = (h * (1.0 + sc) + sh).astype(BF16)
    hb32 = hb.astype(F32)
    half = h2_ref.shape[1]
    h2_ref[rows, :] = _pack_bf16_pair(hb32[:, :half], hb32[:, half:])
    logits = jnp.dot(hb, wr_ref[...], preferred_element_type=F32) + br_ref[...]
    tm, ne = logits.shape
    lane = lax.broadcasted_iota(I32, logits.shape, 1).astype(F32)
    work = logits
    vals, ids, sels = [], [], []
    for _ in range(TOP_K):
        m = jnp.max(work, axis=1, keepdims=True)
        idx = jnp.min(jnp.where(work == m, lane, float(ne)), axis=1, keepdims=True)
        sel = lane == idx
        vals.append(m)
        ids.append(idx)
        sels.append(sel)
        work = jnp.where(sel, -jnp.inf, work)
    exps = [jnp.exp(v - vals[0]) for v in vals]
    tot = exps[0]
    for ex in exps[1:]:
        tot = tot + ex
    gates = [ex / tot for ex in exps]
    onehot = sels[0]
    for s in sels[1:]:
        onehot = onehot | s
    onehot = onehot.astype(F32)
    row = lax.broadcasted_iota(I32, (tm, tm), 0)
    col = lax.broadcasted_iota(I32, (tm, tm), 1)
    below = (col < row).astype(BF16)
    before = jnp.dot(below, onehot.astype(BF16), preferred_element_type=F32) + run_ref[...]
    ranks = [jnp.sum(jnp.where(s, before, 0.0), axis=1, keepdims=True) for s in sels]
    run_ref[...] = run_ref[...] + jnp.sum(onehot, axis=0, keepdims=True)

    wide = lax.broadcasted_iota(I32, (tm, ti_ref.shape[1]), 1)

    def spread(cols):
        out = cols[TOP_K - 1]
        for k in range(TOP_K - 2, -1, -1):
            out = jnp.where(wide == k, cols[k], out)
        return out

    ti_ref[rows, :] = spread(ids).astype(I32)
    gate_ref[rows, :] = spread(gates)
    rank_ref[rows, :] = spread(ranks).astype(I32)


def _outproj(cnt0, x, yc, ys, wo_bf, mod, per_row, rows_per_batch, g, wr_bf, br, h2_prev, n_all,
             row0, tm):
    n, d = x.shape
    dc = yc.shape[1]
    ne = wr_bf.shape[1]
    blk0 = row0 // tm
    row = lambda w: pl.BlockSpec((tm, w), lambda i: (i, 0))
    const = lambda s: pl.BlockSpec(s, lambda i: (0, 0))
    has_prev = h2_prev is not None
    in_specs = [const((1, ne)), row(d), row(dc), row(dc),
                _resident((d, d), lambda i: (0, 0)),
                _mod_spec(per_row, tm, rows_per_batch, d, 2), const((1, d)),
                _mod_spec(per_row, tm, rows_per_batch, d, 4),
                _mod_spec(per_row, tm, rows_per_batch, d, 3),
                const((d, ne)), const((1, ne))]
    args = [cnt0, x, yc, ys, wo_bf, mod, g.reshape(1, d), mod, mod, wr_bf, br.reshape(1, ne)]
    if has_prev:
        in_specs.append(pl.BlockSpec(memory_space=pl.ANY))
        args.append(h2_prev)
    return pl.pallas_call(
        functools.partial(_outproj_body, has_prev=has_prev),
        grid=(n // tm,),
        in_specs=in_specs,
        out_specs=[row(d), pl.BlockSpec((tm, d // 2), lambda i: (blk0 + i, 0)),
                   row(LANES), row(LANES), row(LANES), const((1, ne))],
        out_shape=[jax.ShapeDtypeStruct((n, d), F32),
                   jax.ShapeDtypeStruct((n_all, d // 2), jnp.uint32),
                   jax.ShapeDtypeStruct((n, LANES), I32), jax.ShapeDtypeStruct((n, LANES), F32),
                   jax.ShapeDtypeStruct((n, LANES), I32), jax.ShapeDtypeStruct((1, ne), F32)],
        scratch_shapes=[pltpu.VMEM((1, ne), F32)],
        input_output_aliases={len(args) - 1: 1} if has_prev else {},
        compiler_params=_cparams(1),
        name="outproj_router",
    )(*args)


def _moe_body(be_ref, rb_ref, nch_ref, nrow_ref, tok0_ref, tokn_ref, h2_hbm,
              w1g_ref, w1l_ref, b1g_ref, b1l_ref, w2_ref, b2_ref,
              o_ref, xp, act, sem, *, nf1):
    del be_ref, rb_ref
    b = pl.program_id(0)
    s = pl.program_id(1)
    n_chunks = nch_ref[b]

    def issue_rows(tok_ref, n_rows):
        def body(i, carry):
            r0 = pl.multiple_of(i * 8, 8)
            for u in range(8):
                pltpu.make_async_copy(h2_hbm.at[pl.ds(tok_ref[0, r0 + u], 1), :],
                                      xp.at[pl.ds(r0 + u, 1), :], sem.at[0]).start()
            return carry

        lax.fori_loop(0, n_rows // 8, body, 0)

    def wait_rows(n_rows):
        size = 8
        while size <= MOE_TB:
            @pl.when((n_rows & size) != 0)
            def _(size=size):
                pltpu.make_async_copy(h2_hbm.at[pl.ds(0, size), :], xp.at[pl.ds(0, size), :],
                                      sem.at[0]).wait()

            size *= 2

    @pl.when((b == 0) & (s == 0))
    def _():
        xp[...] = jnp.zeros_like(xp)
        issue_rows(tok0_ref, nrow_ref[0])

    @pl.when(s == 0)
    def _():
        wait_rows(nrow_ref[b])

    @pl.when(s == nf1)
    def _():
        issue_rows(tokn_ref, nrow_ref[b + 1])

    def chunk_groups(one):
        assert MOE_TB // MOE_CH < 8
        for size in (4, 2, 1):
            @pl.when((n_chunks & size) != 0)
            def _(size=size):
                r0 = (n_chunks & ~(2 * size - 1)) * MOE_CH
                one(pl.multiple_of(r0, size * MOE_CH), size * MOE_CH)

    @pl.when((s < nf1) & (n_chunks > 0))
    def _():
        def one(r0, rows):
            x = jnp.concatenate(_unpack_bf16_pair(xp[pl.ds(r0, rows), :]), axis=1)
            hg = jnp.dot(x, w1g_ref[...].astype(BF16), preferred_element_type=F32)
            hl = jnp.dot(x, w1l_ref[...].astype(BF16), preferred_element_type=F32)
            gl = jnp.minimum(hg + b1g_ref[...], SWIGLU_LIMIT)
            ln = jnp.clip(hl + b1l_ref[...], -SWIGLU_LIMIT, SWIGLU_LIMIT)
            a = gl * _sigmoid(SWIGLU_ALPHA * gl) * (ln + 1.0)
            act[s, pl.ds(r0, rows), :] = a.astype(BF16)

        chunk_groups(one)

    @pl.when((s >= nf1) & (n_chunks > 0))
    def _():
        def one(r0, rows):
            a = jnp.concatenate([act[k, pl.ds(r0, rows), :] for k in range(nf1)], axis=1)
            o_ref[pl.ds(r0, rows), :] = (
                jnp.dot(a, w2_ref[...].astype(BF16), preferred_element_type=F32) + b2_ref[...])

        chunk_groups(one)

        def zero(i, carry):
            r0 = pl.multiple_of(i * MOE_CH, MOE_CH)
            o_ref[pl.ds(r0, MOE_CH), :] = jnp.zeros((MOE_CH, o_ref.shape[1]), F32)
            return carry

        lax.fori_loop(n_chunks, MOE_TB // MOE_CH, zero, 0)


def _moe_experts(h2_all, slot_tok, w1, b1, w2, b2, blk_expert, blk_rows, blk_chunks, blk_nrow,
                 n_grid_blocks):
    ne, d, f2 = w1.shape
    f = f2 // 2
    nf1 = f // MOE_TF
    nf2 = d // MOE_TN
    nb = slot_tok.shape[0]

    def s1(b, s, nch):
        return jnp.where(nch[b] > 0, jnp.minimum(s, nf1 - 1), nf1 - 1)

    def s2(b, s, nch):
        return jnp.where(nch[b] > 0, jnp.maximum(s - nf1, 0), nf2 - 1)

    def w2_index(b, s, be, rb, nch, nr):
        early = (s < nf1 - 1) & (b > 0)
        e = jnp.where(early, be[jnp.maximum(b - 1, 0)], be[b])
        return e, 0, jnp.where(early, nf2 - 1, s2(b, s, nch))

    grid_spec = pltpu.PrefetchScalarGridSpec(
        num_scalar_prefetch=4,
        grid=(n_grid_blocks, nf1 + nf2),
        in_specs=[
            pl.BlockSpec((None, 1, MOE_TB), lambda b, s, be, rb, nch, nr: (0, 0, 0),
                         memory_space=pltpu.SMEM),
            pl.BlockSpec((None, 1, MOE_TB),
                         lambda b, s, be, rb, nch, nr: (jnp.minimum(b + 1, nb - 1), 0, 0),
                         memory_space=pltpu.SMEM),
            pl.BlockSpec(memory_space=pl.ANY),
            pl.BlockSpec((None, d, MOE_TF), lambda b, s, be, rb, nch, nr: (be[b], 0, s1(b, s, nch))),
            pl.BlockSpec((None, d, MOE_TF),
                         lambda b, s, be, rb, nch, nr: (be[b], 0, nf1 + s1(b, s, nch))),
            pl.BlockSpec((None, 1, MOE_TF), lambda b, s, be, rb, nch, nr: (be[b], 0, s1(b, s, nch))),
            pl.BlockSpec((None, 1, MOE_TF),
                         lambda b, s, be, rb, nch, nr: (be[b], 0, nf1 + s1(b, s, nch))),
            pl.BlockSpec((None, f, MOE_TN), w2_index),
            pl.BlockSpec((None, 1, MOE_TN), w2_index),
        ],
        out_specs=pl.BlockSpec((MOE_TB, MOE_TN),
                               lambda b, s, be, rb, nch, nr: (rb[b], s2(b, s, nch))),
        scratch_shapes=[pltpu.VMEM((MOE_TB, d // 2), jnp.uint32),
                        pltpu.VMEM((nf1, MOE_TB, MOE_TF), BF16),
                        pltpu.SemaphoreType.DMA((1,))],
    )
    return pl.pallas_call(
        functools.partial(_moe_body, nf1=nf1),
        grid_spec=grid_spec,
        out_shape=jax.ShapeDtypeStruct((nb * MOE_TB, d), F32),
        compiler_params=_cparams(2),
        name="moe_experts",
    )(blk_expert, blk_rows, blk_chunks, blk_nrow, slot_tok, slot_tok, h2_all,
      w1, w1, b1.reshape(ne, 1, f2), b1.reshape(ne, 1, f2), w2, b2.reshape(ne, 1, d))


def _moe_plan(counts, top_i, rank, n_tokens, n_blocks):
    ne = counts.shape[0]
    nblk = (counts + MOE_TB - 1) // MOE_TB
    blk_end = jnp.cumsum(nblk)
    blk_start = blk_end - nblk
    n_used = blk_end[-1]
    dest = blk_start[top_i] * MOE_TB + rank
    bidx = jnp.arange(n_blocks + 1, dtype=I32)
    be = jnp.minimum(jnp.searchsorted(blk_end, bidx, side='right'), ne - 1).astype(I32)
    active = bidx < n_used
    valid = jnp.where(active, jnp.clip(counts[be] - (bidx - blk_start[be]) * MOE_TB, 0, MOE_TB), 0)
    chunks = ((valid + MOE_CH - 1) // MOE_CH).astype(I32)
    nrow = ((valid + 7) // 8 * 8).astype(I32)
    last = jnp.maximum(n_used - 1, 0)
    be = jnp.where(active, be, be[last]).astype(I32)
    rows = jnp.where(active, bidx, last).astype(I32)
    tok = jnp.arange(n_tokens * TOP_K, dtype=I32) // TOP_K
    slot_tok = jnp.zeros((n_blocks * MOE_TB,), I32).at[dest.reshape(-1)].set(
        tok, unique_indices=True, mode='promise_in_bounds')
    return dest, slot_tok.reshape(n_blocks, 1, MOE_TB), be, rows, chunks, nrow, n_used


def _combine_body(d0_ref, dn_ref, x1_ref, gate_ref, gt_ref, gf_ref, ys_hbm, o_ref, ybuf, sem):
    i = pl.program_id(0)
    tm = x1_ref.shape[0]
    slot = lax.rem(i, 2)

    def issue_rows(dest_ref, sl):
        def body(i, carry):
            r0 = pl.multiple_of(i * 8, 8)
            for u in range(8):
                for k in range(TOP_K):
                    pltpu.make_async_copy(
                        ys_hbm.at[pl.ds(dest_ref[0, (r0 + u) * TOP_K + k], 1), :],
                        ybuf.at[sl, k, pl.ds(r0 + u, 1), :], sem.at[sl]).start()
            return carry

        lax.fori_loop(0, tm // 8, body, 0)

    @pl.when(i == 0)
    def _():
        issue_rows(d0_ref, 0)

    @pl.when(i + 1 < pl.num_programs(0))
    def _():
        issue_rows(dn_ref, 1 - slot)

    pltpu.make_async_copy(ybuf.at[slot], ybuf.at[slot], sem.at[slot]).wait()
    gates = gate_ref[...]
    y = gates[:, 0:1] * ybuf[slot, 0]
    for k in range(1, TOP_K):
        y = y + gates[:, k:k + 1] * ybuf[slot, k]
    x2 = x1_ref[...] + gt_ref[...] * y
    ms = jnp.mean(x2 * x2, axis=-1, keepdims=True)
    o_ref[...] = x2 * lax.rsqrt(ms + EPS) * gf_ref[...]


def _combine(x1, dest, ys, gates, mod, per_row, rows_per_batch, g_final, tm):
    n, d = x1.shape
    nt = n // tm
    dest3 = dest.reshape(nt, 1, tm * TOP_K)
    return pl.pallas_call(
        _combine_body,
        grid=(nt,),
        in_specs=[pl.BlockSpec((None, 1, tm * TOP_K), lambda i: (0, 0, 0), memory_space=pltpu.SMEM),
                  pl.BlockSpec((None, 1, tm * TOP_K), lambda i: (jnp.minimum(i + 1, nt - 1), 0, 0),
                               memory_space=pltpu.SMEM),
                  pl.BlockSpec((tm, d), lambda i: (i, 0)),
                  pl.BlockSpec((tm, LANES), lambda i: (i, 0)),
                  _mod_spec(per_row, tm, rows_per_batch, d, 5),
                  pl.BlockSpec((1, d), lambda i: (0, 0)),
                  pl.BlockSpec(memory_space=pl.ANY)],
        out_specs=pl.BlockSpec((tm, d), lambda i: (i, 0)),
        out_shape=jax.ShapeDtypeStruct((n, d), F32),
        scratch_shapes=[pltpu.VMEM((2, TOP_K, tm, d), F32), pltpu.SemaphoreType.DMA((2,))],
        compiler_params=_cparams(1),
        name="combine_norm",
    )(dest3, dest3, x1, gates, mod, g_final.reshape(1, d), ys)


def kernel(x_prompt, x_sample, c_prompt, c_sample, state_conv, state_ssm_re, state_ssm_im, g_mix, g_ffn, w_mod, b_mod, w_in, conv_w, ssm_a_re, ssm_a_im, ssm_log_dt, ssm_b_re, ssm_b_im, ssm_c_re, ssm_c_im, ssm_d, glu_w, glu_b, w_out, w_router, b_router, w1, b1, w2, b2, g_final):
    bp, lp, d = x_prompt.shape
    bs, ls, _ = x_sample.shape
    depth = g_mix.shape[0]
    dc = conv_w.shape[-1]
    n_grp, p_st, ch = ssm_b_re.shape[1:]
    ne = w_router.shape[-1]
    np_, ns_ = bp * lp, bs * ls
    n_tok = np_ + ns_
    assert ls == S5_T // 2 and lp % S5_T == 0 and conv_w.shape[1] == 3
    nc = lp // S5_T
    n_blocks = -(-(n_tok * TOP_K) // MOE_TB) + ne

    xp = x_prompt.reshape(np_, d)
    xs = x_sample.reshape(ns_, d)
    c_all = jnp.concatenate([c_prompt, c_sample], axis=0)
    pad = (-c_all.shape[0]) % 8
    c_all = jnp.pad(c_all, ((0, pad), (0, 0)))

    assert depth == 1
    outs = [[] for _ in range(6)]
    for l in range(depth):
        m = _modulation(c_all, w_mod[l], b_mod[l])
        mod_p = m[:bp].reshape(bp, 1, N_MOD * d)
        mod_s = jnp.repeat(m[bp:bp + bs], ls, axis=0)

        w_in_bf = w_in[l].astype(BF16)
        yc_p, u_p, tail_p = _inproj_conv_prompt(xp, mod_p, lp, g_mix[l], w_in_bf, conv_w[l], 512)
        e_s = jnp.pad(state_conv[l], ((0, 0), (0, ls - 2), (0, 0))).reshape(ns_, dc)
        yc_s, u_s, z_s = _inproj_conv_sample(xs, mod_s, g_mix[l], w_in_bf, conv_w[l], e_s, ls, 256)
        new_conv_p = tail_p[:, 6:8, :]
        new_conv_s = z_s.reshape(bs, ls, dc)[:, ls - 2:, :]

        ops = _s5_operators(ssm_a_re[l], ssm_a_im[l], ssm_log_dt[l], ssm_b_re[l], ssm_b_im[l],
                            ssm_c_re[l], ssm_c_im[l], ssm_d[l], glu_w[l], glu_b[l])
        h0 = jnp.concatenate([state_ssm_re[l], state_ssm_im[l]], axis=-1).transpose(1, 0, 2)
        ys_p, hf_p, ys_s, hf_s = _s5(u_p, u_s, h0, ops, bp, nc, ls)
        new_re_p = hf_p[:, :, :p_st].transpose(1, 0, 2)
        new_im_p = hf_p[:, :, p_st:].transpose(1, 0, 2)
        new_re_s = hf_s[:, :, :p_st].transpose(1, 0, 2)
        new_im_s = hf_s[:, :, p_st:].transpose(1, 0, 2)

        wo_bf = w_out[l].astype(BF16)
        wr_bf = w_router[l].astype(BF16)
        cnt0 = jnp.zeros((1, ne), F32)
        x1_p, h2_all, ti_p, gate_p, rank_p, cnt1 = _outproj(
            cnt0, xp, yc_p, ys_p, wo_bf, mod_p, False, lp, g_ffn[l], wr_bf, b_router[l],
            None, n_tok, 0, 2 * OUT_SUB)
        x1_s, h2_all, ti_s, gate_s, rank_s, cnt2 = _outproj(
            cnt1, xs, yc_s, ys_s, wo_bf, mod_s, True, ls, g_ffn[l], wr_bf, b_router[l],
            h2_all, n_tok, np_, OUT_SUB)

        counts = cnt2[0].astype(I32)
        top_i = jnp.concatenate([ti_p[:, :TOP_K], ti_s[:, :TOP_K]], axis=0)
        rank = jnp.concatenate([rank_p[:, :TOP_K], rank_s[:, :TOP_K]], axis=0)
        dest, slot_tok, blk_e, blk_rows, blk_chunks, blk_nrow, n_used = _moe_plan(
            counts, top_i, rank, n_tok, n_blocks)
        y_sorted = _moe_experts(h2_all, slot_tok, w1[l], b1[l], w2[l], b2[l],
                                blk_e, blk_rows, blk_chunks, blk_nrow, n_used)

        xp = _combine(x1_p, dest[:np_], y_sorted, gate_p, mod_p, False, lp, g_final, 256)
        xs = _combine(x1_s, dest[np_:], y_sorted, gate_s, mod_s, True, ls, g_final, 128)
        for lst, val in zip(outs, (new_conv_p, new_re_p, new_im_p, new_conv_s, new_re_s, new_im_s)):
            lst.append(val)

    y_prompt = xp.reshape(bp, lp, d)
    y_sample = xs.reshape(bs, ls, d)
    return (y_prompt, y_sample) + tuple(jnp.stack(o) for o in outs)
```

```python
import functools
import math

import jax
import jax.numpy as jnp
from jax import lax
from jax.experimental import pallas as pl
from jax.experimental.pallas import tpu as pltpu

F32 = jnp.float32
BF16 = jnp.bfloat16
I32 = jnp.int32

LANES = 128
EPS = 1e-6
N_MOD = 6
TOP_K = 4
SWIGLU_LIMIT = 7.0
SWIGLU_ALPHA = 1.702
GELU_C = math.sqrt(2.0 / math.pi)

S5_T = 16
OUT_SUB = 256
MOE_TB = 1792
MOE_CH = 256
MOE_TF = 512
MOE_TN = 512
VMEM_LIMIT = 56 * 1024 * 1024
HI = lax.Precision.HIGHEST


def _cparams(n_axes):
    return pltpu.CompilerParams(dimension_semantics=("arbitrary",) * n_axes,
                                vmem_limit_bytes=VMEM_LIMIT)


def _sigmoid(x):
    return 1.0 / (1.0 + jnp.exp(-x))


def _pack_bf16_pair(lo, hi):
    lo_bits = lax.bitcast_convert_type(lo, jnp.uint32) >> 16
    hi_bits = lax.bitcast_convert_type(hi, jnp.uint32) & jnp.uint32(0xFFFF0000)
    return lo_bits | hi_bits


def _unpack_bf16_pair(packed):
    lo = lax.bitcast_convert_type(packed << 16, F32)
    hi = lax.bitcast_convert_type(packed & jnp.uint32(0xFFFF0000), F32)
    return lo.astype(BF16), hi.astype(BF16)


def _resident(shape, index_map):
    return pl.BlockSpec(shape, index_map, pipeline_mode=pl.Buffered(1))


def _mod_body(c_ref, w_ref, b_ref, o_ref):
    c = c_ref[...]
    s = c * _sigmoid(c)
    o_ref[...] = jnp.dot(s.astype(BF16), w_ref[...].astype(BF16),
                         preferred_element_type=F32) + b_ref[...]


def _modulation(c_all, w_mod, b_mod):
    rows, d = c_all.shape
    n = w_mod.shape[1]
    tn = 1024
    return pl.pallas_call(
        _mod_body,
        grid=(n // tn,),
        in_specs=[pl.BlockSpec((rows, d), lambda j: (0, 0)),
                  pl.BlockSpec((d, tn), lambda j: (0, j)),
                  pl.BlockSpec((1, tn), lambda j: (0, j))],
        out_specs=pl.BlockSpec((rows, tn), lambda j: (0, j)),
        out_shape=jax.ShapeDtypeStruct((rows, n), F32),
        compiler_params=_cparams(1),
        name="modulation",
    )(c_all, w_mod, b_mod.reshape(1, n))


def _mod_spec(per_row, tm, rows_per_batch, d, col):
    if per_row:
        return pl.BlockSpec((tm, d), lambda i: (i, col))
    return pl.BlockSpec((None, 1, d), lambda i: ((i * tm) // rows_per_batch, 0, col))


def _inproj_pieces(x_ref, g_ref, sc_ref, sh_ref, w_ref, dc):
    x = x_ref[...]
    ms = jnp.mean(x * x, axis=-1, keepdims=True)
    h = x * lax.rsqrt(ms + EPS) * g_ref[...]
    hb = (h * (1.0 + sc_ref[...]) + sh_ref[...]).astype(BF16)
    b, c, v = (jnp.dot(hb, w_ref[:, k * dc:(k + 1) * dc], preferred_element_type=F32)
               for k in range(3))
    u = jnp.dot(hb, w_ref[:, 3 * dc:], preferred_element_type=F32)
    return b, c * v, u


def _conv_taps(bgate, z, z1, z2, w_ref):
    w = w_ref[...]
    return bgate * (w[0:1] * z2 + w[1:2] * z1 + w[2:3] * z)


def _inproj_conv_prompt_body(x_ref, g_ref, sc_ref, sh_ref, w_ref, cw_ref, y_ref, u_ref, tail_ref,
                             carry_ref, *, tiles_per_seq):
    @pl.when(pl.program_id(0) % tiles_per_seq == 0)
    def _():
        carry_ref[...] = jnp.zeros_like(carry_ref)

    bgate, z, u = _inproj_pieces(x_ref, g_ref, sc_ref, sh_ref, w_ref, y_ref.shape[1])
    u_ref[...] = u
    tt = z.shape[0]
    zc = jnp.concatenate([carry_ref[...], z], axis=0)
    z1 = pltpu.roll(zc, 1, 0)[8:]
    z2 = pltpu.roll(zc, 2, 0)[8:]
    y_ref[...] = _conv_taps(bgate, z, z1, z2, cw_ref).astype(y_ref.dtype)
    carry_ref[...] = z[tt - 8:]
    tail_ref[...] = z[tt - 8:]


def _inproj_conv_prompt(x, mod, seq, g, w_bf, conv_w, tm):
    n, d = x.shape
    dc = conv_w.shape[1]
    du = w_bf.shape[1] - 3 * dc
    row = lambda w: pl.BlockSpec((tm, w), lambda i: (i, 0))
    return pl.pallas_call(
        functools.partial(_inproj_conv_prompt_body, tiles_per_seq=seq // tm),
        grid=(n // tm,),
        in_specs=[row(d), pl.BlockSpec((1, d), lambda i: (0, 0)),
                  _mod_spec(False, tm, seq, d, 1), _mod_spec(False, tm, seq, d, 0),
                  _resident(w_bf.shape, lambda i: (0, 0)),
                  pl.BlockSpec(conv_w.shape, lambda i: (0, 0))],
        out_specs=[row(dc), row(du), pl.BlockSpec((None, 8, dc), lambda i: ((i * tm) // seq, 0, 0))],
        out_shape=[jax.ShapeDtypeStruct((n, dc), BF16), jax.ShapeDtypeStruct((n, du), F32),
                   jax.ShapeDtypeStruct((n // seq, 8, dc), F32)],
        scratch_shapes=[pltpu.VMEM((8, dc), F32)],
        compiler_params=_cparams(1),
        name="inproj_conv_prompt",
    )(x, g.reshape(1, d), mod, mod, w_bf, conv_w)


def _inproj_conv_sample_body(x_ref, g_ref, sc_ref, sh_ref, w_ref, cw_ref, e_ref, y_ref, u_ref, z_ref,
                             *, steps):
    bgate, z, u = _inproj_pieces(x_ref, g_ref, sc_ref, sh_ref, w_ref, y_ref.shape[1])
    u_ref[...] = u
    e = e_ref[...]
    rows = z.shape[0]
    assert steps & (steps - 1) == 0
    tpos = lax.broadcasted_iota(I32, z.shape, 0) & (steps - 1)
    z1 = jnp.where(tpos == 0, pltpu.roll(e, rows - 1, 0), pltpu.roll(z, 1, 0))
    z2 = jnp.where(tpos < 2, e, pltpu.roll(z, 2, 0))
    y_ref[...] = _conv_taps(bgate, z, z1, z2, cw_ref).astype(y_ref.dtype)
    z_ref[...] = z


def _inproj_conv_sample(x, mod, g, w_bf, conv_w, e, steps, tm):
    n, d = x.shape
    dc = conv_w.shape[1]
    du = w_bf.shape[1] - 3 * dc
    row = lambda w: pl.BlockSpec((tm, w), lambda i: (i, 0))
    return pl.pallas_call(
        functools.partial(_inproj_conv_sample_body, steps=steps),
        grid=(n // tm,),
        in_specs=[row(d), pl.BlockSpec((1, d), lambda i: (0, 0)),
                  _mod_spec(True, tm, 0, d, 1), _mod_spec(True, tm, 0, d, 0),
                  _resident(w_bf.shape, lambda i: (0, 0)),
                  pl.BlockSpec(conv_w.shape, lambda i: (0, 0)), row(dc)],
        out_specs=[row(dc), row(du), row(dc)],
        out_shape=[jax.ShapeDtypeStruct((n, dc), BF16), jax.ShapeDtypeStruct((n, du), F32),
                   jax.ShapeDtypeStruct((n, dc), F32)],
        compiler_params=_cparams(1),
        name="inproj_conv_sample",
    )(x, g.reshape(1, d), mod, mod, w_bf, conv_w, e)


def _s5_operators(a_re, a_im, log_dt, b_re, b_im, c_re, c_im, d_skip, glu_w, glu_b):
    g, p, ch = b_re.shape
    t = S5_T
    dt = jnp.exp(log_dt)[:, None]
    mag = jnp.exp(a_re * dt)
    ar = mag * jnp.cos(a_im * dt)
    ai = mag * jnp.sin(a_im * dt)
    den = a_re * a_re + a_im * a_im
    qr = ((ar - 1.0) * a_re + ai * a_im) / den
    qi = (ai * a_re - (ar - 1.0) * a_im) / den
    bbr = qr[..., None] * b_re - qi[..., None] * b_im
    bbi = qr[..., None] * b_im + qi[..., None] * b_re
    pr, pi = [jnp.ones_like(ar)], [jnp.zeros_like(ar)]
    for _ in range(t):
        pr, pi = pr + [pr[-1] * ar - pi[-1] * ai], pi + [pr[-1] * ai + pi[-1] * ar]
    pw_r, pw_i = jnp.stack(pr), jnp.stack(pi)
    car = c_re[None] * pw_r[:, :, None, :] - c_im[None] * pw_i[:, :, None, :]
    cai = c_re[None] * pw_i[:, :, None, :] + c_im[None] * pw_r[:, :, None, :]
    rev_r, rev_i = pw_r[t - 1::-1][:t], pw_i[t - 1::-1][:t]
    wsr = rev_r[..., None] * bbr[None] - rev_i[..., None] * bbi[None]
    wsi = rev_r[..., None] * bbi[None] + rev_i[..., None] * bbr[None]
    ws_op = jnp.concatenate([wsr.transpose(1, 0, 3, 2), wsi.transpose(1, 0, 3, 2)],
                            axis=-1).reshape(g, t * ch, 2 * p)
    wcr = car[1:].transpose(1, 3, 0, 2).reshape(g, p, t * ch)
    wci = cai[1:].transpose(1, 3, 0, 2).reshape(g, p, t * ch)
    wc_op = jnp.concatenate([wcr, -wci], axis=1)
    cmat_t = jnp.concatenate([c_re, -c_im], axis=-1)
    glu_t = jnp.swapaxes(glu_w, 1, 2)
    d_flat = jnp.tile(d_skip, (1, t))[:, None, :]
    gb_flat = jnp.tile(glu_b, (1, t))[:, None, :]

    def rot_tables(xr, xi):
        return jnp.concatenate([xr, xr], axis=-1), jnp.concatenate([-xi, xi], axis=-1)

    sr, si = pw_r[t], pw_i[t]
    p1s, p2s = [], []
    for _ in range(8):
        t1, t2 = rot_tables(sr, si)
        p1s.append(t1)
        p2s.append(t2)
        sr, si = sr * sr - si * si, 2.0 * sr * si
    h = t // 2
    p1_half, p2_half = rot_tables(pw_r[h], pw_i[h])
    return dict(
        ws=ws_op, wc=wc_op.astype(BF16), cmat=cmat_t, glu=glu_t,
        d=d_flat, gb=gb_flat, p1=jnp.stack(p1s, axis=1), p2=jnp.stack(p2s, axis=1),
        p1_half=p1_half[:, None, :], p2_half=p2_half[:, None, :])


def _gelu_tanh(y):
    return 0.5 * y * (1.0 + jnp.tanh(GELU_C * (y + 0.044715 * (y * y * y))))


def _dot_t(x, w_t):
    return lax.dot_general(x, w_t, (((1,), (1,)), ((), ())), preferred_element_type=F32)


def _block_toeplitz_t(seq, steps):
    ch, n = seq.shape
    lane = lax.broadcasted_iota(I32, seq.shape, 1)
    rows = []
    for t in range(steps):
        shift = (ch * (t + 1)) % n
        r = pltpu.roll(seq, shift, 1) if shift else seq
        rows.append(jnp.where(lane < ch * (t + 1), r, 0.0))
    return jnp.concatenate(rows, axis=0).astype(BF16)


def _s5_group_operators(ws, cmat, glu_t, steps=S5_T):
    n = ws.shape[0]
    seq = lax.dot_general(cmat, ws, (((1,), (1,)), ((), ())), precision=HI,
                          preferred_element_type=F32)
    ch = cmat.shape[0]
    glu_seq = jnp.concatenate([jnp.zeros((ch, n - ch), F32), glu_t], axis=1)
    return _block_toeplitz_t(seq, steps), _block_toeplitz_t(glu_seq, steps)


def _s5_tail(y, gl_t, gb):
    y = _gelu_tanh(y)
    gate = _dot_t(y.astype(BF16), gl_t) + gb
    return y * _sigmoid(gate)


S5_CH = 16
S5_OCT = LANES // S5_CH


def _fold_time(rows, blk):
    halves = [_block_transpose(rows[h:h + S5_OCT], blk) for h in range(0, len(rows), S5_OCT)]
    if len(halves) == 1:
        return halves[0]
    return [jnp.concatenate([hv[q] for hv in halves], axis=1) for q in range(S5_OCT)]


def _unfold_time(ys, blk):
    rows = []
    for h in range(ys[0].shape[1] // LANES):
        rows += _block_transpose([y[:, LANES * h:LANES * (h + 1)] for y in ys], blk)
    return rows


def _block_transpose(vs, blk):
    vs = list(vs)
    for d in (4, 2, 1):
        upper = (blk & d) != 0
        new = list(vs)
        for i in range(S5_OCT):
            if i & d:
                continue
            a, b = vs[i], vs[i + d]
            new[i] = jnp.where(upper, pltpu.roll(b, S5_CH * d, 1), a)
            new[i + d] = jnp.where(upper, b, pltpu.roll(a, LANES - S5_CH * d, 1))
        vs = new
    return vs


def _s5_body(u_ref, us_ref, h0_ref, ws_ref, wc_ref, cm_ref, gl_ref, d_ref, gb_ref, p1_ref, p2_ref,
             p1s_ref, p2s_ref, y_ref, hf_ref, ys_ref, hfs_ref, uf, yf, *, bsz, nc, steps_s):
    half = ws_ref.shape[-1] // 2
    seq = nc * S5_T
    blk = lax.broadcasted_iota(I32, (nc, LANES), 1) // S5_CH
    nseq = us_ref.shape[0] // steps_s
    ws_w = steps_s * S5_CH
    blk_s = lax.broadcasted_iota(I32, (nseq, LANES), 1) // S5_CH
    us = _fold_time([us_ref[pl.ds(t, nseq, stride=steps_s), :] for t in range(steps_s)], blk_s)
    ys_s = []

    def fold(b, carry):
        rows = [u_ref[pl.ds(pl.multiple_of(b * seq, seq) + t, nc, stride=S5_T), :]
                for t in range(S5_T)]
        for q, uq in enumerate(_fold_time(rows, blk)):
            uf[q, pl.ds(pl.multiple_of(b * nc, nc), nc), :] = uq
        return carry

    lax.fori_loop(0, bsz, fold, 0)

    for g in range(S5_OCT):
        u = uf[g]
        ub = u.astype(BF16)
        m_t, glu_t = _s5_group_operators(ws_ref[g], cm_ref[g], gl_ref[g])
        wsb = ws_ref[g].astype(BF16)
        yi = _dot_t(ub, m_t)
        e = jnp.dot(ub, wsb, preferred_element_type=F32)
        cpos = lax.broadcasted_iota(I32, e.shape, 0) & (nc - 1)
        p1 = p1_ref[g]
        p2 = p2_ref[g]
        z = jnp.where(cpos >= 1, pltpu.roll(e, 1, 0), 0.0)
        d, k = 1, 0
        while d < nc:
            zs = jnp.where(cpos >= d, pltpu.roll(z, d, 0), 0.0)
            z = z + zs * p1[k:k + 1] + pltpu.roll(zs, half, 1) * p2[k:k + 1]
            d, k = d * 2, k + 1
        yc = jnp.dot(z.astype(BF16), wc_ref[g], preferred_element_type=F32)
        yf[g] = _s5_tail(yi + yc + d_ref[g] * u, glu_t, gb_ref[g])
        hfin = e + z * p1[0:1] + pltpu.roll(z, half, 1) * p2[0:1]
        for b in range(bsz):
            r = b * nc + nc - 1
            hf_ref[g, b:b + 1, :] = hfin[r:r + 1, :]

        u8 = us[g]
        h0 = h0_ref[g]
        ub8 = u8.astype(BF16)
        yi8 = _dot_t(ub8, m_t[:ws_w, :ws_w])
        e8 = jnp.dot(ub8, wsb[wsb.shape[0] - ws_w:, :], preferred_element_type=F32)
        yc8 = jnp.dot(h0.astype(BF16), wc_ref[g][:, :ws_w], preferred_element_type=F32)
        ys_s.append(_s5_tail(yi8 + yc8 + d_ref[g][:, :ws_w] * u8, glu_t[:ws_w, :ws_w],
                             gb_ref[g][:, :ws_w]))
        hfs_ref[g] = e8 + h0 * p1s_ref[g] + pltpu.roll(h0, half, 1) * p2s_ref[g]

    for t, row in enumerate(_unfold_time(ys_s, blk_s)):
        ys_ref[pl.ds(t, nseq, stride=steps_s), :] = row

    def unfold(b, carry):
        ys = [yf[q, pl.ds(pl.multiple_of(b * nc, nc), nc), :] for q in range(S5_OCT)]
        for t, row in enumerate(_unfold_time(ys, blk)):
            y_ref[pl.ds(pl.multiple_of(b * seq, seq) + t, nc, stride=S5_T), :] = row
        return carry

    lax.fori_loop(0, bsz, unfold, 0)


def _s5(u, u_s, h0, ops, bsz, nc, steps_s):
    n, n_s = u.shape[0], u_s.shape[0]
    g, w, p2x = ops['ws'].shape
    assert nc & (nc - 1) == 0 and nc <= 2 ** (ops['p1'].shape[1] - 1) and steps_s == S5_OCT
    ch = ops['cmat'].shape[1]
    r_s = h0.shape[1]
    blk = lambda s1, s2: pl.BlockSpec((S5_OCT, s1, s2), lambda i: (i, 0, 0))
    col = lambda rows: pl.BlockSpec((rows, LANES), lambda i: (0, i))
    return pl.pallas_call(
        functools.partial(_s5_body, bsz=bsz, nc=nc, steps_s=steps_s),
        grid=(g // S5_OCT,),
        in_specs=[col(n), col(n_s), blk(r_s, p2x),
                  blk(w, p2x), blk(p2x, w), blk(ch, p2x), blk(ch, ch),
                  blk(1, w), blk(1, w), blk(8, p2x), blk(8, p2x), blk(1, p2x), blk(1, p2x)],
        out_specs=[col(n), blk(bsz, p2x), col(n_s), blk(r_s, p2x)],
        out_shape=[jax.ShapeDtypeStruct((n, g * S5_CH), F32),
                   jax.ShapeDtypeStruct((g, bsz, p2x), F32),
                   jax.ShapeDtypeStruct((n_s, g * S5_CH), F32),
                   jax.ShapeDtypeStruct((g, r_s, p2x), F32)],
        scratch_shapes=[pltpu.VMEM((S5_OCT, bsz * nc, w), F32),
                        pltpu.VMEM((S5_OCT, bsz * nc, w), F32)],
        compiler_params=_cparams(1),
        name="s5",
    )(u, u_s, h0, ops['ws'], ops['wc'], ops['cmat'], ops['glu'], ops['d'], ops['gb'],
      ops['p1'], ops['p2'], ops['p1_half'], ops['p2_half'])


def _outproj_body(cnt0_ref, x_ref, yc_ref, ys_ref, wo_ref, gt_ref, g_ref, sc_ref, sh_ref,
                  wr_ref, br_ref, *rest, has_prev):
    x1_ref, h2_ref, ti_ref, gate_ref, rank_ref, cnt_ref, run_ref = rest[1:] if has_prev else rest

    @pl.when(pl.program_id(0) == 0)
    def _():
        run_ref[...] = cnt0_ref[...]

    subs = [slice(r0, r0 + OUT_SUB) for r0 in range(0, x_ref.shape[0], OUT_SUB)]
    dc = yc_ref.shape[1]

    def mod(ref, rows):
        return ref[...] if ref.shape[0] == 1 else ref[rows, :]

    x1s = []
    for rows in subs:
        mix = (jnp.dot(yc_ref[rows, :], wo_ref[0:dc, :], preferred_element_type=F32)
               + jnp.dot(ys_ref[rows, :].astype(BF16), wo_ref[dc:, :], preferred_element_type=F32))
        x1s.append(x_ref[rows, :] + mod(gt_ref, rows) * mix)
    for rows, x1 in zip(subs, x1s):
        _route_rows(rows, x1, mod(sc_ref, rows), mod(sh_ref, rows), g_ref, wr_ref, br_ref,
                    x1_ref, h2_ref, ti_ref, gate_ref, rank_ref, run_ref)
    cnt_ref[...] = run_ref[...]


def _route_rows(rows, x1, sc, sh, g_ref, wr_ref, br_ref, x1_ref, h2_ref, ti_ref, gate_ref, rank_ref,
                run_ref):
    x1_ref[rows, :] = x1
    ms = jnp.mean(x1 * x1, axis=-1, keepdims=True)
    h = x1 * lax.rsqrt(ms + EPS) * g_ref[...]
    hb = (h * (1.0 + sc) + sh).astype(BF16)
    hb32 = hb.astype(F32)
    half = h2_ref.shape[1]
    h2_ref[rows, :] = _pack_bf16_pair(hb32[:, :half], hb32[:, half:])
    logits = jnp.dot(hb, wr_ref[...], preferred_element_type=F32) + br_ref[...]
    tm, ne = logits.shape
    lane = lax.broadcasted_iota(I32, logits.shape, 1).astype(F32)
    work = logits
    vals, ids, sels = [], [], []
    for _ in range(TOP_K):
        m = jnp.max(work, axis=1, keepdims=True)
        idx = jnp.min(jnp.where(work == m, lane, float(ne)), axis=1, keepdims=True)
        sel = lane == idx
        vals.append(m)
        ids.append(idx)
        sels.append(sel)
        work = jnp.where(sel, -jnp.inf, work)
    exps = [jnp.exp(v - vals[0]) for v in vals]
    tot = exps[0]
    for ex in exps[1:]:
        tot = tot + ex
    gates = [ex / tot for ex in exps]
    onehot = sels[0]
    for s in sels[1:]:
        onehot = onehot | s
    onehot = onehot.astype(F32)
    row = lax.broadcasted_iota(I32, (tm, tm), 0)
    col = lax.broadcasted_iota(I32, (tm, tm), 1)
    below = (col < row).astype(BF16)
    before = jnp.dot(below, onehot.astype(BF16), preferred_element_type=F32) + run_ref[...]
    ranks = [jnp.sum(jnp.where(s, before, 0.0), axis=1, keepdims=True) for s in sels]
    run_ref[...] = run_ref[...] + jnp.sum(onehot, axis=0, keepdims=True)

    wide = lax.broadcasted_iota(I32, (tm, ti_ref.shape[1]), 1)

    def spread(cols):
        out = cols[TOP_K - 1]
        for k in range(TOP_K - 2, -1, -1):
            out = jnp.where(wide == k, cols[k], out)
        return out

    ti_ref[rows, :] = spread(ids).astype(I32)
    gate_ref[rows, :] = spread(gates)
    rank_ref[rows, :] = spread(ranks).astype(I32)


def _outproj(cnt0, x, yc, ys, wo_bf, mod, per_row, rows_per_batch, g, wr_bf, br, h2_prev, n_all,
             row0, tm):
    n, d = x.shape
    dc = yc.shape[1]
    ne = wr_bf.shape[1]
    blk0 = row0 // tm
    row = lambda w: pl.BlockSpec((tm, w), lambda i: (i, 0))
    const = lambda s: pl.BlockSpec(s, lambda i: (0, 0))
    has_prev = h2_prev is not None
    in_specs = [const((1, ne)), row(d), row(dc), row(dc),
                _resident((d, d), lambda i: (0, 0)),
                _mod_spec(per_row, tm, rows_per_batch, d, 2), const((1, d)),
                _mod_spec(per_row, tm, rows_per_batch, d, 4),
                _mod_spec(per_row, tm, rows_per_batch, d, 3),
                const((d, ne)), const((1, ne))]
    args = [cnt0, x, yc, ys, wo_bf, mod, g.reshape(1, d), mod, mod, wr_bf, br.reshape(1, ne)]
    if has_prev:
        in_specs.append(pl.BlockSpec(memory_space=pl.ANY))
        args.append(h2_prev)
    return pl.pallas_call(
        functools.partial(_outproj_body, has_prev=has_prev),
        grid=(n // tm,),
        in_specs=in_specs,
        out_specs=[row(d), pl.BlockSpec((tm, d // 2), lambda i: (blk0 + i, 0)),
                   row(LANES), row(LANES), row(LANES), const((1, ne))],
        out_shape=[jax.ShapeDtypeStruct((n, d), F32),
                   jax.ShapeDtypeStruct((n_all, d // 2), jnp.uint32),
                   jax.ShapeDtypeStruct((n, LANES), I32), jax.ShapeDtypeStruct((n, LANES), F32),
                   jax.ShapeDtypeStruct((n, LANES), I32), jax.ShapeDtypeStruct((1, ne), F32)],
        scratch_shapes=[pltpu.VMEM((1, ne), F32)],
        input_output_aliases={len(args) - 1: 1} if has_prev else {},
        compiler_params=_cparams(1),
        name="outproj_router",
    )(*args)


def _moe_body(be_ref, rb_ref, nch_ref, nrow_ref, tok0_ref, tokn_ref, h2_hbm,
              w1g_ref, w1l_ref, b1g_ref, b1l_ref, w2_ref, b2_ref,
              o_ref, xp, act, sem, *, nf1):
    del be_ref, rb_ref
    b = pl.program_id(0)
    s = pl.program_id(1)
    n_chunks = nch_ref[b]

    def issue_rows(tok_ref, n_rows):
        def body(i, carry):
            r0 = pl.multiple_of(i * 8, 8)
            for u in range(8):
                pltpu.make_async_copy(h2_hbm.at[pl.ds(tok_ref[0, r0 + u], 1), :],
                                      xp.at[pl.ds(r0 + u, 1), :], sem.at[0]).start(priority=1)
            return carry

        lax.fori_loop(0, n_rows // 8, body, 0)

    def wait_rows(n_rows):
        size = 8
        while size <= MOE_TB:
            @pl.when((n_rows & size) != 0)
            def _(size=size):
                pltpu.make_async_copy(h2_hbm.at[pl.ds(0, size), :], xp.at[pl.ds(0, size), :],
                                      sem.at[0]).wait()

            size *= 2

    @pl.when((b == 0) & (s == 0))
    def _():
        xp[...] = jnp.zeros_like(xp)
        issue_rows(tok0_ref, nrow_ref[0])

    @pl.when(s == 0)
    def _():
        wait_rows(nrow_ref[b])

    @pl.when(s == nf1)
    def _():
        issue_rows(tokn_ref, nrow_ref[b + 1])

    def chunk_groups(one):
        assert MOE_TB // MOE_CH < 8
        for size in (4, 2, 1):
            @pl.when((n_chunks & size) != 0)
            def _(size=size):
                r0 = (n_chunks & ~(2 * size - 1)) * MOE_CH
                one(pl.multiple_of(r0, size * MOE_CH), size * MOE_CH)

    @pl.when((s < nf1) & (n_chunks > 0))
    def _():
        def one(r0, rows):
            x = jnp.concatenate(_unpack_bf16_pair(xp[pl.ds(r0, rows), :]), axis=1)
            hg = jnp.dot(x, w1g_ref[...].astype(BF16), preferred_element_type=F32)
            hl = jnp.dot(x, w1l_ref[...].astype(BF16), preferred_element_type=F32)
            gl = jnp.minimum(hg + b1g_ref[...], SWIGLU_LIMIT)
            ln = jnp.clip(hl + b1l_ref[...], -SWIGLU_LIMIT, SWIGLU_LIMIT)
            a = gl * _sigmoid(SWIGLU_ALPHA * gl) * (ln + 1.0)
            act[s, pl.ds(r0, rows), :] = a.astype(BF16)

        chunk_groups(one)

    @pl.when((s >= nf1) & (n_chunks > 0))
    def _():
        def one(r0, rows):
            a = jnp.concatenate([act[k, pl.ds(r0, rows), :] for k in range(nf1)], axis=1)
            o_ref[pl.ds(r0, rows), :] = (
                jnp.dot(a, w2_ref[...].astype(BF16), preferred_element_type=F32) + b2_ref[...])

        chunk_groups(one)

        def zero(i, carry):
            r0 = pl.multiple_of(i * MOE_CH, MOE_CH)
            o_ref[pl.ds(r0, MOE_CH), :] = jnp.zeros((MOE_CH, o_ref.shape[1]), F32)
            return carry

        lax.fori_loop(n_chunks, MOE_TB // MOE_CH, zero, 0)


def _moe_experts(h2_all, slot_tok, w1, b1, w2, b2, blk_expert, blk_rows, blk_chunks, blk_nrow,
                 n_grid_blocks):
    ne, d, f2 = w1.shape
    f = f2 // 2
    nf1 = f // MOE_TF
    nf2 = d // MOE_TN
    nb = slot_tok.shape[0]

    def s1(b, s, nch):
        return jnp.where(nch[b] > 0, jnp.minimum(s, nf1 - 1), nf1 - 1)

    def s2(b, s, nch):
        return jnp.where(nch[b] > 0, jnp.maximum(s - nf1, 0), nf2 - 1)

    def w2_index(b, s, be, rb, nch, nr):
        early = (s < nf1 - 1) & (b > 0)
        e = jnp.where(early, be[jnp.maximum(b - 1, 0)], be[b])
        return e, 0, jnp.where(early, nf2 - 1, s2(b, s, nch))

    grid_spec = pltpu.PrefetchScalarGridSpec(
        num_scalar_prefetch=4,
        grid=(n_grid_blocks, nf1 + nf2),
        in_specs=[
            pl.BlockSpec((None, 1, MOE_TB), lambda b, s, be, rb, nch, nr: (0, 0, 0),
                         memory_space=pltpu.SMEM),
            pl.BlockSpec((None, 1, MOE_TB),
                         lambda b, s, be, rb, nch, nr: (jnp.minimum(b + 1, nb - 1), 0, 0),
                         memory_space=pltpu.SMEM),
            pl.BlockSpec(memory_space=pl.ANY),
            pl.BlockSpec((None, d, MOE_TF), lambda b, s, be, rb, nch, nr: (be[b], 0, s1(b, s, nch))),
            pl.BlockSpec((None, d, MOE_TF),
                         lambda b, s, be, rb, nch, nr: (be[b], 0, nf1 + s1(b, s, nch))),
            pl.BlockSpec((None, 1, MOE_TF), lambda b, s, be, rb, nch, nr: (be[b], 0, s1(b, s, nch))),
            pl.BlockSpec((None, 1, MOE_TF),
                         lambda b, s, be, rb, nch, nr: (be[b], 0, nf1 + s1(b, s, nch))),
            pl.BlockSpec((None, f, MOE_TN), w2_index),
            pl.BlockSpec((None, 1, MOE_TN), w2_index),
        ],
        out_specs=pl.BlockSpec((MOE_TB, MOE_TN),
                               lambda b, s, be, rb, nch, nr: (rb[b], s2(b, s, nch))),
        scratch_shapes=[pltpu.VMEM((MOE_TB, d // 2), jnp.uint32),
                        pltpu.VMEM((nf1, MOE_TB, MOE_TF), BF16),
                        pltpu.SemaphoreType.DMA((1,))],
    )
    return pl.pallas_call(
        functools.partial(_moe_body, nf1=nf1),
        grid_spec=grid_spec,
        out_shape=jax.ShapeDtypeStruct((nb * MOE_TB, d), F32),
        compiler_params=_cparams(2),
        name="moe_experts",
    )(blk_expert, blk_rows, blk_chunks, blk_nrow, slot_tok, slot_tok, h2_all,
      w1, w1, b1.reshape(ne, 1, f2), b1.reshape(ne, 1, f2), w2, b2.reshape(ne, 1, d))


def _moe_plan(counts, top_i, rank, n_tokens, n_blocks):
    ne = counts.shape[0]
    nblk = (counts + MOE_TB - 1) // MOE_TB
    blk_end = jnp.cumsum(nblk)
    blk_start = blk_end - nblk
    n_used = blk_end[-1]
    dest = blk_start[top_i] * MOE_TB + rank
    bidx = jnp.arange(n_blocks + 1, dtype=I32)
    be = jnp.minimum(jnp.searchsorted(blk_end, bidx, side='right'), ne - 1).astype(I32)
    active = bidx < n_used
    valid = jnp.where(active, jnp.clip(counts[be] - (bidx - blk_start[be]) * MOE_TB, 0, MOE_TB), 0)
    chunks = ((valid + MOE_CH - 1) // MOE_CH).astype(I32)
    nrow = ((valid + 7) // 8 * 8).astype(I32)
    last = jnp.maximum(n_used - 1, 0)
    be = jnp.where(active, be, be[last]).astype(I32)
    rows = jnp.where(active, bidx, last).astype(I32)
    tok = jnp.arange(n_tokens * TOP_K, dtype=I32) // TOP_K
    slot_tok = jnp.zeros((n_blocks * MOE_TB,), I32).at[dest.reshape(-1)].set(
        tok, unique_indices=True, mode='promise_in_bounds')
    return dest, slot_tok.reshape(n_blocks, 1, MOE_TB), be, rows, chunks, nrow, n_used


def _combine_body(d0_ref, dn_ref, x1_ref, gate_ref, gt_ref, gf_ref, ys_hbm, o_ref, ybuf, sem):
    i = pl.program_id(0)
    tm = x1_ref.shape[0]
    slot = lax.rem(i, 2)

    def issue_rows(dest_ref, sl):
        def body(i, carry):
            r0 = pl.multiple_of(i * 8, 8)
            for u in range(8):
                for k in range(TOP_K):
                    pltpu.make_async_copy(
                        ys_hbm.at[pl.ds(dest_ref[0, (r0 + u) * TOP_K + k], 1), :],
                        ybuf.at[sl, k, pl.ds(r0 + u, 1), :], sem.at[sl]).start(priority=k % 2)
            return carry

        lax.fori_loop(0, tm // 8, body, 0)

    @pl.when(i == 0)
    def _():
        issue_rows(d0_ref, 0)

    @pl.when(i + 1 < pl.num_programs(0))
    def _():
        issue_rows(dn_ref, 1 - slot)

    pltpu.make_async_copy(ybuf.at[slot], ybuf.at[slot], sem.at[slot]).wait()
    gates = gate_ref[...]
    y = gates[:, 0:1] * ybuf[slot, 0]
    for k in range(1, TOP_K):
        y = y + gates[:, k:k + 1] * ybuf[slot, k]
    x2 = x1_ref[...] + gt_ref[...] * y
    ms = jnp.mean(x2 * x2, axis=-1, keepdims=True)
    o_ref[...] = x2 * lax.rsqrt(ms + EPS) * gf_ref[...]


def _combine(x1, dest, ys, gates, mod, per_row, rows_per_batch, g_final, tm):
    n, d = x1.shape
    nt = n // tm
    dest3 = dest.reshape(nt, 1, tm * TOP_K)
    return pl.pallas_call(
        _combine_body,
        grid=(nt,),
        in_specs=[pl.BlockSpec((None, 1, tm * TOP_K), lambda i: (0, 0, 0), memory_space=pltpu.SMEM),
                  pl.BlockSpec((None, 1, tm * TOP_K), lambda i: (jnp.minimum(i + 1, nt - 1), 0, 0),
                               memory_space=pltpu.SMEM),
                  pl.BlockSpec((tm, d), lambda i: (i, 0)),
                  pl.BlockSpec((tm, LANES), lambda i: (i, 0)),
                  _mod_spec(per_row, tm, rows_per_batch, d, 5),
                  pl.BlockSpec((1, d), lambda i: (0, 0)),
                  pl.BlockSpec(memory_space=pl.ANY)],
        out_specs=pl.BlockSpec((tm, d), lambda i: (i, 0)),
        out_shape=jax.ShapeDtypeStruct((n, d), F32),
        scratch_shapes=[pltpu.VMEM((2, TOP_K, tm, d), F32), pltpu.SemaphoreType.DMA((2,))],
        compiler_params=_cparams(1),
        name="combine_norm",
    )(dest3, dest3, x1, gates, mod, g_final.reshape(1, d), ys)


def kernel(x_prompt, x_sample, c_prompt, c_sample, state_conv, state_ssm_re, state_ssm_im, g_mix, g_ffn, w_mod, b_mod, w_in, conv_w, ssm_a_re, ssm_a_im, ssm_log_dt, ssm_b_re, ssm_b_im, ssm_c_re, ssm_c_im, ssm_d, glu_w, glu_b, w_out, w_router, b_router, w1, b1, w2, b2, g_final):
    bp, lp, d = x_prompt.shape
    bs, ls, _ = x_sample.shape
    depth = g_mix.shape[0]
    dc = conv_w.shape[-1]
    n_grp, p_st, ch = ssm_b_re.shape[1:]
    ne = w_router.shape[-1]
    np_, ns_ = bp * lp, bs * ls
    n_tok = np_ + ns_
    assert ls == S5_T // 2 and lp % S5_T == 0 and conv_w.shape[1] == 3
    nc = lp // S5_T
    n_blocks = -(-(n_tok * TOP_K) // MOE_TB) + ne

    xp = x_prompt.reshape(np_, d)
    xs = x_sample.reshape(ns_, d)
    c_all = jnp.concatenate([c_prompt, c_sample], axis=0)
    pad = (-c_all.shape[0]) % 8
    c_all = jnp.pad(c_all, ((0, pad), (0, 0)))

    assert depth == 1
    outs = [[] for _ in range(6)]
    for l in range(depth):
        m = _modulation(c_all, w_mod[l], b_mod[l])
        mod_p = m[:bp].reshape(bp, 1, N_MOD * d)
        mod_s = jnp.repeat(m[bp:bp + bs], ls, axis=0)

        w_in_bf = w_in[l].astype(BF16)
        yc_p, u_p, tail_p = _inproj_conv_prompt(xp, mod_p, lp, g_mix[l], w_in_bf, conv_w[l], 512)
        e_s = jnp.pad(state_conv[l], ((0, 0), (0, ls - 2), (0, 0))).reshape(ns_, dc)
        yc_s, u_s, z_s = _inproj_conv_sample(xs, mod_s, g_mix[l], w_in_bf, conv_w[l], e_s, ls, 256)
        new_conv_p = tail_p[:, 6:8, :]
        new_conv_s = z_s.reshape(bs, ls, dc)[:, ls - 2:, :]

        ops = _s5_operators(ssm_a_re[l], ssm_a_im[l], ssm_log_dt[l], ssm_b_re[l], ssm_b_im[l],
                            ssm_c_re[l], ssm_c_im[l], ssm_d[l], glu_w[l], glu_b[l])
        h0 = jnp.concatenate([state_ssm_re[l], state_ssm_im[l]], axis=-1).transpose(1, 0, 2)
        ys_p, hf_p, ys_s, hf_s = _s5(u_p, u_s, h0, ops, bp, nc, ls)
        new_re_p = hf_p[:, :, :p_st].transpose(1, 0, 2)
        new_im_p = hf_p[:, :, p_st:].transpose(1, 0, 2)
        new_re_s = hf_s[:, :, :p_st].transpose(1, 0, 2)
        new_im_s = hf_s[:, :, p_st:].transpose(1, 0, 2)

        wo_bf = w_out[l].astype(BF16)
        wr_bf = w_router[l].astype(BF16)
        cnt0 = jnp.zeros((1, ne), F32)
        x1_p, h2_all, ti_p, gate_p, rank_p, cnt1 = _outproj(
            cnt0, xp, yc_p, ys_p, wo_bf, mod_p, False, lp, g_ffn[l], wr_bf, b_router[l],
            None, n_tok, 0, 2 * OUT_SUB)
        x1_s, h2_all, ti_s, gate_s, rank_s, cnt2 = _outproj(
            cnt1, xs, yc_s, ys_s, wo_bf, mod_s, True, ls, g_ffn[l], wr_bf, b_router[l],
            h2_all, n_tok, np_, OUT_SUB)

        counts = cnt2[0].astype(I32)
        top_i = jnp.concatenate([ti_p[:, :TOP_K], ti_s[:, :TOP_K]], axis=0)
        rank = jnp.concatenate([rank_p[:, :TOP_K], rank_s[:, :TOP_K]], axis=0)
        dest, slot_tok, blk_e, blk_rows, blk_chunks, blk_nrow, n_used = _moe_plan(
            counts, top_i, rank, n_tok, n_blocks)
        y_sorted = _moe_experts(h2_all, slot_tok, w1[l], b1[l], w2[l], b2[l],
                                blk_e, blk_rows, blk_chunks, blk_nrow, n_used)

        xp = _combine(x1_p, dest[:np_], y_sorted, gate_p, mod_p, False, lp, g_final, 256)
        xs = _combine(x1_s, dest[np_:], y_sorted, gate_s, mod_s, True, ls, g_final, 128)
        for lst, val in zip(outs, (new_conv_p, new_re_p, new_im_p, new_conv_s, new_re_s, new_im_s)):
            lst.append(val)

    y_prompt = xp.reshape(bp, lp, d)
    y_sample = xs.reshape(bs, ls, d)
    return (y_prompt, y_sample) + tuple(jnp.stack(o) for o in outs)
```

```python
import functools
import math

import jax
import jax.numpy as jnp
from jax import lax
from jax.experimental import pallas as pl
from jax.experimental.pallas import tpu as pltpu

F32 = jnp.float32
BF16 = jnp.bfloat16
I32 = jnp.int32

LANES = 128
EPS = 1e-6
N_MOD = 6
TOP_K = 4
SWIGLU_LIMIT = 7.0
SWIGLU_ALPHA = 1.702
GELU_C = math.sqrt(2.0 / math.pi)

S5_T = 16
OUT_SUB = 256
MOE_TB = 1792
MOE_CH = 256
MOE_TF = 512
MOE_TN = 512
VMEM_LIMIT = 56 * 1024 * 1024
HI = lax.Precision.HIGHEST


def _cparams(n_axes):
    return pltpu.CompilerParams(dimension_semantics=("arbitrary",) * n_axes,
                                vmem_limit_bytes=VMEM_LIMIT)


def _sigmoid(x):
    return 1.0 / (1.0 + jnp.exp(-x))


def _pack_bf16_pair(lo, hi):
    lo_bits = lax.bitcast_convert_type(lo, jnp.uint32) >> 16
    hi_bits = lax.bitcast_convert_type(hi, jnp.uint32) & jnp.uint32(0xFFFF0000)
    return lo_bits | hi_bits


def _unpack_bf16_pair(packed):
    lo = lax.bitcast_convert_type(packed << 16, F32)
    hi = lax.bitcast_convert_type(packed & jnp.uint32(0xFFFF0000), F32)
    return lo.astype(BF16), hi.astype(BF16)


def _resident(shape, index_map):
    return pl.BlockSpec(shape, index_map, pipeline_mode=pl.Buffered(1))


def _mod_body(c_ref, w_ref, b_ref, o_ref):
    c = c_ref[...]
    s = c * _sigmoid(c)
    o_ref[...] = jnp.dot(s.astype(BF16), w_ref[...].astype(BF16),
                         preferred_element_type=F32) + b_ref[...]


def _modulation(c_all, w_mod, b_mod):
    rows, d = c_all.shape
    n = w_mod.shape[1]
    tn = 2048
    return pl.pallas_call(
        _mod_body,
        grid=(n // tn,),
        in_specs=[pl.BlockSpec((rows, d), lambda j: (0, 0)),
                  pl.BlockSpec((d, tn), lambda j: (0, j)),
                  pl.BlockSpec((1, tn), lambda j: (0, j))],
        out_specs=pl.BlockSpec((rows, tn), lambda j: (0, j)),
        out_shape=jax.ShapeDtypeStruct((rows, n), F32),
        compiler_params=_cparams(1),
        name="modulation",
    )(c_all, w_mod, b_mod.reshape(1, n))


def _mod_spec(per_row, tm, rows_per_batch, d, col):
    if per_row:
        return pl.BlockSpec((tm, d), lambda i: (i, col))
    return pl.BlockSpec((None, 1, d), lambda i: ((i * tm) // rows_per_batch, 0, col))


def _inproj_pieces(x_ref, g_ref, sc_ref, sh_ref, w_ref, dc):
    x = x_ref[...]
    ms = jnp.mean(x * x, axis=-1, keepdims=True)
    h = x * lax.rsqrt(ms + EPS) * g_ref[...]
    hb = (h * (1.0 + sc_ref[...]) + sh_ref[...]).astype(BF16)
    b, c, v = (jnp.dot(hb, w_ref[:, k * dc:(k + 1) * dc], preferred_element_type=F32)
               for k in range(3))
    u = jnp.dot(hb, w_ref[:, 3 * dc:], preferred_element_type=F32)
    return b, c * v, u


def _conv_taps(bgate, z, z1, z2, w_ref):
    w = w_ref[...]
    return bgate * (w[0:1] * z2 + w[1:2] * z1 + w[2:3] * z)


def _inproj_conv_prompt_body(x_ref, g_ref, sc_ref, sh_ref, w_ref, cw_ref, y_ref, u_ref, tail_ref,
                             carry_ref, *, tiles_per_seq):
    @pl.when(pl.program_id(0) % tiles_per_seq == 0)
    def _():
        carry_ref[...] = jnp.zeros_like(carry_ref)

    bgate, z, u = _inproj_pieces(x_ref, g_ref, sc_ref, sh_ref, w_ref, y_ref.shape[1])
    u_ref[...] = u
    tt = z.shape[0]
    zc = jnp.concatenate([carry_ref[...], z], axis=0)
    z1 = pltpu.roll(zc, 1, 0)[8:]
    z2 = pltpu.roll(zc, 2, 0)[8:]
    y_ref[...] = _conv_taps(bgate, z, z1, z2, cw_ref).astype(y_ref.dtype)
    carry_ref[...] = z[tt - 8:]
    tail_ref[...] = z[tt - 8:]


def _inproj_conv_prompt(x, mod, seq, g, w_bf, conv_w, tm):
    n, d = x.shape
    dc = conv_w.shape[1]
    du = w_bf.shape[1] - 3 * dc
    row = lambda w: pl.BlockSpec((tm, w), lambda i: (i, 0))
    return pl.pallas_call(
        functools.partial(_inproj_conv_prompt_body, tiles_per_seq=seq // tm),
        grid=(n // tm,),
        in_specs=[row(d), pl.BlockSpec((1, d), lambda i: (0, 0)),
                  _mod_spec(False, tm, seq, d, 1), _mod_spec(False, tm, seq, d, 0),
                  _resident(w_bf.shape, lambda i: (0, 0)),
                  pl.BlockSpec(conv_w.shape, lambda i: (0, 0))],
        out_specs=[row(dc), row(du), pl.BlockSpec((None, 8, dc), lambda i: ((i * tm) // seq, 0, 0))],
        out_shape=[jax.ShapeDtypeStruct((n, dc), BF16), jax.ShapeDtypeStruct((n, du), F32),
                   jax.ShapeDtypeStruct((n // seq, 8, dc), F32)],
        scratch_shapes=[pltpu.VMEM((8, dc), F32)],
        compiler_params=_cparams(1),
        name="inproj_conv_prompt",
    )(x, g.reshape(1, d), mod, mod, w_bf, conv_w)


def _inproj_conv_sample_body(x_ref, g_ref, sc_ref, sh_ref, w_ref, cw_ref, e_ref, y_ref, u_ref, z_ref,
                             *, steps):
    bgate, z, u = _inproj_pieces(x_ref, g_ref, sc_ref, sh_ref, w_ref, y_ref.shape[1])
    u_ref[...] = u
    e = e_ref[...]
    rows = z.shape[0]
    assert steps & (steps - 1) == 0
    tpos = lax.broadcasted_iota(I32, z.shape, 0) & (steps - 1)
    z1 = jnp.where(tpos == 0, pltpu.roll(e, rows - 1, 0), pltpu.roll(z, 1, 0))
    z2 = jnp.where(tpos < 2, e, pltpu.roll(z, 2, 0))
    y_ref[...] = _conv_taps(bgate, z, z1, z2, cw_ref).astype(y_ref.dtype)
    z_ref[...] = z


def _inproj_conv_sample(x, mod, g, w_bf, conv_w, e, steps, tm):
    n, d = x.shape
    dc = conv_w.shape[1]
    du = w_bf.shape[1] - 3 * dc
    row = lambda w: pl.BlockSpec((tm, w), lambda i: (i, 0))
    return pl.pallas_call(
        functools.partial(_inproj_conv_sample_body, steps=steps),
        grid=(n // tm,),
        in_specs=[row(d), pl.BlockSpec((1, d), lambda i: (0, 0)),
                  _mod_spec(True, tm, 0, d, 1), _mod_spec(True, tm, 0, d, 0),
                  _resident(w_bf.shape, lambda i: (0, 0)),
                  pl.BlockSpec(conv_w.shape, lambda i: (0, 0)), row(dc)],
        out_specs=[row(dc), row(du), row(dc)],
        out_shape=[jax.ShapeDtypeStruct((n, dc), BF16), jax.ShapeDtypeStruct((n, du), F32),
                   jax.ShapeDtypeStruct((n, dc), F32)],
        compiler_params=_cparams(1),
        name="inproj_conv_sample",
    )(x, g.reshape(1, d), mod, mod, w_bf, conv_w, e)


def _s5_operators(a_re, a_im, log_dt, b_re, b_im, c_re, c_im, d_skip, glu_w, glu_b):
    g, p, ch = b_re.shape
    t = S5_T
    dt = jnp.exp(log_dt)[:, None]
    mag = jnp.exp(a_re * dt)
    ar = mag * jnp.cos(a_im * dt)
    ai = mag * jnp.sin(a_im * dt)
    den = a_re * a_re + a_im * a_im
    qr = ((ar - 1.0) * a_re + ai * a_im) / den
    qi = (ai * a_re - (ar - 1.0) * a_im) / den
    bbr = qr[..., None] * b_re - qi[..., None] * b_im
    bbi = qr[..., None] * b_im + qi[..., None] * b_re
    pr, pi = [jnp.ones_like(ar)], [jnp.zeros_like(ar)]
    for _ in range(t):
        pr, pi = pr + [pr[-1] * ar - pi[-1] * ai], pi + [pr[-1] * ai + pi[-1] * ar]
    pw_r, pw_i = jnp.stack(pr), jnp.stack(pi)
    car = c_re[None] * pw_r[:, :, None, :] - c_im[None] * pw_i[:, :, None, :]
    cai = c_re[None] * pw_i[:, :, None, :] + c_im[None] * pw_r[:, :, None, :]
    rev_r, rev_i = pw_r[t - 1::-1][:t], pw_i[t - 1::-1][:t]
    wsr = rev_r[..., None] * bbr[None] - rev_i[..., None] * bbi[None]
    wsi = rev_r[..., None] * bbi[None] + rev_i[..., None] * bbr[None]
    ws_op = jnp.concatenate([wsr.transpose(1, 0, 3, 2), wsi.transpose(1, 0, 3, 2)],
                            axis=-1).reshape(g, t * ch, 2 * p)
    wcr = car[1:].transpose(1, 3, 0, 2).reshape(g, p, t * ch)
    wci = cai[1:].transpose(1, 3, 0, 2).reshape(g, p, t * ch)
    wc_op = jnp.concatenate([wcr, -wci], axis=1)
    cmat_t = jnp.concatenate([c_re, -c_im], axis=-1)
    glu_t = jnp.swapaxes(glu_w, 1, 2)
    d_flat = jnp.tile(d_skip, (1, t))[:, None, :]
    gb_flat = jnp.tile(glu_b, (1, t))[:, None, :]

    def rot_tables(xr, xi):
        return jnp.concatenate([xr, xr], axis=-1), jnp.concatenate([-xi, xi], axis=-1)

    sr, si = pw_r[t], pw_i[t]
    p1s, p2s = [], []
    for _ in range(8):
        t1, t2 = rot_tables(sr, si)
        p1s.append(t1)
        p2s.append(t2)
        sr, si = sr * sr - si * si, 2.0 * sr * si
    h = t // 2
    p1_half, p2_half = rot_tables(pw_r[h], pw_i[h])
    return dict(
        ws=ws_op, wc=wc_op.astype(BF16), cmat=cmat_t, glu=glu_t,
        d=d_flat, gb=gb_flat, p1=jnp.stack(p1s, axis=1), p2=jnp.stack(p2s, axis=1),
        p1_half=p1_half[:, None, :], p2_half=p2_half[:, None, :])


def _gelu_tanh(y):
    return 0.5 * y * (1.0 + jnp.tanh(GELU_C * (y + 0.044715 * (y * y * y))))


def _dot_t(x, w_t):
    return lax.dot_general(x, w_t, (((1,), (1,)), ((), ())), preferred_element_type=F32)


def _block_toeplitz_t(seq, steps):
    ch, n = seq.shape
    lane = lax.broadcasted_iota(I32, seq.shape, 1)
    rows = []
    for t in range(steps):
        shift = (ch * (t + 1)) % n
        r = pltpu.roll(seq, shift, 1) if shift else seq
        rows.append(jnp.where(lane < ch * (t + 1), r, 0.0))
    return jnp.concatenate(rows, axis=0).astype(BF16)


def _s5_group_operators(ws, cmat, glu_t, steps=S5_T):
    n = ws.shape[0]
    seq = lax.dot_general(cmat, ws, (((1,), (1,)), ((), ())), precision=HI,
                          preferred_element_type=F32)
    ch = cmat.shape[0]
    glu_seq = jnp.concatenate([jnp.zeros((ch, n - ch), F32), glu_t], axis=1)
    return _block_toeplitz_t(seq, steps), _block_toeplitz_t(glu_seq, steps)


def _s5_tail(y, gl_t, gb):
    y = _gelu_tanh(y)
    gate = _dot_t(y.astype(BF16), gl_t) + gb
    return y * _sigmoid(gate)


S5_CH = 16
S5_OCT = LANES // S5_CH


def _fold_time(rows, blk):
    halves = [_block_transpose(rows[h:h + S5_OCT], blk) for h in range(0, len(rows), S5_OCT)]
    if len(halves) == 1:
        return halves[0]
    return [jnp.concatenate([hv[q] for hv in halves], axis=1) for q in range(S5_OCT)]


def _unfold_time(ys, blk):
    rows = []
    for h in range(ys[0].shape[1] // LANES):
        rows += _block_transpose([y[:, LANES * h:LANES * (h + 1)] for y in ys], blk)
    return rows


def _block_transpose(vs, blk):
    vs = list(vs)
    for d in (4, 2, 1):
        upper = (blk & d) != 0
        new = list(vs)
        for i in range(S5_OCT):
            if i & d:
                continue
            a, b = vs[i], vs[i + d]
            new[i] = jnp.where(upper, pltpu.roll(b, S5_CH * d, 1), a)
            new[i + d] = jnp.where(upper, b, pltpu.roll(a, LANES - S5_CH * d, 1))
        vs = new
    return vs


def _s5_body(u_ref, us_ref, h0_ref, ws_ref, wc_ref, cm_ref, gl_ref, d_ref, gb_ref, p1_ref, p2_ref,
             p1s_ref, p2s_ref, y_ref, hf_ref, ys_ref, hfs_ref, uf, yf, *, bsz, nc, steps_s):
    half = ws_ref.shape[-1] // 2
    seq = nc * S5_T
    blk = lax.broadcasted_iota(I32, (nc, LANES), 1) // S5_CH
    nseq = us_ref.shape[0] // steps_s
    ws_w = steps_s * S5_CH
    blk_s = lax.broadcasted_iota(I32, (nseq, LANES), 1) // S5_CH
    us = _fold_time([us_ref[pl.ds(t, nseq, stride=steps_s), :] for t in range(steps_s)], blk_s)
    ys_s = []

    def fold(b, carry):
        rows = [u_ref[pl.ds(pl.multiple_of(b * seq, seq) + t, nc, stride=S5_T), :]
                for t in range(S5_T)]
        for q, uq in enumerate(_fold_time(rows, blk)):
            uf[q, pl.ds(pl.multiple_of(b * nc, nc), nc), :] = uq
        return carry

    lax.fori_loop(0, bsz, fold, 0)

    for g in range(S5_OCT):
        u = uf[g]
        ub = u.astype(BF16)
        m_t, glu_t = _s5_group_operators(ws_ref[g], cm_ref[g], gl_ref[g])
        wsb = ws_ref[g].astype(BF16)
        yi = _dot_t(ub, m_t)
        e = jnp.dot(ub, wsb, preferred_element_type=F32)
        cpos = lax.broadcasted_iota(I32, e.shape, 0) & (nc - 1)
        p1 = p1_ref[g]
        p2 = p2_ref[g]
        z = jnp.where(cpos >= 1, pltpu.roll(e, 1, 0), 0.0)
        d, k = 1, 0
        while d < nc:
            zs = jnp.where(cpos >= d, pltpu.roll(z, d, 0), 0.0)
            z = z + zs * p1[k:k + 1] + pltpu.roll(zs, half, 1) * p2[k:k + 1]
            d, k = d * 2, k + 1
        yc = jnp.dot(z.astype(BF16), wc_ref[g], preferred_element_type=F32)
        yf[g] = _s5_tail(yi + yc + d_ref[g] * u, glu_t, gb_ref[g])
        hfin = e + z * p1[0:1] + pltpu.roll(z, half, 1) * p2[0:1]
        for b in range(bsz):
            r = b * nc + nc - 1
            hf_ref[g, b:b + 1, :] = hfin[r:r + 1, :]

        u8 = us[g]
        h0 = h0_ref[g]
        ub8 = u8.astype(BF16)
        yi8 = _dot_t(ub8, m_t[:ws_w, :ws_w])
        e8 = jnp.dot(ub8, wsb[wsb.shape[0] - ws_w:, :], preferred_element_type=F32)
        yc8 = jnp.dot(h0.astype(BF16), wc_ref[g][:, :ws_w], preferred_element_type=F32)
        ys_s.append(_s5_tail(yi8 + yc8 + d_ref[g][:, :ws_w] * u8, glu_t[:ws_w, :ws_w],
                             gb_ref[g][:, :ws_w]))
        hfs_ref[g] = e8 + h0 * p1s_ref[g] + pltpu.roll(h0, half, 1) * p2s_ref[g]

    for t, row in enumerate(_unfold_time(ys_s, blk_s)):
        ys_ref[pl.ds(t, nseq, stride=steps_s), :] = row

    def unfold(b, carry):
        ys = [yf[q, pl.ds(pl.multiple_of(b * nc, nc), nc), :] for q in range(S5_OCT)]
        for t, row in enumerate(_unfold_time(ys, blk)):
            y_ref[pl.ds(pl.multiple_of(b * seq, seq) + t, nc, stride=S5_T), :] = row
        return carry

    lax.fori_loop(0, bsz, unfold, 0)


def _s5(u, u_s, h0, ops, bsz, nc, steps_s):
    n, n_s = u.shape[0], u_s.shape[0]
    g, w, p2x = ops['ws'].shape
    assert nc & (nc - 1) == 0 and nc <= 2 ** (ops['p1'].shape[1] - 1) and steps_s == S5_OCT
    ch = ops['cmat'].shape[1]
    r_s = h0.shape[1]
    blk = lambda s1, s2: pl.BlockSpec((S5_OCT, s1, s2), lambda i: (i, 0, 0))
    col = lambda rows: pl.BlockSpec((rows, LANES), lambda i: (0, i))
    return pl.pallas_call(
        functools.partial(_s5_body, bsz=bsz, nc=nc, steps_s=steps_s),
        grid=(g // S5_OCT,),
        in_specs=[col(n), col(n_s), blk(r_s, p2x),
                  blk(w, p2x), blk(p2x, w), blk(ch, p2x), blk(ch, ch),
                  blk(1, w), blk(1, w), blk(8, p2x), blk(8, p2x), blk(1, p2x), blk(1, p2x)],
        out_specs=[col(n), blk(bsz, p2x), col(n_s), blk(r_s, p2x)],
        out_shape=[jax.ShapeDtypeStruct((n, g * S5_CH), F32),
                   jax.ShapeDtypeStruct((g, bsz, p2x), F32),
                   jax.ShapeDtypeStruct((n_s, g * S5_CH), F32),
                   jax.ShapeDtypeStruct((g, r_s, p2x), F32)],
        scratch_shapes=[pltpu.VMEM((S5_OCT, bsz * nc, w), F32),
                        pltpu.VMEM((S5_OCT, bsz * nc, w), F32)],
        compiler_params=_cparams(1),
        name="s5",
    )(u, u_s, h0, ops['ws'], ops['wc'], ops['cmat'], ops['glu'], ops['d'], ops['gb'],
      ops['p1'], ops['p2'], ops['p1_half'], ops['p2_half'])


def _outproj_body(cnt0_ref, x_ref, yc_ref, ys_ref, wo_ref, gt_ref, g_ref, sc_ref, sh_ref,
                  wr_ref, br_ref, *rest, has_prev):
    x1_ref, h2_ref, ti_ref, gate_ref, rank_ref, cnt_ref, run_ref = rest[1:] if has_prev else rest

    @pl.when(pl.program_id(0) == 0)
    def _():
        run_ref[...] = cnt0_ref[...]

    subs = [slice(r0, r0 + OUT_SUB) for r0 in range(0, x_ref.shape[0], OUT_SUB)]
    dc = yc_ref.shape[1]

    def mod(ref, rows):
        return ref[...] if ref.shape[0] == 1 else ref[rows, :]

    x1s = []
    for rows in subs:
        mix = (jnp.dot(yc_ref[rows, :], wo_ref[0:dc, :], preferred_element_type=F32)
               + jnp.dot(ys_ref[rows, :].astype(BF16), wo_ref[dc:, :], preferred_element_type=F32))
        x1s.append(x_ref[rows, :] + mod(gt_ref, rows) * mix)
    for rows, x1 in zip(subs, x1s):
        _route_rows(rows, x1, mod(sc_ref, rows), mod(sh_ref, rows), g_ref, wr_ref, br_ref,
                    x1_ref, h2_ref, ti_ref, gate_ref, rank_ref, run_ref)
    cnt_ref[...] = run_ref[...]


def _route_rows(rows, x1, sc, sh, g_ref, wr_ref, br_ref, x1_ref, h2_ref, ti_ref, gate_ref, rank_ref,
                run_ref):
    x1_ref[rows, :] = x1
    ms = jnp.mean(x1 * x1, axis=-1, keepdims=True)
    h = x1 * lax.rsqrt(ms + EPS) * g_ref[...]
    hb = (h * (1.0 + sc) + sh).astype(BF16)
    hb32 = hb.astype(F32)
    half = h2_ref.shape[1]
    h2_ref[rows, :] = _pack_bf16_pair(hb32[:, :half], hb32[:, half:])
    logits = jnp.dot(hb, wr_ref[...], preferred_element_type=F32) + br_ref[...]
    tm, ne = logits.shape
    lane = lax.broadcasted_iota(I32, logits.shape, 1).astype(F32)
    work = logits
    vals, ids, sels = [], [], []
    for _ in range(TOP_K):
        m = jnp.max(work, axis=1, keepdims=True)
        idx = jnp.min(jnp.where(work == m, lane, float(ne)), axis=1, keepdims=True)
        sel = lane == idx
        vals.append(m)
        ids.append(idx)
        sels.append(sel)
        work = jnp.where(sel, -jnp.inf, work)
    exps = [jnp.exp(v - vals[0]) for v in vals]
    tot = exps[0]
    for ex in exps[1:]:
        tot = tot + ex
    gates = [ex / tot for ex in exps]
    onehot = sels[0]
    for s in sels[1:]:
        onehot = onehot | s
    onehot = onehot.astype(F32)
    row = lax.broadcasted_iota(I32, (tm, tm), 0)
    col = lax.broadcasted_iota(I32, (tm, tm), 1)
    below = (col < row).astype(BF16)
    before = jnp.dot(below, onehot.astype(BF16), preferred_element_type=F32) + run_ref[...]
    ranks = [jnp.sum(jnp.where(s, before, 0.0), axis=1, keepdims=True) for s in sels]
    run_ref[...] = run_ref[...] + jnp.sum(onehot, axis=0, keepdims=True)

    wide = lax.broadcasted_iota(I32, (tm, ti_ref.shape[1]), 1)

    def spread(cols):
        out = cols[TOP_K - 1]
        for k in range(TOP_K - 2, -1, -1):
            out = jnp.where(wide == k, cols[k], out)
        return out

    ti_ref[rows, :] = spread(ids).astype(I32)
    gate_ref[rows, :] = spread(gates)
    rank_ref[rows, :] = spread(ranks).astype(I32)


def _outproj(cnt0, x, yc, ys, wo_bf, mod, per_row, rows_per_batch, g, wr_bf, br, h2_prev, n_all,
             row0, tm):
    n, d = x.shape
    dc = yc.shape[1]
    ne = wr_bf.shape[1]
    blk0 = row0 // tm
    row = lambda w: pl.BlockSpec((tm, w), lambda i: (i, 0))
    const = lambda s: pl.BlockSpec(s, lambda i: (0, 0))
    has_prev = h2_prev is not None
    in_specs = [const((1, ne)), row(d), row(dc), row(dc),
                _resident((d, d), lambda i: (0, 0)),
                _mod_spec(per_row, tm, rows_per_batch, d, 2), const((1, d)),
                _mod_spec(per_row, tm, rows_per_batch, d, 4),
                _mod_spec(per_row, tm, rows_per_batch, d, 3),
                const((d, ne)), const((1, ne))]
    args = [cnt0, x, yc, ys, wo_bf, mod, g.reshape(1, d), mod, mod, wr_bf, br.reshape(1, ne)]
    if has_prev:
        in_specs.append(pl.BlockSpec(memory_space=pl.ANY))
        args.append(h2_prev)
    return pl.pallas_call(
        functools.partial(_outproj_body, has_prev=has_prev),
        grid=(n // tm,),
        in_specs=in_specs,
        out_specs=[row(d), pl.BlockSpec((tm, d // 2), lambda i: (blk0 + i, 0)),
                   row(LANES), row(LANES), row(LANES), const((1, ne))],
        out_shape=[jax.ShapeDtypeStruct((n, d), F32),
                   jax.ShapeDtypeStruct((n_all, d // 2), jnp.uint32),
                   jax.ShapeDtypeStruct((n, LANES), I32), jax.ShapeDtypeStruct((n, LANES), F32),
                   jax.ShapeDtypeStruct((n, LANES), I32), jax.ShapeDtypeStruct((1, ne), F32)],
        scratch_shapes=[pltpu.VMEM((1, ne), F32)],
        input_output_aliases={len(args) - 1: 1} if has_prev else {},
        compiler_params=_cparams(1),
        name="outproj_router",
    )(*args)


def _moe_body(be_ref, rb_ref, nch_ref, nrow_ref, tok0_ref, tokn_ref, h2_hbm,
              w1g_ref, w1l_ref, b1g_ref, b1l_ref, w2_ref, b2_ref,
              o_ref, xp, act, sem, *, nf1):
    del be_ref, rb_ref
    b = pl.program_id(0)
    s = pl.program_id(1)
    n_chunks = nch_ref[b]

    def issue_rows(tok_ref, n_rows):
        def body(i, carry):
            r0 = pl.multiple_of(i * 8, 8)
            for u in range(8):
                pltpu.make_async_copy(h2_hbm.at[pl.ds(tok_ref[0, r0 + u], 1), :],
                                      xp.at[pl.ds(r0 + u, 1), :], sem.at[0]).start()
            return carry

        lax.fori_loop(0, n_rows // 8, body, 0)

    def wait_rows(n_rows):
        size = 8
        while size <= MOE_TB:
            @pl.when((n_rows & size) != 0)
            def _(size=size):
                pltpu.make_async_copy(h2_hbm.at[pl.ds(0, size), :], xp.at[pl.ds(0, size), :],
                                      sem.at[0]).wait()

            size *= 2

    @pl.when((b == 0) & (s == 0))
    def _():
        xp[...] = jnp.zeros_like(xp)
        issue_rows(tok0_ref, nrow_ref[0])

    @pl.when(s == 0)
    def _():
        wait_rows(nrow_ref[b])

    @pl.when(s == nf1)
    def _():
        issue_rows(tokn_ref, nrow_ref[b + 1])

    def chunk_groups(one, whole=()):
        assert MOE_TB // MOE_CH < 8
        for w in whole:
            @pl.when(n_chunks == w)
            def _(w=w):
                one(0, w * MOE_CH)

        split = functools.reduce(lambda acc, w: acc & (n_chunks != w), whole, n_chunks > 0)
        for size in (4, 2, 1):
            @pl.when(split & ((n_chunks & size) != 0))
            def _(size=size):
                r0 = (n_chunks & ~(2 * size - 1)) * MOE_CH
                one(pl.multiple_of(r0, size * MOE_CH), size * MOE_CH)

    @pl.when((s < nf1) & (n_chunks > 0))
    def _():
        def one(r0, rows):
            x = jnp.concatenate(_unpack_bf16_pair(xp[pl.ds(r0, rows), :]), axis=1)
            hg = jnp.dot(x, w1g_ref[...].astype(BF16), preferred_element_type=F32)
            hl = jnp.dot(x, w1l_ref[...].astype(BF16), preferred_element_type=F32)
            gl = jnp.minimum(hg + b1g_ref[...], SWIGLU_LIMIT)
            ln = jnp.clip(hl + b1l_ref[...], -SWIGLU_LIMIT, SWIGLU_LIMIT)
            a = gl * _sigmoid(SWIGLU_ALPHA * gl) * (ln + 1.0)
            act[s, pl.ds(r0, rows), :] = a.astype(BF16)

        chunk_groups(one)

    @pl.when((s >= nf1) & (n_chunks > 0))
    def _():
        def one(r0, rows):
            a = jnp.concatenate([act[k, pl.ds(r0, rows), :] for k in range(nf1)], axis=1)
            o_ref[pl.ds(r0, rows), :] = (
                jnp.dot(a, w2_ref[...].astype(BF16), preferred_element_type=F32) + b2_ref[...])

        chunk_groups(one, whole=(5, 6))

        def zero(i, carry):
            r0 = pl.multiple_of(i * MOE_CH, MOE_CH)
            o_ref[pl.ds(r0, MOE_CH), :] = jnp.zeros((MOE_CH, o_ref.shape[1]), F32)
            return carry

        lax.fori_loop(n_chunks, MOE_TB // MOE_CH, zero, 0)


def _moe_experts(h2_all, slot_tok, w1, b1, w2, b2, blk_expert, blk_rows, blk_chunks, blk_nrow,
                 n_grid_blocks):
    ne, d, f2 = w1.shape
    f = f2 // 2
    nf1 = f // MOE_TF
    nf2 = d // MOE_TN
    nb = slot_tok.shape[0]

    def s1(b, s, nch):
        return jnp.where(nch[b] > 0, jnp.minimum(s, nf1 - 1), nf1 - 1)

    def s2(b, s, nch):
        return jnp.where(nch[b] > 0, jnp.maximum(s - nf1, 0), nf2 - 1)

    def w2_index(b, s, be, rb, nch, nr):
        early = (s < nf1 - 1) & (b > 0)
        e = jnp.where(early, be[jnp.maximum(b - 1, 0)], be[b])
        return e, 0, jnp.where(early, nf2 - 1, s2(b, s, nch))

    grid_spec = pltpu.PrefetchScalarGridSpec(
        num_scalar_prefetch=4,
        grid=(n_grid_blocks, nf1 + nf2),
        in_specs=[
            pl.BlockSpec((None, 1, MOE_TB), lambda b, s, be, rb, nch, nr: (0, 0, 0),
                         memory_space=pltpu.SMEM),
            pl.BlockSpec((None, 1, MOE_TB),
                         lambda b, s, be, rb, nch, nr: (jnp.minimum(b + 1, nb - 1), 0, 0),
                         memory_space=pltpu.SMEM),
            pl.BlockSpec(memory_space=pl.ANY),
            pl.BlockSpec((None, d, MOE_TF), lambda b, s, be, rb, nch, nr: (be[b], 0, s1(b, s, nch))),
            pl.BlockSpec((None, d, MOE_TF),
                         lambda b, s, be, rb, nch, nr: (be[b], 0, nf1 + s1(b, s, nch))),
            pl.BlockSpec((None, 1, MOE_TF), lambda b, s, be, rb, nch, nr: (be[b], 0, s1(b, s, nch))),
            pl.BlockSpec((None, 1, MOE_TF),
                         lambda b, s, be, rb, nch, nr: (be[b], 0, nf1 + s1(b, s, nch))),
            pl.BlockSpec((None, f, MOE_TN), w2_index),
            pl.BlockSpec((None, 1, MOE_TN), w2_index),
        ],
        out_specs=pl.BlockSpec((MOE_TB, MOE_TN),
                               lambda b, s, be, rb, nch, nr: (rb[b], s2(b, s, nch))),
        scratch_shapes=[pltpu.VMEM((MOE_TB, d // 2), jnp.uint32),
                        pltpu.VMEM((nf1, MOE_TB, MOE_TF), BF16),
                        pltpu.SemaphoreType.DMA((1,))],
    )
    return pl.pallas_call(
        functools.partial(_moe_body, nf1=nf1),
        grid_spec=grid_spec,
        out_shape=jax.ShapeDtypeStruct((nb * MOE_TB, d), F32),
        compiler_params=_cparams(2),
        name="moe_experts",
    )(blk_expert, blk_rows, blk_chunks, blk_nrow, slot_tok, slot_tok, h2_all,
      w1, w1, b1.reshape(ne, 1, f2), b1.reshape(ne, 1, f2), w2, b2.reshape(ne, 1, d))


def _moe_plan(counts, top_i, rank, n_tokens, n_blocks):
    ne = counts.shape[0]
    nblk = (counts + MOE_TB - 1) // MOE_TB
    blk_end = jnp.cumsum(nblk)
    blk_start = blk_end - nblk
    n_used = blk_end[-1]
    dest = blk_start[top_i] * MOE_TB + rank
    bidx = jnp.arange(n_blocks + 1, dtype=I32)
    be = jnp.minimum(jnp.searchsorted(blk_end, bidx, side='right'), ne - 1).astype(I32)
    active = bidx < n_used
    valid = jnp.where(active, jnp.clip(counts[be] - (bidx - blk_start[be]) * MOE_TB, 0, MOE_TB), 0)
    chunks = ((valid + MOE_CH - 1) // MOE_CH).astype(I32)
    nrow = ((valid + 7) // 8 * 8).astype(I32)
    last = jnp.maximum(n_used - 1, 0)
    be = jnp.where(active, be, be[last]).astype(I32)
    rows = jnp.where(active, bidx, last).astype(I32)
    tok = jnp.arange(n_tokens * TOP_K, dtype=I32) // TOP_K
    slot_tok = jnp.zeros((n_blocks * MOE_TB,), I32).at[dest.reshape(-1)].set(
        tok, unique_indices=True, mode='promise_in_bounds')
    return dest, slot_tok.reshape(n_blocks, 1, MOE_TB), be, rows, chunks, nrow, n_used


def _combine_body(d0_ref, dn_ref, x1_ref, gate_ref, gt_ref, gf_ref, ys_hbm, o_ref, ybuf, sem):
    i = pl.program_id(0)
    tm = x1_ref.shape[0]
    slot = lax.rem(i, 2)

    def issue_rows(dest_ref, sl):
        def body(i, carry):
            r0 = pl.multiple_of(i * 8, 8)
            for u in range(8):
                for k in range(TOP_K):
                    pltpu.make_async_copy(
                        ys_hbm.at[pl.ds(dest_ref[0, (r0 + u) * TOP_K + k], 1), :],
                        ybuf.at[sl, k, pl.ds(r0 + u, 1), :], sem.at[sl]).start()
            return carry

        lax.fori_loop(0, tm // 8, body, 0)

    @pl.when(i == 0)
    def _():
        issue_rows(d0_ref, 0)

    @pl.when(i + 1 < pl.num_programs(0))
    def _():
        issue_rows(dn_ref, 1 - slot)

    pltpu.make_async_copy(ybuf.at[slot], ybuf.at[slot], sem.at[slot]).wait()
    gates = gate_ref[...]
    y = gates[:, 0:1] * ybuf[slot, 0]
    for k in range(1, TOP_K):
        y = y + gates[:, k:k + 1] * ybuf[slot, k]
    x2 = x1_ref[...] + gt_ref[...] * y
    ms = jnp.mean(x2 * x2, axis=-1, keepdims=True)
    o_ref[...] = x2 * lax.rsqrt(ms + EPS) * gf_ref[...]


def _combine(x1, dest, ys, gates, mod, per_row, rows_per_batch, g_final, tm):
    n, d = x1.shape
    nt = n // tm
    dest3 = dest.reshape(nt, 1, tm * TOP_K)
    return pl.pallas_call(
        _combine_body,
        grid=(nt,),
        in_specs=[pl.BlockSpec((None, 1, tm * TOP_K), lambda i: (0, 0, 0), memory_space=pltpu.SMEM),
                  pl.BlockSpec((None, 1, tm * TOP_K), lambda i: (jnp.minimum(i + 1, nt - 1), 0, 0),
                               memory_space=pltpu.SMEM),
                  pl.BlockSpec((tm, d), lambda i: (i, 0)),
                  pl.BlockSpec((tm, LANES), lambda i: (i, 0)),
                  _mod_spec(per_row, tm, rows_per_batch, d, 5),
                  pl.BlockSpec((1, d), lambda i: (0, 0)),
                  pl.BlockSpec(memory_space=pl.ANY)],
        out_specs=pl.BlockSpec((tm, d), lambda i: (i, 0)),
        out_shape=jax.ShapeDtypeStruct((n, d), F32),
        scratch_shapes=[pltpu.VMEM((2, TOP_K, tm, d), F32), pltpu.SemaphoreType.DMA((2,))],
        compiler_params=_cparams(1),
        name="combine_norm",
    )(dest3, dest3, x1, gates, mod, g_final.reshape(1, d), ys)


def kernel(x_prompt, x_sample, c_prompt, c_sample, state_conv, state_ssm_re, state_ssm_im, g_mix, g_ffn, w_mod, b_mod, w_in, conv_w, ssm_a_re, ssm_a_im, ssm_log_dt, ssm_b_re, ssm_b_im, ssm_c_re, ssm_c_im, ssm_d, glu_w, glu_b, w_out, w_router, b_router, w1, b1, w2, b2, g_final):
    bp, lp, d = x_prompt.shape
    bs, ls, _ = x_sample.shape
    depth = g_mix.shape[0]
    dc = conv_w.shape[-1]
    n_grp, p_st, ch = ssm_b_re.shape[1:]
    ne = w_router.shape[-1]
    np_, ns_ = bp * lp, bs * ls
    n_tok = np_ + ns_
    assert ls == S5_T // 2 and lp % S5_T == 0 and conv_w.shape[1] == 3
    nc = lp // S5_T
    n_blocks = -(-(n_tok * TOP_K) // MOE_TB) + ne

    xp = x_prompt.reshape(np_, d)
    xs = x_sample.reshape(ns_, d)
    c_all = jnp.concatenate([c_prompt, c_sample], axis=0)
    pad = (-c_all.shape[0]) % 8
    c_all = jnp.pad(c_all, ((0, pad), (0, 0)))

    assert depth == 1
    outs = [[] for _ in range(6)]
    for l in range(depth):
        m = _modulation(c_all, w_mod[l], b_mod[l])
        mod_p = m[:bp].reshape(bp, 1, N_MOD * d)
        mod_s = jnp.repeat(m[bp:bp + bs], ls, axis=0)

        w_in_bf = w_in[l].astype(BF16)
        yc_p, u_p, tail_p = _inproj_conv_prompt(xp, mod_p, lp, g_mix[l], w_in_bf, conv_w[l], 512)
        e_s = jnp.pad(state_conv[l], ((0, 0), (0, ls - 2), (0, 0))).reshape(ns_, dc)
        yc_s, u_s, z_s = _inproj_conv_sample(xs, mod_s, g_mix[l], w_in_bf, conv_w[l], e_s, ls, 256)
        new_conv_p = tail_p[:, 6:8, :]
        new_conv_s = z_s.reshape(bs, ls, dc)[:, ls - 2:, :]

        ops = _s5_operators(ssm_a_re[l], ssm_a_im[l], ssm_log_dt[l], ssm_b_re[l], ssm_b_im[l],
                            ssm_c_re[l], ssm_c_im[l], ssm_d[l], glu_w[l], glu_b[l])
        h0 = jnp.concatenate([state_ssm_re[l], state_ssm_im[l]], axis=-1).transpose(1, 0, 2)
        ys_p, hf_p, ys_s, hf_s = _s5(u_p, u_s, h0, ops, bp, nc, ls)
        new_re_p = hf_p[:, :, :p_st].transpose(1, 0, 2)
        new_im_p = hf_p[:, :, p_st:].transpose(1, 0, 2)
        new_re_s = hf_s[:, :, :p_st].transpose(1, 0, 2)
        new_im_s = hf_s[:, :, p_st:].transpose(1, 0, 2)

        wo_bf = w_out[l].astype(BF16)
        wr_bf = w_router[l].astype(BF16)
        cnt0 = jnp.zeros((1, ne), F32)
        x1_p, h2_all, ti_p, gate_p, rank_p, cnt1 = _outproj(
            cnt0, xp, yc_p, ys_p, wo_bf, mod_p, False, lp, g_ffn[l], wr_bf, b_router[l],
            None, n_tok, 0, 2 * OUT_SUB)
        x1_s, h2_all, ti_s, gate_s, rank_s, cnt2 = _outproj(
            cnt1, xs, yc_s, ys_s, wo_bf, mod_s, True, ls, g_ffn[l], wr_bf, b_router[l],
            h2_all, n_tok, np_, OUT_SUB)

        counts = cnt2[0].astype(I32)
        top_i = jnp.concatenate([ti_p[:, :TOP_K], ti_s[:, :TOP_K]], axis=0)
        rank = jnp.concatenate([rank_p[:, :TOP_K], rank_s[:, :TOP_K]], axis=0)
        dest, slot_tok, blk_e, blk_rows, blk_chunks, blk_nrow, n_used = _moe_plan(
            counts, top_i, rank, n_tok, n_blocks)
        y_sorted = _moe_experts(h2_all, slot_tok, w1[l], b1[l], w2[l], b2[l],
                                blk_e, blk_rows, blk_chunks, blk_nrow, n_used)

        xp = _combine(x1_p, dest[:np_], y_sorted, gate_p, mod_p, False, lp, g_final, 256)
        xs = _combine(x1_s, dest[np_:], y_sorted, gate_s, mod_s, True, ls, g_final, 128)
        for lst, val in zip(outs, (new_conv_p, new_re_p, new_im_p, new_conv_s, new_re_s, new_im_s)):
            lst.append(val)

    y_prompt = xp.reshape(bp, lp, d)
    y_sample = xs.reshape(bs, ls, d)
    return (y_prompt, y_sample) + tuple(jnp.stack(o) for o in outs)
```

```python
import functools
import math

import jax
import jax.numpy as jnp
from jax import lax
from jax.experimental import pallas as pl
from jax.experimental.pallas import tpu as pltpu

F32 = jnp.float32
BF16 = jnp.bfloat16
I32 = jnp.int32

LANES = 128
EPS = 1e-6
N_MOD = 6
TOP_K = 4
SWIGLU_LIMIT = 7.0
SWIGLU_ALPHA = 1.702
GELU_C = math.sqrt(2.0 / math.pi)

S5_T = 16
OUT_SUB = 256
MOE_TB = 1792
MOE_CH = 256
MOE_TF = 512
MOE_TN = 512
VMEM_LIMIT = 56 * 1024 * 1024
HI = lax.Precision.HIGHEST


def _cparams(n_axes):
    return pltpu.CompilerParams(dimension_semantics=("arbitrary",) * n_axes,
                                vmem_limit_bytes=VMEM_LIMIT)


def _sigmoid(x):
    return 1.0 / (1.0 + jnp.exp(-x))


def _pack_bf16_pair(lo, hi):
    lo_bits = lax.bitcast_convert_type(lo, jnp.uint32) >> 16
    hi_bits = lax.bitcast_convert_type(hi, jnp.uint32) & jnp.uint32(0xFFFF0000)
    return lo_bits | hi_bits


def _unpack_bf16_pair(packed):
    lo = lax.bitcast_convert_type(packed << 16, F32)
    hi = lax.bitcast_convert_type(packed & jnp.uint32(0xFFFF0000), F32)
    return lo.astype(BF16), hi.astype(BF16)


def _resident(shape, index_map):
    return pl.BlockSpec(shape, index_map, pipeline_mode=pl.Buffered(1))


def _mod_body(c_ref, w_ref, b_ref, o_ref):
    c = c_ref[...]
    s = c * _sigmoid(c)
    o_ref[...] = jnp.dot(s.astype(BF16), w_ref[...].astype(BF16),
                         preferred_element_type=F32) + b_ref[...]


def _modulation(c_all, w_mod, b_mod):
    rows, d = c_all.shape
    n = w_mod.shape[1]
    tn = 1024
    return pl.pallas_call(
        _mod_body,
        grid=(n // tn,),
        in_specs=[pl.BlockSpec((rows, d), lambda j: (0, 0)),
                  pl.BlockSpec((d, tn), lambda j: (0, j)),
                  pl.BlockSpec((1, tn), lambda j: (0, j))],
        out_specs=pl.BlockSpec((rows, tn), lambda j: (0, j)),
        out_shape=jax.ShapeDtypeStruct((rows, n), F32),
        compiler_params=_cparams(1),
        name="modulation",
    )(c_all, w_mod, b_mod.reshape(1, n))


def _mod_spec(per_row, tm, rows_per_batch, d, col):
    if per_row:
        return pl.BlockSpec((tm // rows_per_batch, d), lambda i: (i, col))
    return pl.BlockSpec((None, 1, d), lambda i: ((i * tm) // rows_per_batch, 0, col))


def _mod_rows(ref, tm):
    m = ref[...]
    r = m.shape[0]
    if r == 1 or r == tm:
        return m
    k = tm // r
    assert k & (k - 1) == 0
    row = lax.broadcasted_iota(I32, (tm, r), 0) >> (k.bit_length() - 1)
    onehot = (row == lax.broadcasted_iota(I32, (tm, r), 1)).astype(F32)
    return jnp.dot(onehot, m, precision=HI, preferred_element_type=F32)


def _inproj_pieces(x_ref, g_ref, sc_ref, sh_ref, w_ref, dc):
    x = x_ref[...]
    ms = jnp.mean(x * x, axis=-1, keepdims=True)
    h = x * lax.rsqrt(ms + EPS) * g_ref[...]
    tm = x.shape[0]
    hb = (h * (1.0 + _mod_rows(sc_ref, tm)) + _mod_rows(sh_ref, tm)).astype(BF16)
    b, c, v = (jnp.dot(hb, w_ref[:, k * dc:(k + 1) * dc], preferred_element_type=F32)
               for k in range(3))
    u = jnp.dot(hb, w_ref[:, 3 * dc:], preferred_element_type=F32)
    return b, c * v, u


def _conv_taps(bgate, z, z1, z2, w_ref):
    w = w_ref[...]
    return bgate * (w[0:1] * z2 + w[1:2] * z1 + w[2:3] * z)


def _inproj_conv_prompt_body(x_ref, g_ref, sc_ref, sh_ref, w_ref, cw_ref, y_ref, u_ref, tail_ref,
                             carry_ref, *, tiles_per_seq):
    @pl.when(pl.program_id(0) % tiles_per_seq == 0)
    def _():
        carry_ref[...] = jnp.zeros_like(carry_ref)

    bgate, z, u = _inproj_pieces(x_ref, g_ref, sc_ref, sh_ref, w_ref, y_ref.shape[1])
    u_ref[...] = u
    tt = z.shape[0]
    zc = jnp.concatenate([carry_ref[...], z], axis=0)
    z1 = pltpu.roll(zc, 1, 0)[8:]
    z2 = pltpu.roll(zc, 2, 0)[8:]
    y_ref[...] = _conv_taps(bgate, z, z1, z2, cw_ref).astype(y_ref.dtype)
    carry_ref[...] = z[tt - 8:]
    tail_ref[...] = z[tt - 8:]


def _inproj_conv_prompt(x, mod, seq, g, w_bf, conv_w, tm):
    n, d = x.shape
    dc = conv_w.shape[1]
    du = w_bf.shape[1] - 3 * dc
    row = lambda w: pl.BlockSpec((tm, w), lambda i: (i, 0))
    return pl.pallas_call(
        functools.partial(_inproj_conv_prompt_body, tiles_per_seq=seq // tm),
        grid=(n // tm,),
        in_specs=[row(d), pl.BlockSpec((1, d), lambda i: (0, 0)),
                  _mod_spec(False, tm, seq, d, 1), _mod_spec(False, tm, seq, d, 0),
                  _resident(w_bf.shape, lambda i: (0, 0)),
                  pl.BlockSpec(conv_w.shape, lambda i: (0, 0))],
        out_specs=[row(dc), row(du), pl.BlockSpec((None, 8, dc), lambda i: ((i * tm) // seq, 0, 0))],
        out_shape=[jax.ShapeDtypeStruct((n, dc), BF16), jax.ShapeDtypeStruct((n, du), F32),
                   jax.ShapeDtypeStruct((n // seq, 8, dc), F32)],
        scratch_shapes=[pltpu.VMEM((8, dc), F32)],
        compiler_params=_cparams(1),
        name="inproj_conv_prompt",
    )(x, g.reshape(1, d), mod, mod, w_bf, conv_w)


def _inproj_conv_sample_body(x_ref, g_ref, sc_ref, sh_ref, w_ref, cw_ref, e_ref, y_ref, u_ref, z_ref,
                             *, steps):
    bgate, z, u = _inproj_pieces(x_ref, g_ref, sc_ref, sh_ref, w_ref, y_ref.shape[1])
    u_ref[...] = u
    e = e_ref[...]
    rows = z.shape[0]
    assert steps & (steps - 1) == 0
    tpos = lax.broadcasted_iota(I32, z.shape, 0) & (steps - 1)
    z1 = jnp.where(tpos == 0, pltpu.roll(e, rows - 1, 0), pltpu.roll(z, 1, 0))
    z2 = jnp.where(tpos < 2, e, pltpu.roll(z, 2, 0))
    y_ref[...] = _conv_taps(bgate, z, z1, z2, cw_ref).astype(y_ref.dtype)
    z_ref[...] = z


def _inproj_conv_sample(x, mod, g, w_bf, conv_w, e, steps, tm):
    n, d = x.shape
    dc = conv_w.shape[1]
    du = w_bf.shape[1] - 3 * dc
    row = lambda w: pl.BlockSpec((tm, w), lambda i: (i, 0))
    return pl.pallas_call(
        functools.partial(_inproj_conv_sample_body, steps=steps),
        grid=(n // tm,),
        in_specs=[row(d), pl.BlockSpec((1, d), lambda i: (0, 0)),
                  _mod_spec(True, tm, steps, d, 1), _mod_spec(True, tm, steps, d, 0),
                  _resident(w_bf.shape, lambda i: (0, 0)),
                  pl.BlockSpec(conv_w.shape, lambda i: (0, 0)), row(dc)],
        out_specs=[row(dc), row(du), row(dc)],
        out_shape=[jax.ShapeDtypeStruct((n, dc), BF16), jax.ShapeDtypeStruct((n, du), F32),
                   jax.ShapeDtypeStruct((n, dc), F32)],
        compiler_params=_cparams(1),
        name="inproj_conv_sample",
    )(x, g.reshape(1, d), mod, mod, w_bf, conv_w, e)


def _s5_operators(a_re, a_im, log_dt, b_re, b_im, c_re, c_im, d_skip, glu_w, glu_b):
    g, p, ch = b_re.shape
    t = S5_T
    dt = jnp.exp(log_dt)[:, None]
    mag = jnp.exp(a_re * dt)
    ar = mag * jnp.cos(a_im * dt)
    ai = mag * jnp.sin(a_im * dt)
    den = a_re * a_re + a_im * a_im
    qr = ((ar - 1.0) * a_re + ai * a_im) / den
    qi = (ai * a_re - (ar - 1.0) * a_im) / den
    bbr = qr[..., None] * b_re - qi[..., None] * b_im
    bbi = qr[..., None] * b_im + qi[..., None] * b_re
    pr, pi = [jnp.ones_like(ar)], [jnp.zeros_like(ar)]
    for _ in range(t):
        pr, pi = pr + [pr[-1] * ar - pi[-1] * ai], pi + [pr[-1] * ai + pi[-1] * ar]
    pw_r, pw_i = jnp.stack(pr), jnp.stack(pi)
    car = c_re[None] * pw_r[:, :, None, :] - c_im[None] * pw_i[:, :, None, :]
    cai = c_re[None] * pw_i[:, :, None, :] + c_im[None] * pw_r[:, :, None, :]
    rev_r, rev_i = pw_r[t - 1::-1][:t], pw_i[t - 1::-1][:t]
    wsr = rev_r[..., None] * bbr[None] - rev_i[..., None] * bbi[None]
    wsi = rev_r[..., None] * bbi[None] + rev_i[..., None] * bbr[None]
    ws_op = jnp.concatenate([wsr.transpose(1, 0, 3, 2), wsi.transpose(1, 0, 3, 2)],
                            axis=-1).reshape(g, t * ch, 2 * p)
    wcr = car[1:].transpose(1, 3, 0, 2).reshape(g, p, t * ch)
    wci = cai[1:].transpose(1, 3, 0, 2).reshape(g, p, t * ch)
    wc_op = jnp.concatenate([wcr, -wci], axis=1)
    cmat_t = jnp.concatenate([c_re, -c_im], axis=-1)
    glu_t = jnp.swapaxes(glu_w, 1, 2)
    d_flat = jnp.tile(d_skip, (1, t))[:, None, :]
    gb_flat = jnp.tile(glu_b, (1, t))[:, None, :]

    def rot_tables(xr, xi):
        return jnp.concatenate([xr, xr], axis=-1), jnp.concatenate([-xi, xi], axis=-1)

    sr, si = pw_r[t], pw_i[t]
    p1s, p2s = [], []
    for _ in range(8):
        t1, t2 = rot_tables(sr, si)
        p1s.append(t1)
        p2s.append(t2)
        sr, si = sr * sr - si * si, 2.0 * sr * si
    h = t // 2
    p1_half, p2_half = rot_tables(pw_r[h], pw_i[h])
    return dict(
        ws=ws_op, wc=wc_op.astype(BF16), cmat=cmat_t, glu=glu_t,
        d=d_flat, gb=gb_flat, p1=jnp.stack(p1s, axis=1), p2=jnp.stack(p2s, axis=1),
        p1_half=p1_half[:, None, :], p2_half=p2_half[:, None, :])


def _gelu_tanh(y):
    return 0.5 * y * (1.0 + jnp.tanh(GELU_C * (y + 0.044715 * (y * y * y))))


def _dot_t(x, w_t):
    return lax.dot_general(x, w_t, (((1,), (1,)), ((), ())), preferred_element_type=F32)


def _block_toeplitz_t(seq, steps):
    ch, n = seq.shape
    lane = lax.broadcasted_iota(I32, seq.shape, 1)
    rows = []
    for t in range(steps):
        shift = (ch * (t + 1)) % n
        r = pltpu.roll(seq, shift, 1) if shift else seq
        rows.append(jnp.where(lane < ch * (t + 1), r, 0.0))
    return jnp.concatenate(rows, axis=0).astype(BF16)


def _s5_group_operators(ws, cmat, glu_t, steps=S5_T):
    n = ws.shape[0]
    seq = lax.dot_general(cmat, ws, (((1,), (1,)), ((), ())), precision=HI,
                          preferred_element_type=F32)
    ch = cmat.shape[0]
    glu_seq = jnp.concatenate([jnp.zeros((ch, n - ch), F32), glu_t], axis=1)
    return _block_toeplitz_t(seq, steps), _block_toeplitz_t(glu_seq, steps)


def _s5_tail(y, gl_t, gb):
    y = _gelu_tanh(y)
    gate = _dot_t(y.astype(BF16), gl_t) + gb
    return y * _sigmoid(gate)


S5_CH = 16
S5_OCT = LANES // S5_CH


def _fold_time(rows, blk):
    halves = [_block_transpose(rows[h:h + S5_OCT], blk) for h in range(0, len(rows), S5_OCT)]
    if len(halves) == 1:
        return halves[0]
    return [jnp.concatenate([hv[q] for hv in halves], axis=1) for q in range(S5_OCT)]


def _unfold_time(ys, blk):
    rows = []
    for h in range(ys[0].shape[1] // LANES):
        rows += _block_transpose([y[:, LANES * h:LANES * (h + 1)] for y in ys], blk)
    return rows


def _block_transpose(vs, blk):
    vs = list(vs)
    for d in (4, 2, 1):
        upper = (blk & d) != 0
        new = list(vs)
        for i in range(S5_OCT):
            if i & d:
                continue
            a, b = vs[i], vs[i + d]
            new[i] = jnp.where(upper, pltpu.roll(b, S5_CH * d, 1), a)
            new[i + d] = jnp.where(upper, b, pltpu.roll(a, LANES - S5_CH * d, 1))
        vs = new
    return vs


def _s5_body(u_ref, us_ref, h0_ref, ws_ref, wc_ref, cm_ref, gl_ref, d_ref, gb_ref, p1_ref, p2_ref,
             p1s_ref, p2s_ref, y_ref, hf_ref, ys_ref, hfs_ref, uf, yf, *, bsz, nc, steps_s):
    half = ws_ref.shape[-1] // 2
    seq = nc * S5_T
    blk = lax.broadcasted_iota(I32, (nc, LANES), 1) // S5_CH
    nseq = us_ref.shape[0] // steps_s
    ws_w = steps_s * S5_CH
    blk_s = lax.broadcasted_iota(I32, (nseq, LANES), 1) // S5_CH
    us = _fold_time([us_ref[pl.ds(t, nseq, stride=steps_s), :] for t in range(steps_s)], blk_s)
    ys_s = []

    def fold(b, carry):
        rows = [u_ref[pl.ds(pl.multiple_of(b * seq, seq) + t, nc, stride=S5_T), :]
                for t in range(S5_T)]
        for q, uq in enumerate(_fold_time(rows, blk)):
            uf[q, pl.ds(pl.multiple_of(b * nc, nc), nc), :] = uq
        return carry

    lax.fori_loop(0, bsz, fold, 0)

    for g in range(S5_OCT):
        u = uf[g]
        ub = u.astype(BF16)
        m_t, glu_t = _s5_group_operators(ws_ref[g], cm_ref[g], gl_ref[g])
        wsb = ws_ref[g].astype(BF16)
        yi = _dot_t(ub, m_t)
        e = jnp.dot(ub, wsb, preferred_element_type=F32)
        cpos = lax.broadcasted_iota(I32, e.shape, 0) & (nc - 1)
        p1 = p1_ref[g]
        p2 = p2_ref[g]
        z = jnp.where(cpos >= 1, pltpu.roll(e, 1, 0), 0.0)
        d, k = 1, 0
        while d < nc:
            zs = jnp.where(cpos >= d, pltpu.roll(z, d, 0), 0.0)
            z = z + zs * p1[k:k + 1] + pltpu.roll(zs, half, 1) * p2[k:k + 1]
            d, k = d * 2, k + 1
        yc = jnp.dot(z.astype(BF16), wc_ref[g], preferred_element_type=F32)
        yf[g] = _s5_tail(yi + yc + d_ref[g] * u, glu_t, gb_ref[g])
        hfin = e + z * p1[0:1] + pltpu.roll(z, half, 1) * p2[0:1]
        for b in range(bsz):
            r = b * nc + nc - 1
            hf_ref[g, b:b + 1, :] = hfin[r:r + 1, :]

        u8 = us[g]
        h0 = h0_ref[g]
        ub8 = u8.astype(BF16)
        yi8 = _dot_t(ub8, m_t[:ws_w, :ws_w])
        e8 = jnp.dot(ub8, wsb[wsb.shape[0] - ws_w:, :], preferred_element_type=F32)
        yc8 = jnp.dot(h0.astype(BF16), wc_ref[g][:, :ws_w], preferred_element_type=F32)
        ys_s.append(_s5_tail(yi8 + yc8 + d_ref[g][:, :ws_w] * u8, glu_t[:ws_w, :ws_w],
                             gb_ref[g][:, :ws_w]))
        hfs_ref[g] = e8 + h0 * p1s_ref[g] + pltpu.roll(h0, half, 1) * p2s_ref[g]

    for t, row in enumerate(_unfold_time(ys_s, blk_s)):
        ys_ref[pl.ds(t, nseq, stride=steps_s), :] = row

    def unfold(b, carry):
        ys = [yf[q, pl.ds(pl.multiple_of(b * nc, nc), nc), :] for q in range(S5_OCT)]
        for t, row in enumerate(_unfold_time(ys, blk)):
            y_ref[pl.ds(pl.multiple_of(b * seq, seq) + t, nc, stride=S5_T), :] = row
        return carry

    lax.fori_loop(0, bsz, unfold, 0)


def _s5(u, u_s, h0, ops, bsz, nc, steps_s):
    n, n_s = u.shape[0], u_s.shape[0]
    g, w, p2x = ops['ws'].shape
    assert nc & (nc - 1) == 0 and nc <= 2 ** (ops['p1'].shape[1] - 1) and steps_s == S5_OCT
    ch = ops['cmat'].shape[1]
    r_s = h0.shape[1]
    blk = lambda s1, s2: pl.BlockSpec((S5_OCT, s1, s2), lambda i: (i, 0, 0))
    col = lambda rows: pl.BlockSpec((rows, LANES), lambda i: (0, i))
    return pl.pallas_call(
        functools.partial(_s5_body, bsz=bsz, nc=nc, steps_s=steps_s),
        grid=(g // S5_OCT,),
        in_specs=[col(n), col(n_s), blk(r_s, p2x),
                  blk(w, p2x), blk(p2x, w), blk(ch, p2x), blk(ch, ch),
                  blk(1, w), blk(1, w), blk(8, p2x), blk(8, p2x), blk(1, p2x), blk(1, p2x)],
        out_specs=[col(n), blk(bsz, p2x), col(n_s), blk(r_s, p2x)],
        out_shape=[jax.ShapeDtypeStruct((n, g * S5_CH), F32),
                   jax.ShapeDtypeStruct((g, bsz, p2x), F32),
                   jax.ShapeDtypeStruct((n_s, g * S5_CH), F32),
                   jax.ShapeDtypeStruct((g, r_s, p2x), F32)],
        scratch_shapes=[pltpu.VMEM((S5_OCT, bsz * nc, w), F32),
                        pltpu.VMEM((S5_OCT, bsz * nc, w), F32)],
        compiler_params=_cparams(1),
        name="s5",
    )(u, u_s, h0, ops['ws'], ops['wc'], ops['cmat'], ops['glu'], ops['d'], ops['gb'],
      ops['p1'], ops['p2'], ops['p1_half'], ops['p2_half'])


def _outproj_body(cnt0_ref, x_ref, yc_ref, ys_ref, wo_ref, gt_ref, g_ref, sc_ref, sh_ref,
                  wr_ref, br_ref, *rest, has_prev):
    x1_ref, h2_ref, ti_ref, gate_ref, rank_ref, cnt_ref, run_ref = rest[1:] if has_prev else rest

    @pl.when(pl.program_id(0) == 0)
    def _():
        run_ref[...] = cnt0_ref[...]

    subs = [slice(r0, r0 + OUT_SUB) for r0 in range(0, x_ref.shape[0], OUT_SUB)]
    dc = yc_ref.shape[1]

    def mod(ref, rows):
        m = _mod_rows(ref, x_ref.shape[0])
        return m if m.shape[0] == 1 else m[rows, :]

    x1s = []
    for rows in subs:
        mix = (jnp.dot(yc_ref[rows, :], wo_ref[0:dc, :], preferred_element_type=F32)
               + jnp.dot(ys_ref[rows, :].astype(BF16), wo_ref[dc:, :], preferred_element_type=F32))
        x1s.append(x_ref[rows, :] + mod(gt_ref, rows) * mix)
    for rows, x1 in zip(subs, x1s):
        _route_rows(rows, x1, mod(sc_ref, rows), mod(sh_ref, rows), g_ref, wr_ref, br_ref,
                    x1_ref, h2_ref, ti_ref, gate_ref, rank_ref, run_ref)
    cnt_ref[...] = run_ref[...]


def _route_rows(rows, x1, sc, sh, g_ref, wr_ref, br_ref, x1_ref, h2_ref, ti_ref, gate_ref, rank_ref,
                run_ref):
    x1_ref[rows, :] = x1
    ms = jnp.mean(x1 * x1, axis=-1, keepdims=True)
    h = x1 * lax.rsqrt(ms + EPS) * g_ref[...]
    hb = (h * (1.0 + sc) + sh).astype(BF16)
    hb32 = hb.astype(F32)
    half = h2_ref.shape[1]
    h2_ref[rows, :] = _pack_bf16_pair(hb32[:, :half], hb32[:, half:])
    logits = jnp.dot(hb, wr_ref[...], preferred_element_type=F32) + br_ref[...]
    tm, ne = logits.shape
    lane = lax.broadcasted_iota(I32, logits.shape, 1).astype(F32)
    work = logits
    vals, ids, sels = [], [], []
    for _ in range(TOP_K):
        m = jnp.max(work, axis=1, keepdims=True)
        idx = jnp.min(jnp.where(work == m, lane, float(ne)), axis=1, keepdims=True)
        sel = lane == idx
        vals.append(m)
        ids.append(idx)
        sels.append(sel)
        work = jnp.where(sel, -jnp.inf, work)
    exps = [jnp.exp(v - vals[0]) for v in vals]
    tot = exps[0]
    for ex in exps[1:]:
        tot = tot + ex
    gates = [ex / tot for ex in exps]
    onehot = sels[0]
    for s in sels[1:]:
        onehot = onehot | s
    onehot = onehot.astype(F32)
    row = lax.broadcasted_iota(I32, (tm, tm), 0)
    col = lax.broadcasted_iota(I32, (tm, tm), 1)
    below = (col < row).astype(BF16)
    before = jnp.dot(below, onehot.astype(BF16), preferred_element_type=F32) + run_ref[...]
    ranks = [jnp.sum(jnp.where(s, before, 0.0), axis=1, keepdims=True) for s in sels]
    run_ref[...] = run_ref[...] + jnp.sum(onehot, axis=0, keepdims=True)

    wide = lax.broadcasted_iota(I32, (tm, ti_ref.shape[1]), 1)

    def spread(cols):
        out = cols[TOP_K - 1]
        for k in range(TOP_K - 2, -1, -1):
            out = jnp.where(wide == k, cols[k], out)
        return out

    ti_ref[rows, :] = spread(ids).astype(I32)
    gate_ref[rows, :] = spread(gates)
    rank_ref[rows, :] = spread(ranks).astype(I32)


def _outproj(cnt0, x, yc, ys, wo_bf, mod, per_row, rows_per_batch, g, wr_bf, br, h2_prev, n_all,
             row0, tm):
    n, d = x.shape
    dc = yc.shape[1]
    ne = wr_bf.shape[1]
    blk0 = row0 // tm
    row = lambda w: pl.BlockSpec((tm, w), lambda i: (i, 0))
    const = lambda s: pl.BlockSpec(s, lambda i: (0, 0))
    has_prev = h2_prev is not None
    in_specs = [const((1, ne)), row(d), row(dc), row(dc),
                _resident((d, d), lambda i: (0, 0)),
                _mod_spec(per_row, tm, rows_per_batch, d, 2), const((1, d)),
                _mod_spec(per_row, tm, rows_per_batch, d, 4),
                _mod_spec(per_row, tm, rows_per_batch, d, 3),
                const((d, ne)), const((1, ne))]
    args = [cnt0, x, yc, ys, wo_bf, mod, g.reshape(1, d), mod, mod, wr_bf, br.reshape(1, ne)]
    if has_prev:
        in_specs.append(pl.BlockSpec(memory_space=pl.ANY))
        args.append(h2_prev)
    return pl.pallas_call(
        functools.partial(_outproj_body, has_prev=has_prev),
        grid=(n // tm,),
        in_specs=in_specs,
        out_specs=[row(d), pl.BlockSpec((tm, d // 2), lambda i: (blk0 + i, 0)),
                   row(LANES), row(LANES), row(LANES), const((1, ne))],
        out_shape=[jax.ShapeDtypeStruct((n, d), F32),
                   jax.ShapeDtypeStruct((n_all, d // 2), jnp.uint32),
                   jax.ShapeDtypeStruct((n, LANES), I32), jax.ShapeDtypeStruct((n, LANES), F32),
                   jax.ShapeDtypeStruct((n, LANES), I32), jax.ShapeDtypeStruct((1, ne), F32)],
        scratch_shapes=[pltpu.VMEM((1, ne), F32)],
        input_output_aliases={len(args) - 1: 1} if has_prev else {},
        compiler_params=_cparams(1),
        name="outproj_router",
    )(*args)


def _moe_body(be_ref, rb_ref, nch_ref, nrow_ref, tok0_ref, tokn_ref, h2_hbm,
              w1g_ref, w1l_ref, b1g_ref, b1l_ref, w2_ref, b2_ref,
              o_ref, xp, act, sem, *, nf1):
    del be_ref, rb_ref
    b = pl.program_id(0)
    s = pl.program_id(1)
    n_chunks = nch_ref[b]

    def issue_rows(tok_ref, n_rows):
        def body(i, carry):
            r0 = pl.multiple_of(i * 8, 8)
            for u in range(8):
                pltpu.make_async_copy(h2_hbm.at[pl.ds(tok_ref[0, r0 + u], 1), :],
                                      xp.at[pl.ds(r0 + u, 1), :], sem.at[0]).start()
            return carry

        lax.fori_loop(0, n_rows // 8, body, 0)

    def wait_rows(n_rows):
        size = 8
        while size <= MOE_TB:
            @pl.when((n_rows & size) != 0)
            def _(size=size):
                pltpu.make_async_copy(h2_hbm.at[pl.ds(0, size), :], xp.at[pl.ds(0, size), :],
                                      sem.at[0]).wait()

            size *= 2

    @pl.when((b == 0) & (s == 0))
    def _():
        xp[...] = jnp.zeros_like(xp)
        issue_rows(tok0_ref, nrow_ref[0])

    @pl.when(s == 0)
    def _():
        wait_rows(nrow_ref[b])

    @pl.when(s == nf1)
    def _():
        issue_rows(tokn_ref, nrow_ref[b + 1])

    def chunk_groups(one):
        assert MOE_TB // MOE_CH < 8
        for size in (4, 2, 1):
            @pl.when((n_chunks & size) != 0)
            def _(size=size):
                r0 = (n_chunks & ~(2 * size - 1)) * MOE_CH
                one(pl.multiple_of(r0, size * MOE_CH), size * MOE_CH)

    @pl.when((s < nf1) & (n_chunks > 0))
    def _():
        def one(r0, rows):
            x = jnp.concatenate(_unpack_bf16_pair(xp[pl.ds(r0, rows), :]), axis=1)
            hg = jnp.dot(x, w1g_ref[...].astype(BF16), preferred_element_type=F32)
            hl = jnp.dot(x, w1l_ref[...].astype(BF16), preferred_element_type=F32)
            gl = jnp.minimum(hg + b1g_ref[...], SWIGLU_LIMIT)
            ln = jnp.clip(hl + b1l_ref[...], -SWIGLU_LIMIT, SWIGLU_LIMIT)
            a = gl * _sigmoid(SWIGLU_ALPHA * gl) * (ln + 1.0)
            act[s, pl.ds(r0, rows), :] = a.astype(BF16)

        chunk_groups(one)

    @pl.when((s >= nf1) & (n_chunks > 0))
    def _():
        def one(r0, rows):
            a = jnp.concatenate([act[k, pl.ds(r0, rows), :] for k in range(nf1)], axis=1)
            o_ref[pl.ds(r0, rows), :] = (
                jnp.dot(a, w2_ref[...].astype(BF16), preferred_element_type=F32) + b2_ref[...])

        chunk_groups(one)

        def zero(i, carry):
            r0 = pl.multiple_of(i * MOE_CH, MOE_CH)
            o_ref[pl.ds(r0, MOE_CH), :] = jnp.zeros((MOE_CH, o_ref.shape[1]), F32)
            return carry

        lax.fori_loop(n_chunks, MOE_TB // MOE_CH, zero, 0)


def _moe_experts(h2_all, slot_tok, w1, b1, w2, b2, blk_expert, blk_rows, blk_chunks, blk_nrow,
                 n_grid_blocks):
    ne, d, f2 = w1.shape
    f = f2 // 2
    nf1 = f // MOE_TF
    nf2 = d // MOE_TN
    nb = slot_tok.shape[0]

    def s1(b, s, nch):
        return jnp.where(nch[b] > 0, jnp.minimum(s, nf1 - 1), nf1 - 1)

    def s2(b, s, nch):
        return jnp.where(nch[b] > 0, jnp.maximum(s - nf1, 0), nf2 - 1)

    def w2_index(b, s, be, rb, nch, nr):
        early = (s < nf1 - 1) & (b > 0)
        e = jnp.where(early, be[jnp.maximum(b - 1, 0)], be[b])
        return e, 0, jnp.where(early, nf2 - 1, s2(b, s, nch))

    grid_spec = pltpu.PrefetchScalarGridSpec(
        num_scalar_prefetch=4,
        grid=(n_grid_blocks, nf1 + nf2),
        in_specs=[
            pl.BlockSpec((None, 1, MOE_TB), lambda b, s, be, rb, nch, nr: (0, 0, 0),
                         memory_space=pltpu.SMEM),
            pl.BlockSpec((None, 1, MOE_TB),
                         lambda b, s, be, rb, nch, nr: (jnp.minimum(b + 1, nb - 1), 0, 0),
                         memory_space=pltpu.SMEM),
            pl.BlockSpec(memory_space=pl.ANY),
            pl.BlockSpec((None, d, MOE_TF), lambda b, s, be, rb, nch, nr: (be[b], 0, s1(b, s, nch))),
            pl.BlockSpec((None, d, MOE_TF),
                         lambda b, s, be, rb, nch, nr: (be[b], 0, nf1 + s1(b, s, nch))),
            pl.BlockSpec((None, 1, MOE_TF), lambda b, s, be, rb, nch, nr: (be[b], 0, s1(b, s, nch))),
            pl.BlockSpec((None, 1, MOE_TF),
                         lambda b, s, be, rb, nch, nr: (be[b], 0, nf1 + s1(b, s, nch))),
            pl.BlockSpec((None, f, MOE_TN), w2_index),
            pl.BlockSpec((None, 1, MOE_TN), w2_index),
        ],
        out_specs=pl.BlockSpec((MOE_TB, MOE_TN),
                               lambda b, s, be, rb, nch, nr: (rb[b], s2(b, s, nch))),
        scratch_shapes=[pltpu.VMEM((MOE_TB, d // 2), jnp.uint32),
                        pltpu.VMEM((nf1, MOE_TB, MOE_TF), BF16),
                        pltpu.SemaphoreType.DMA((1,))],
    )
    return pl.pallas_call(
        functools.partial(_moe_body, nf1=nf1),
        grid_spec=grid_spec,
        out_shape=jax.ShapeDtypeStruct((nb * MOE_TB, d), F32),
        compiler_params=_cparams(2),
        name="moe_experts",
    )(blk_expert, blk_rows, blk_chunks, blk_nrow, slot_tok, slot_tok, h2_all,
      w1, w1, b1.reshape(ne, 1, f2), b1.reshape(ne, 1, f2), w2, b2.reshape(ne, 1, d))


def _moe_plan(counts, top_i, rank, n_tokens, n_blocks):
    ne = counts.shape[0]
    nblk = (counts + MOE_TB - 1) // MOE_TB
    blk_end = jnp.cumsum(nblk)
    blk_start = blk_end - nblk
    n_used = blk_end[-1]
    dest = blk_start[top_i] * MOE_TB + rank
    bidx = jnp.arange(n_blocks + 1, dtype=I32)
    be = jnp.minimum(jnp.searchsorted(blk_end, bidx, side='right'), ne - 1).astype(I32)
    active = bidx < n_used
    valid = jnp.where(active, jnp.clip(counts[be] - (bidx - blk_start[be]) * MOE_TB, 0, MOE_TB), 0)
    chunks = ((valid + MOE_CH - 1) // MOE_CH).astype(I32)
    nrow = ((valid + 7) // 8 * 8).astype(I32)
    last = jnp.maximum(n_used - 1, 0)
    be = jnp.where(active, be, be[last]).astype(I32)
    rows = jnp.where(active, bidx, last).astype(I32)
    tok = jnp.arange(n_tokens * TOP_K, dtype=I32) // TOP_K
    slot_tok = jnp.zeros((n_blocks * MOE_TB,), I32).at[dest.reshape(-1)].set(
        tok, unique_indices=True, mode='promise_in_bounds')
    return dest, slot_tok.reshape(n_blocks, 1, MOE_TB), be, rows, chunks, nrow, n_used


def _combine_body(d0_ref, dn_ref, x1_ref, gate_ref, gt_ref, gf_ref, ys_hbm, o_ref, ybuf, sem):
    i = pl.program_id(0)
    tm = x1_ref.shape[0]
    slot = lax.rem(i, 2)

    def issue_rows(dest_ref, sl):
        def body(i, carry):
            r0 = pl.multiple_of(i * 8, 8)
            for u in range(8):
                for k in range(TOP_K):
                    pltpu.make_async_copy(
                        ys_hbm.at[pl.ds(dest_ref[0, (r0 + u) * TOP_K + k], 1), :],
                        ybuf.at[sl, k, pl.ds(r0 + u, 1), :], sem.at[sl]).start()
            return carry

        lax.fori_loop(0, tm // 8, body, 0)

    @pl.when(i == 0)
    def _():
        issue_rows(d0_ref, 0)

    @pl.when(i + 1 < pl.num_programs(0))
    def _():
        issue_rows(dn_ref, 1 - slot)

    pltpu.make_async_copy(ybuf.at[slot], ybuf.at[slot], sem.at[slot]).wait()
    gates = gate_ref[...]
    y = gates[:, 0:1] * ybuf[slot, 0]
    for k in range(1, TOP_K):
        y = y + gates[:, k:k + 1] * ybuf[slot, k]
    x2 = x1_ref[...] + _mod_rows(gt_ref, tm) * y
    ms = jnp.mean(x2 * x2, axis=-1, keepdims=True)
    o_ref[...] = x2 * lax.rsqrt(ms + EPS) * gf_ref[...]


def _combine(x1, dest, ys, gates, mod, per_row, rows_per_batch, g_final, tm):
    n, d = x1.shape
    nt = n // tm
    dest3 = dest.reshape(nt, 1, tm * TOP_K)
    return pl.pallas_call(
        _combine_body,
        grid=(nt,),
        in_specs=[pl.BlockSpec((None, 1, tm * TOP_K), lambda i: (0, 0, 0), memory_space=pltpu.SMEM),
                  pl.BlockSpec((None, 1, tm * TOP_K), lambda i: (jnp.minimum(i + 1, nt - 1), 0, 0),
                               memory_space=pltpu.SMEM),
                  pl.BlockSpec((tm, d), lambda i: (i, 0)),
                  pl.BlockSpec((tm, LANES), lambda i: (i, 0)),
                  _mod_spec(per_row, tm, rows_per_batch, d, 5),
                  pl.BlockSpec((1, d), lambda i: (0, 0)),
                  pl.BlockSpec(memory_space=pl.ANY)],
        out_specs=pl.BlockSpec((tm, d), lambda i: (i, 0)),
        out_shape=jax.ShapeDtypeStruct((n, d), F32),
        scratch_shapes=[pltpu.VMEM((2, TOP_K, tm, d), F32), pltpu.SemaphoreType.DMA((2,))],
        compiler_params=_cparams(1),
        name="combine_norm",
    )(dest3, dest3, x1, gates, mod, g_final.reshape(1, d), ys)


def kernel(x_prompt, x_sample, c_prompt, c_sample, state_conv, state_ssm_re, state_ssm_im, g_mix, g_ffn, w_mod, b_mod, w_in, conv_w, ssm_a_re, ssm_a_im, ssm_log_dt, ssm_b_re, ssm_b_im, ssm_c_re, ssm_c_im, ssm_d, glu_w, glu_b, w_out, w_router, b_router, w1, b1, w2, b2, g_final):
    bp, lp, d = x_prompt.shape
    bs, ls, _ = x_sample.shape
    depth = g_mix.shape[0]
    dc = conv_w.shape[-1]
    n_grp, p_st, ch = ssm_b_re.shape[1:]
    ne = w_router.shape[-1]
    np_, ns_ = bp * lp, bs * ls
    n_tok = np_ + ns_
    assert ls == S5_T // 2 and lp % S5_T == 0 and conv_w.shape[1] == 3
    nc = lp // S5_T
    n_blocks = -(-(n_tok * TOP_K) // MOE_TB) + ne

    xp = x_prompt.reshape(np_, d)
    xs = x_sample.reshape(ns_, d)
    c_all = jnp.concatenate([c_prompt, c_sample], axis=0)
    pad = (-c_all.shape[0]) % 8
    c_all = jnp.pad(c_all, ((0, pad), (0, 0)))

    assert depth == 1
    outs = [[] for _ in range(6)]
    for l in range(depth):
        m = _modulation(c_all, w_mod[l], b_mod[l])
        mod_p = m[:bp].reshape(bp, 1, N_MOD * d)
        mod_s = m[bp:bp + bs]

        w_in_bf = w_in[l].astype(BF16)
        yc_p, u_p, tail_p = _inproj_conv_prompt(xp, mod_p, lp, g_mix[l], w_in_bf, conv_w[l], 512)
        e_s = jnp.pad(state_conv[l], ((0, 0), (0, ls - 2), (0, 0))).reshape(ns_, dc)
        yc_s, u_s, z_s = _inproj_conv_sample(xs, mod_s, g_mix[l], w_in_bf, conv_w[l], e_s, ls, 256)
        new_conv_p = tail_p[:, 6:8, :]
        new_conv_s = z_s.reshape(bs, ls, dc)[:, ls - 2:, :]

        ops = _s5_operators(ssm_a_re[l], ssm_a_im[l], ssm_log_dt[l], ssm_b_re[l], ssm_b_im[l],
                            ssm_c_re[l], ssm_c_im[l], ssm_d[l], glu_w[l], glu_b[l])
        h0 = jnp.concatenate([state_ssm_re[l], state_ssm_im[l]], axis=-1).transpose(1, 0, 2)
        ys_p, hf_p, ys_s, hf_s = _s5(u_p, u_s, h0, ops, bp, nc, ls)
        new_re_p = hf_p[:, :, :p_st].transpose(1, 0, 2)
        new_im_p = hf_p[:, :, p_st:].transpose(1, 0, 2)
        new_re_s = hf_s[:, :, :p_st].transpose(1, 0, 2)
        new_im_s = hf_s[:, :, p_st:].transpose(1, 0, 2)

        wo_bf = w_out[l].astype(BF16)
        wr_bf = w_router[l].astype(BF16)
        cnt0 = jnp.zeros((1, ne), F32)
        x1_p, h2_all, ti_p, gate_p, rank_p, cnt1 = _outproj(
            cnt0, xp, yc_p, ys_p, wo_bf, mod_p, False, lp, g_ffn[l], wr_bf, b_router[l],
            None, n_tok, 0, 2 * OUT_SUB)
        x1_s, h2_all, ti_s, gate_s, rank_s, cnt2 = _outproj(
            cnt1, xs, yc_s, ys_s, wo_bf, mod_s, True, ls, g_ffn[l], wr_bf, b_router[l],
            h2_all, n_tok, np_, OUT_SUB)

        counts = cnt2[0].astype(I32)
        top_i = jnp.concatenate([ti_p[:, :TOP_K], ti_s[:, :TOP_K]], axis=0)
        rank = jnp.concatenate([rank_p[:, :TOP_K], rank_s[:, :TOP_K]], axis=0)
        dest, slot_tok, blk_e, blk_rows, blk_chunks, blk_nrow, n_used = _moe_plan(
            counts, top_i, rank, n_tok, n_blocks)
        y_sorted = _moe_experts(h2_all, slot_tok, w1[l], b1[l], w2[l], b2[l],
                                blk_e, blk_rows, blk_chunks, blk_nrow, n_used)

        xp = _combine(x1_p, dest[:np_], y_sorted, gate_p, mod_p, False, lp, g_final, 256)
        xs = _combine(x1_s, dest[np_:], y_sorted, gate_s, mod_s, True, ls, g_final, 128)
        for lst, val in zip(outs, (new_conv_p, new_re_p, new_im_p, new_conv_s, new_re_s, new_im_s)):
            lst.append(val)

    y_prompt = xp.reshape(bp, lp, d)
    y_sample = xs.reshape(bs, ls, d)
    return (y_prompt, y_sample) + tuple(jnp.stack(o) for o in outs)
```
